```python
import math
import jax, jax.numpy as jnp
from jax import lax
import numpy as np

D_MODEL = 2048
BATCH = 2
SEQ = 4096
DEPTH = 4
DEC_BATCH = 8
DEC_SEQ = 4
PAST_LEN = 16384
PAGE_SIZE = 128

HEAD_DIM = 128
CONV_DIM = D_MODEL // 4
CONV_WIDTH = 31
RET_HEADS = D_MODEL // 512
RET_QK_DIM = 128
RET_V_DIM = 2 * RET_QK_DIM
RET_CHUNK = 128
DIL_GROUPS = ((128, 1), (512, 4), (2048, 16))
DIL_HEADS = D_MODEL // 512
DIL_WIDTH = DIL_HEADS * HEAD_DIM
ATTN_BLOCK = 128
N_BRANCH = 3
D_FF = 4 * D_MODEL
ROPE_THETA = 10000.0
EPS = 1e-6
IN_SIZES = ((2 * CONV_DIM,
             RET_HEADS * RET_QK_DIM, RET_HEADS * RET_QK_DIM, RET_HEADS * RET_V_DIM, RET_HEADS * RET_V_DIM)
            + (DIL_WIDTH,) * (3 * len(DIL_GROUPS))
            + (D_MODEL,) * N_BRANCH)
IN_SPLITS = tuple(int(s) for s in np.cumsum(IN_SIZES)[:-1])
N_IN = sum(IN_SIZES)

kernel_name = 'hybrid_conv_retention_dilated_decoder_step'


def _rms_norm(x, g):
    xf = x.astype(jnp.float32)
    y = xf * lax.rsqrt(jnp.mean(xf * xf, axis=-1, keepdims=True) + EPS)
    return (y * g.astype(jnp.float32)).astype(x.dtype)


def _layer_norm(x, g, b):
    xf = x.astype(jnp.float32)
    xc = xf - jnp.mean(xf, axis=-1, keepdims=True)
    y = xc * lax.rsqrt(jnp.mean(xc * xc, axis=-1, keepdims=True) + EPS)
    return (y * g.astype(jnp.float32) + b.astype(jnp.float32)).astype(x.dtype)


def _rope_inv_freq(dim):
    return 1.0 / (ROPE_THETA ** (jnp.arange(0, dim, 2, dtype=jnp.float32) / dim))


def _retnet_inv_freq(dim):
    return 1.0 / (ROPE_THETA ** jnp.linspace(0.0, 1.0, dim // 2, dtype=jnp.float32))


def _rotate(x, pos, inv_freq):
    ang = pos.astype(jnp.float32)[:, None] * inv_freq[None, :]
    cos = jnp.cos(ang)[None, :, None, :]
    sin = jnp.sin(ang)[None, :, None, :]
    x1, x2 = jnp.split(x.astype(jnp.float32), 2, axis=-1)
    return jnp.concatenate([x1 * cos - x2 * sin, x2 * cos + x1 * sin], axis=-1).astype(x.dtype)


def _conv_module(u2, buf, conv_w, conv_b, ln_g, ln_b):
    a, b = jnp.split(u2, 2, axis=-1)
    u = a * jax.nn.sigmoid(b)
    full = jnp.concatenate([buf.astype(u.dtype), u], axis=1)
    y = lax.conv_general_dilated(full, conv_w[:, None, :].astype(u.dtype), window_strides=(1,),
                                 padding='VALID', dimension_numbers=('NWC', 'WIO', 'NWC'),
                                 feature_group_count=CONV_DIM)
    y = jax.nn.silu(_layer_norm(y + conv_b, ln_g, ln_b))
    return y, full[:, -(CONV_WIDTH - 1):]


def _retention(q, k, v, s0):
    B, T, H, DK = q.shape
    DV = v.shape[-1]
    L = math.gcd(T, RET_CHUNK)
    n = T // L
    lg = jnp.log(1.0 - 2.0 ** (-5.0 - jnp.arange(H, dtype=jnp.float32)))
    i = jnp.arange(L, dtype=jnp.float32)
    rel = i[:, None] - i[None, :]
    dmask = jnp.where(rel[None] >= 0, jnp.exp(jnp.maximum(rel, 0.0)[None] * lg[:, None, None]), 0.0)
    q_dec = jnp.exp((i[:, None] + 1.0) * lg[None, :])[None, :, :, None]
    k_dec = jnp.exp((L - 1.0 - i[:, None]) * lg[None, :])[None, :, :, None]
    chunk_dec = jnp.exp(L * lg)[None, :, None, None]

    def to_chunks(t):
        return jnp.moveaxis(t.astype(jnp.float32).reshape(B, n, L, H, t.shape[-1]), 1, 0)

    def step(S, inp):
        qc, kc, vc = inp
        sc = jnp.einsum('bihd,bjhd->bhij', qc, kc) * dmask[None]
        o = (jnp.einsum('bhij,bjhe->bihe', sc, vc)
             + jnp.einsum('bihd,bhde->bihe', qc * q_dec, S))
        S = S * chunk_dec + jnp.einsum('bjhd,bjhe->bhde', kc * k_dec, vc)
        return S, o

    S, o = lax.scan(step, s0.astype(jnp.float32), (to_chunks(q), to_chunks(k), to_chunks(v)))
    o = jnp.moveaxis(o, 0, 1).reshape(B, T, H, DV)
    return o, S.astype(s0.dtype)


def _retention_branch(rq, rk, rv, rg, pos, s0):
    B, T, _ = rq.shape
    inv = _retnet_inv_freq(RET_QK_DIM)
    q = _rotate(rq.reshape(B, T, RET_HEADS, RET_QK_DIM), pos, inv) * RET_QK_DIM ** -0.5
    k = _rotate(rk.reshape(B, T, RET_HEADS, RET_QK_DIM), pos, inv)
    v = rv.reshape(B, T, RET_HEADS, RET_V_DIM)
    o, s_new = _retention(q, k, v, s0)
    o = o * lax.rsqrt(jnp.mean(o * o, axis=-1, keepdims=True) + EPS)
    y = o.reshape(B, T, RET_HEADS * RET_V_DIM) * jax.nn.silu(rg.astype(jnp.float32))
    return y.astype(rq.dtype), s_new


def _dilated_attention(q, k, v, q_off, dil, n_back):
    B, Tq, H, D = q.shape
    blk = math.gcd(Tq, ATTN_BLOCK)
    nb = Tq // blk
    steps = jnp.arange(n_back + 1) * dil

    def block(args):
        qb, b0 = args
        idx = q_off + b0 + jnp.arange(blk)[:, None] - steps[None, :]
        valid = idx >= 0
        idx = jnp.maximum(idx, 0)
        kb = jnp.take(k, idx, axis=1)
        vb = jnp.take(v, idx, axis=1)
        s = jnp.einsum('bqhd,bqkhd->bqhk', qb, kb, preferred_element_type=jnp.float32)
        s = jnp.where(valid[None, :, None, :], s, -jnp.inf)
        m = jnp.max(s, axis=-1, keepdims=True)
        p = jnp.exp(s - m)
        den = jnp.sum(p, axis=-1, keepdims=True)
        o = jnp.einsum('bqhk,bqkhd->bqhd', (p / den).astype(v.dtype), vb)
        return o, (m + jnp.log(den))[..., 0]

    qbs = jnp.moveaxis(q.reshape(B, nb, blk, H, D), 1, 0)
    o, lse = lax.map(block, (qbs, jnp.arange(nb) * blk))
    o = jnp.moveaxis(o, 0, 1).reshape(B, Tq, H, D)
    lse = jnp.moveaxis(lse, 0, 1).reshape(B, Tq, H)
    return o, lse


def _dilated_branch(dil_qkv, pos, win_bufs, win_lens):
    B, T, _ = dil_qkv[0].shape
    inv = _rope_inv_freq(HEAD_DIM)
    outs, lses, new_bufs = [], [], []
    for g, (window, dil) in enumerate(DIL_GROUPS):
        q, k, v = [t.reshape(B, T, DIL_HEADS, HEAD_DIM) for t in dil_qkv[3 * g:3 * g + 3]]
        q = _rotate(q, pos, inv) * HEAD_DIM ** -0.5
        k = _rotate(k, pos, inv)
        buf = win_bufs[g]
        kk = jnp.concatenate([buf[:, :, 0].astype(k.dtype), k], axis=1)
        vv = jnp.concatenate([buf[:, :, 1].astype(v.dtype), v], axis=1)
        o, lse = _dilated_attention(q, kk, vv, buf.shape[1], dil, window // dil)
        outs.append(o)
        lses.append(lse)
        n_keep = win_lens[g]
        new_bufs.append(jnp.stack([kk[:, -n_keep:], vv[:, -n_keep:]], axis=2))
    w = jax.nn.softmax(jnp.stack(lses, axis=0), axis=0)
    y = jnp.einsum('gbth,gbthd->bthd', w.astype(outs[0].dtype), jnp.stack(outs, axis=0))
    return y.reshape(B, T, DIL_HEADS * HEAD_DIM), tuple(new_bufs)


def _layer(x, c, pos, conv_buf, ret_state, win_bufs, win_lens,
           norm_mix_g, norm_mix_post_g, norm_ffn_g, norm_ffn_post_g, w_mod, b_mod, w_in,
           conv_w, conv_b, conv_ln_g, conv_ln_b, w_branch_a, w_branch_b, w_branch_c,
           w_out, w_up, w_down):
    mod = jax.nn.silu(c) @ w_mod + b_mod
    sh1, sc1, g1, sh2, sc2, g2 = jnp.split(mod[:, None, :], 6, axis=-1)
    h = _rms_norm(x, norm_mix_g) * (1.0 + sc1) + sh1
    parts = jnp.split(h @ w_in, IN_SPLITS, axis=-1)
    n_dil = 3 * len(DIL_GROUPS)
    ya, conv_new = _conv_module(parts[0], conv_buf, conv_w, conv_b, conv_ln_g, conv_ln_b)
    yb, ret_new = _retention_branch(parts[1], parts[2], parts[3], parts[4], pos, ret_state)
    yc, win_new = _dilated_branch(parts[5:5 + n_dil], pos, win_bufs, win_lens)
    ga, gb, gc = parts[5 + n_dil:]
    merged = (jax.nn.sigmoid(ga) * (ya @ w_branch_a)
              + jax.nn.sigmoid(gb) * (yb @ w_branch_b)
              + jax.nn.sigmoid(gc) * (yc @ w_branch_c))
    x = x + g1 * _rms_norm(merged @ w_out, norm_mix_post_g)
    h2 = _rms_norm(x, norm_ffn_g) * (1.0 + sc2) + sh2
    f = jnp.square(jax.nn.relu(h2 @ w_up)) @ w_down
    x = x + g2 * _rms_norm(f, norm_ffn_post_g)
    return x, conv_new, ret_new, win_new


def setup_inputs(seed: int = 0) -> dict:
    key = jax.random.key(seed)
    ks = jax.random.split(key, 32)
    f32 = jnp.float32
    d = D_MODEL

    def nrm(k, shape, scale):
        return jax.random.normal(k, shape, f32) * scale

    def win_shape(g):
        return (DEPTH, DEC_BATCH, min(DIL_GROUPS[g][0], PAST_LEN), 2, DIL_HEADS, HEAD_DIM)

    return {
        'x_prompt': nrm(ks[0], (BATCH, SEQ, d), 1.0),
        'x_sample': nrm(ks[1], (DEC_BATCH, DEC_SEQ, d), 1.0),
        'c_prompt': nrm(ks[2], (BATCH, d), 1.0),
        'c_sample': nrm(ks[3], (DEC_BATCH, d), 1.0),
        'state_conv': nrm(ks[4], (DEPTH, DEC_BATCH, CONV_WIDTH - 1, CONV_DIM), 0.5),
        'state_retention': nrm(ks[5], (DEPTH, DEC_BATCH, RET_HEADS, RET_QK_DIM, RET_V_DIM), 4.0),
        'cache_win128': nrm(ks[6], win_shape(0), 1.0),
        'cache_win512': nrm(ks[7], win_shape(1), 1.0),
        'cache_win2048': nrm(ks[8], win_shape(2), 1.0),
        'norm_mix_g': 1.0 + nrm(ks[9], (DEPTH, d), 0.05),
        'norm_mix_post_g': 1.0 + nrm(ks[10], (DEPTH, d), 0.05),
        'norm_ffn_g': 1.0 + nrm(ks[11], (DEPTH, d), 0.05),
        'norm_ffn_post_g': 1.0 + nrm(ks[12], (DEPTH, d), 0.05),
        'w_mod': nrm(ks[13], (DEPTH, d, 6 * d), 0.5 * d ** -0.5),
        'b_mod': nrm(ks[14], (DEPTH, 6 * d), 0.02),
        'w_in': nrm(ks[15], (DEPTH, d, N_IN), d ** -0.5),
        'conv_w': nrm(ks[16], (DEPTH, CONV_WIDTH, CONV_DIM), CONV_WIDTH ** -0.5),
        'conv_b': nrm(ks[17], (DEPTH, CONV_DIM), 0.02),
        'conv_ln_g': 1.0 + nrm(ks[18], (DEPTH, CONV_DIM), 0.05),
        'conv_ln_b': nrm(ks[19], (DEPTH, CONV_DIM), 0.02),
        'w_branch_a': nrm(ks[20], (DEPTH, CONV_DIM, d), CONV_DIM ** -0.5),
        'w_branch_b': nrm(ks[21], (DEPTH, RET_HEADS * RET_V_DIM, d), (RET_HEADS * RET_V_DIM) ** -0.5),
        'w_branch_c': nrm(ks[22], (DEPTH, DIL_WIDTH, d), DIL_WIDTH ** -0.5),
        'w_out': nrm(ks[23], (DEPTH, d, d), d ** -0.5),
        'w_up': nrm(ks[24], (DEPTH, d, D_FF), d ** -0.5),
        'w_down': nrm(ks[25], (DEPTH, D_FF, d), D_FF ** -0.5),
    }


def reference(x_prompt, x_sample, c_prompt, c_sample, state_conv, state_retention,
              cache_win128, cache_win512, cache_win2048,
              norm_mix_g, norm_mix_post_g, norm_ffn_g, norm_ffn_post_g, w_mod, b_mod, w_in,
              conv_w, conv_b, conv_ln_g, conv_ln_b, w_branch_a, w_branch_b, w_branch_c,
              w_out, w_up, w_down):
    xp, xs = x_prompt, x_sample
    bp, tp = xp.shape[0], xp.shape[1]
    pos_p = jnp.arange(tp)
    pos_s = PAST_LEN + jnp.arange(xs.shape[1])
    conv0 = jnp.zeros((bp, CONV_WIDTH - 1, CONV_DIM), xp.dtype)
    ret0 = jnp.zeros((bp, RET_HEADS, RET_QK_DIM, RET_V_DIM), xp.dtype)
    win0 = tuple(jnp.zeros((bp, 0, 2, DIL_HEADS, HEAD_DIM), xp.dtype) for _ in DIL_GROUPS)
    lens_p = tuple(min(w, tp) for w, _ in DIL_GROUPS)
    win_caches = (cache_win128, cache_win512, cache_win2048)
    lens_s = tuple(cw.shape[2] for cw in win_caches)

    conv_p, conv_s, ret_p, ret_s = [], [], [], []
    win_p = [[] for _ in DIL_GROUPS]
    win_s = [[] for _ in DIL_GROUPS]
    for l in range(DEPTH):
        lw = (norm_mix_g[l], norm_mix_post_g[l], norm_ffn_g[l], norm_ffn_post_g[l], w_mod[l], b_mod[l],
              w_in[l], conv_w[l], conv_b[l], conv_ln_g[l], conv_ln_b[l], w_branch_a[l], w_branch_b[l],
              w_branch_c[l], w_out[l], w_up[l], w_down[l])
        xp, cp, rp, wp = _layer(xp, c_prompt, pos_p, conv0, ret0, win0, lens_p, *lw)
        xs, cs, rs, wsm = _layer(xs, c_sample, pos_s, state_conv[l], state_retention[l],
                                 tuple(cw[l] for cw in win_caches), lens_s, *lw)
        conv_p.append(cp)
        conv_s.append(cs)
        ret_p.append(rp)
        ret_s.append(rs)
        for g in range(len(DIL_GROUPS)):
            win_p[g].append(wp[g])
            win_s[g].append(wsm[g])

    conv_prompt = jnp.stack(conv_p)
    conv_sample = jnp.stack(conv_s)
    ret_prompt = jnp.stack(ret_p)
    ret_sample = jnp.stack(ret_s)
    win128_prompt = jnp.stack(win_p[0])
    win128_sample = jnp.stack(win_s[0])
    win512_prompt = jnp.stack(win_p[1])
    win512_sample = jnp.stack(win_s[1])
    win2048_prompt = jnp.stack(win_p[2])
    win2048_sample = jnp.stack(win_s[2])
    return (xp, xs, conv_prompt, conv_sample, ret_prompt, ret_sample,
            win128_prompt, win128_sample, win512_prompt, win512_sample,
            win2048_prompt, win2048_sample)
```

```python
import functools
import math

import numpy as np
import jax
import jax.numpy as jnp
from jax import lax
from jax.experimental import pallas as pl
from jax.experimental.pallas import tpu as pltpu

F32 = jnp.float32
BF16 = jnp.bfloat16

D_MODEL = 2048
PAST_LEN = 16384
HEAD_DIM = 128
CONV_DIM = D_MODEL // 4
CONV_WIDTH = 31
CONV_HALO = 32
N_HEADS = D_MODEL // 512
RET_QK = 128
RET_V = 256
RET_CHUNK = 128
DIL_GROUPS = ((128, 1), (512, 4), (2048, 16))
ATTN_BLOCK = 128
D_FF = 4 * D_MODEL
ROPE_THETA = 10000.0
EPS = 1e-6
HW = N_HEADS * HEAD_DIM
RV_W = N_HEADS * RET_V

OFF_CONV_A = 0
OFF_CONV_B = CONV_DIM
OFF_RQ = 2 * CONV_DIM
OFF_RK = OFF_RQ + HW
OFF_RV = OFF_RK + HW
OFF_RG = OFF_RV + RV_W
OFF_DIL = OFF_RG + RV_W
OFF_GATE = OFF_DIL + 3 * len(DIL_GROUPS) * HW
N_IN = OFF_GATE + 3 * D_MODEL

NEG = -1e30
VMEM_LIMIT = 52 * 1024 * 1024


def _params(*sem):
    return pltpu.CompilerParams(dimension_semantics=sem, vmem_limit_bytes=VMEM_LIMIT)


def _silu(x):
    return x * jax.nn.sigmoid(x)


def _rot(x, cos2, sin2):
    return x * cos2 + pltpu.roll(x, HEAD_DIM // 2, axis=1) * sin2


def _dot(a, b):
    return jnp.dot(a, b, preferred_element_type=F32)


def _dot_nt(a, b):
    return lax.dot_general(a, b, (((1,), (1,)), ((), ())), preferred_element_type=F32)


def _dot_tn(a, b):
    return lax.dot_general(a, b, (((0,), (0,)), ((), ())), preferred_element_type=F32)


def _rope_tables(pos, inv_freq, rows):
    ang = np.asarray(pos, np.float64)[:, None] * inv_freq[None, :]
    cos = np.concatenate([np.cos(ang), np.cos(ang)], axis=-1)
    sin = np.concatenate([-np.sin(ang), np.sin(ang)], axis=-1)
    pad = rows - cos.shape[0]
    if pad:
        cos = np.pad(cos, ((0, pad), (0, 0)))
        sin = np.pad(sin, ((0, pad), (0, 0)))
    return jnp.asarray(cos, F32), jnp.asarray(sin, F32)


def _rope_inv_freq():
    return 1.0 / (ROPE_THETA ** (np.arange(0, HEAD_DIM, 2, dtype=np.float64) / HEAD_DIM))


def _retnet_inv_freq():
    return 1.0 / (ROPE_THETA ** np.linspace(0.0, 1.0, RET_QK // 2, dtype=np.float64))


def _ret_decay_tables(rows, lp):
    lg = np.log(1.0 - 2.0 ** (-5.0 - np.arange(N_HEADS, dtype=np.float64)))
    i = np.arange(lp, dtype=np.float64)
    rel = i[:, None] - i[None, :]
    real = (i < rows)
    dmask = np.where((rel >= 0) & real[:, None] & real[None, :],
                     np.exp(np.maximum(rel, 0.0)[None] * lg[:, None, None]), 0.0)
    qdec = np.where(real[None, :], np.exp((i[None, :] + 1.0) * lg[:, None]), 0.0)
    kdec = np.where(real[None, :], np.exp((rows - 1.0 - i[None, :]) * lg[:, None]), 0.0)
    cdec = tuple(float(v) for v in np.exp(rows * lg))
    bl = lambda a: jnp.asarray(np.broadcast_to(a[:, :, None], (N_HEADS, lp, HEAD_DIM)), F32)
    return jnp.asarray(dmask, F32), bl(qdec), bl(kdec), cdec


class _Rows:
    def __init__(self, m, tm, rows_per_batch):
        self.m = m
        self.tm = tm
        self.per_row = rows_per_batch is None
        self.tiles_per_b = None if self.per_row else rows_per_batch // tm

    def mod_spec(self):
        if self.per_row:
            return pl.BlockSpec((1, self.tm, D_MODEL), lambda i, *_: (0, i, 0))
        tpb = self.tiles_per_b
        return pl.BlockSpec((1, 1, D_MODEL), lambda i, *_: (i // tpb, 0, 0))


def _mod_kernel(c_ref, w_ref, b_ref, o_ref):
    c = c_ref[...]
    s = _silu(c).astype(BF16)
    o_ref[0] = _dot(s, w_ref[0].astype(BF16)) + b_ref[0]


def _modulation(c_all, w_mod, b_mod):
    depth, d, n = w_mod.shape
    rows = c_all.shape[0]
    tn = 1024
    return pl.pallas_call(
        _mod_kernel,
        grid=(depth, n // tn),
        in_specs=[pl.BlockSpec((rows, d), lambda l, j: (0, 0)),
                  pl.BlockSpec((1, d, tn), lambda l, j: (l, 0, j)),
                  pl.BlockSpec((1, 1, tn), lambda l, j: (l, 0, j))],
        out_specs=pl.BlockSpec((1, rows, tn), lambda l, j: (l, 0, j)),
        out_shape=jax.ShapeDtypeStruct((depth, rows, n), F32),
        compiler_params=_params("parallel", "parallel"),
    )(c_all, w_mod, b_mod.reshape(depth, 1, n))


def _norm_mod(x, g, sc, sh):
    y = x * lax.rsqrt(jnp.mean(x * x, axis=-1, keepdims=True) + EPS) * g
    return y * (1.0 + sc) + sh


def _prenorm_kernel(x_ref, g_ref, sc_ref, sh_ref, h_ref):
    h_ref[...] = _norm_mod(x_ref[...], g_ref[0], sc_ref[0], sh_ref[0]).astype(BF16)


def _prenorm(x, g, l, sc, sh, rows):
    m, d = x.shape
    tm = rows.tm
    return pl.pallas_call(
        _prenorm_kernel,
        grid=(m // tm,),
        in_specs=[pl.BlockSpec((tm, d), lambda i: (i, 0)),
                  pl.BlockSpec((1, 1, d), lambda i: (l, 0, 0)),
                  rows.mod_spec(), rows.mod_spec()],
        out_specs=pl.BlockSpec((tm, d), lambda i: (i, 0)),
        out_shape=jax.ShapeDtypeStruct((m, d), BF16),
        compiler_params=_params("parallel"),
    )(x, g, sc, sh)


def _mm_kernel(x_ref, w_ref, o_ref, *, act):
    y = _dot(x_ref[...], w_ref[...])
    if act == "relu2":
        y = jnp.square(jnp.maximum(y, 0.0))
    o_ref[...] = y.astype(o_ref.dtype)


def _matmul(x, w, l, out_dtype, tm, tn, act=None):
    m, k = x.shape
    n = w.shape[-1]
    return pl.pallas_call(
        functools.partial(_mm_kernel, act=act),
        grid=(m // tm, n // tn),
        in_specs=[pl.BlockSpec((tm, k), lambda i, j: (i, 0)),
                  pl.BlockSpec((None, k, tn), lambda i, j: (l, 0, j))],
        out_specs=pl.BlockSpec((tm, tn), lambda i, j: (i, j)),
        out_shape=jax.ShapeDtypeStruct((m, n), out_dtype),
        compiler_params=_params("parallel", "parallel"),
    )(x, w)


def _conv_ln_silu(full_ref, first, rows, w_ref, cb, lg, lb):
    acc = jnp.zeros((rows, CONV_DIM), F32)
    for j in range(CONV_WIDTH):
        acc = acc + full_ref[first + j:first + j + rows, :] * w_ref[j:j + 1, :]
    y = acc + cb
    yc = y - jnp.mean(y, axis=-1, keepdims=True)
    z = yc * lax.rsqrt(jnp.mean(yc * yc, axis=-1, keepdims=True) + EPS) * lg + lb
    return _silu(z)


def _conv_p_kernel(a_ref, b_ref, ha_ref, hb_ref, w_ref, cb_ref, lg_ref, lb_ref, ya_ref, st_ref, full_ref,
                   *, tt, rc):
    t = pl.program_id(1)
    u = a_ref[...] * jax.nn.sigmoid(b_ref[...])
    uh = ha_ref[...] * jax.nn.sigmoid(hb_ref[...])
    full_ref[0:CONV_HALO, :] = jnp.where(t > 0, uh, 0.0)
    full_ref[CONV_HALO:, :] = u
    st_ref[...] = u[tt - CONV_HALO:, :]
    skip = CONV_HALO - (CONV_WIDTH - 1)
    for r0 in range(0, tt, rc):
        y = _conv_ln_silu(full_ref, skip + r0, rc, w_ref, cb_ref[...], lg_ref[...], lb_ref[...])
        ya_ref[r0:r0 + rc, :] = y.astype(BF16)


def _conv_prompt(proj3, l, conv_w, conv_b, ln_g, ln_b):
    b, t, _ = proj3.shape
    tt, rc = 256, 32
    hpt = tt // CONV_HALO
    vec = pl.BlockSpec((None, 1, CONV_DIM), lambda i, j: (l, 0, 0))
    return pl.pallas_call(
        functools.partial(_conv_p_kernel, tt=tt, rc=rc),
        grid=(b, t // tt),
        in_specs=[pl.BlockSpec((None, tt, CONV_DIM), lambda i, j: (i, j, 0)),
                  pl.BlockSpec((None, tt, CONV_DIM), lambda i, j: (i, j, 1)),
                  pl.BlockSpec((None, CONV_HALO, CONV_DIM), lambda i, j: (i, jnp.maximum(j * hpt - 1, 0), 0)),
                  pl.BlockSpec((None, CONV_HALO, CONV_DIM), lambda i, j: (i, jnp.maximum(j * hpt - 1, 0), 1)),
                  pl.BlockSpec((None, CONV_WIDTH, CONV_DIM), lambda i, j: (l, 0, 0)),
                  vec, vec, vec],
        out_specs=[pl.BlockSpec((None, tt, CONV_DIM), lambda i, j: (i, j, 0)),
                   pl.BlockSpec((None, CONV_HALO, CONV_DIM), lambda i, j: (i, 0, 0))],
        out_shape=[jax.ShapeDtypeStruct((b, t, CONV_DIM), BF16),
                   jax.ShapeDtypeStruct((b, CONV_HALO, CONV_DIM), F32)],
        scratch_shapes=[pltpu.VMEM((tt + CONV_HALO, CONV_DIM), F32)],
        compiler_params=_params("parallel", "arbitrary"),
    )(proj3, proj3, proj3, proj3, conv_w, conv_b, ln_g, ln_b)


def _conv_s_kernel(a_ref, b_ref, st_in_ref, w_ref, cb_ref, lg_ref, lb_ref, ya_ref, st_ref, full_ref, *, t):
    keep = CONV_WIDTH - 1
    u = a_ref[...] * jax.nn.sigmoid(b_ref[...])
    full_ref[0:keep, :] = st_in_ref[...]
    full_ref[keep:keep + t, :] = u
    st_ref[...] = full_ref[t:t + keep, :]
    y = _conv_ln_silu(full_ref, 0, t, w_ref, cb_ref[...], lg_ref[...], lb_ref[...])
    ya_ref[...] = y.astype(BF16)


def _conv_sample(proj3, l, state_conv, conv_w, conv_b, ln_g, ln_b):
    b, t, _ = proj3.shape
    keep = CONV_WIDTH - 1
    vec = pl.BlockSpec((None, 1, CONV_DIM), lambda i: (l, 0, 0))
    return pl.pallas_call(
        functools.partial(_conv_s_kernel, t=t),
        grid=(b,),
        in_specs=[pl.BlockSpec((None, t, CONV_DIM), lambda i: (i, 0, 0)),
                  pl.BlockSpec((None, t, CONV_DIM), lambda i: (i, 0, 1)),
                  pl.BlockSpec((None, None, keep, CONV_DIM), lambda i: (l, i, 0, 0)),
                  pl.BlockSpec((None, CONV_WIDTH, CONV_DIM), lambda i: (l, 0, 0)),
                  vec, vec, vec],
        out_specs=[pl.BlockSpec((None, t, CONV_DIM), lambda i: (i, 0, 0)),
                   pl.BlockSpec((None, keep, CONV_DIM), lambda i: (i, 0, 0))],
        out_shape=[jax.ShapeDtypeStruct((b, t, CONV_DIM), BF16),
                   jax.ShapeDtypeStruct((b, keep, CONV_DIM), F32)],
        scratch_shapes=[pltpu.VMEM((keep + 8 + t, CONV_DIM), F32)],
        compiler_params=_params("parallel"),
    )(proj3, proj3, state_conv, conv_w, conv_b, ln_g, ln_b)


def _ret_kernel(*refs, rows, lp, has_s0, cdec, n_chunks):
    if has_s0:
        rq, rk, rv, rg, cos, sin, dm, qd, kd, s0, yb, sout, s_ref, qp, kp, vp, gp = refs
    else:
        rq, rk, rv, rg, cos, sin, dm, qd, kd, yb, sout, s_ref = refs
    c = pl.program_id(1)

    @pl.when(c == 0)
    def _():
        if has_s0:
            s_ref[...] = s0[...]
        else:
            s_ref[...] = jnp.zeros_like(s_ref)

    if rows < lp:
        for pad, src in ((qp, rq), (kp, rk), (vp, rv), (gp, rg)):
            pad[...] = jnp.zeros_like(pad)
            pad[0:rows, :] = src[...]
        rq, rk, rv, rg = qp, kp, vp, gp

    cos2 = cos[...]
    sin2 = sin[...]
    scale = RET_QK ** -0.5
    for h in range(N_HEADS):
        qs = slice(h * RET_QK, (h + 1) * RET_QK)
        vs = slice(h * RET_V, (h + 1) * RET_V)
        q = _rot(rq[:, qs], cos2, sin2) * scale
        k = _rot(rk[:, qs], cos2, sin2)
        v = rv[:, vs].astype(BF16)
        sc = _dot_nt(q.astype(BF16), k.astype(BF16)) * dm[h]
        s_h = s_ref[h]
        o = _dot(sc.astype(BF16), v) + _dot((q * qd[h]).astype(BF16), s_h.astype(BF16))
        s_ref[h] = s_h * cdec[h] + _dot_tn((k * kd[h]).astype(BF16), v)
        o = o * lax.rsqrt(jnp.mean(o * o, axis=-1, keepdims=True) + EPS)
        y = o * _silu(rg[:, vs])
        yb[:, vs] = y[0:rows, :].astype(BF16)

    @pl.when(c == n_chunks - 1)
    def _():
        sout[...] = s_ref[...]


def _retention(proj3, cos, sin, s0, l):
    b, t, _ = proj3.shape
    rows = math.gcd(t, RET_CHUNK)
    lp = max(rows, 16)
    n_chunks = t // rows
    dmask, qdec, kdec, cdec = _ret_decay_tables(rows, lp)
    has_s0 = s0 is not None
    assert has_s0 == (rows < lp)
    in_specs = [pl.BlockSpec((None, rows, HW), lambda i, c: (i, c, OFF_RQ // HW)),
                pl.BlockSpec((None, rows, HW), lambda i, c: (i, c, OFF_RK // HW)),
                pl.BlockSpec((None, rows, RV_W), lambda i, c: (i, c, OFF_RV // RV_W)),
                pl.BlockSpec((None, rows, RV_W), lambda i, c: (i, c, OFF_RG // RV_W)),
                pl.BlockSpec((lp, HEAD_DIM), lambda i, c: (c, 0)),
                pl.BlockSpec((lp, HEAD_DIM), lambda i, c: (c, 0)),
                pl.BlockSpec((N_HEADS, lp, lp), lambda i, c: (0, 0, 0)),
                pl.BlockSpec((N_HEADS, lp, HEAD_DIM), lambda i, c: (0, 0, 0)),
                pl.BlockSpec((N_HEADS, lp, HEAD_DIM), lambda i, c: (0, 0, 0))]
    args = [proj3, proj3, proj3, proj3, cos, sin, dmask, qdec, kdec]
    scratch = [pltpu.VMEM((N_HEADS, RET_QK, RET_V), F32)]
    if has_s0:
        in_specs.append(pl.BlockSpec((None, None, N_HEADS, RET_QK, RET_V), lambda i, c: (l, i, 0, 0, 0)))
        args.append(s0)
        scratch += [pltpu.VMEM((lp, HW), F32), pltpu.VMEM((lp, HW), F32),
                    pltpu.VMEM((lp, RV_W), F32), pltpu.VMEM((lp, RV_W), F32)]
    return pl.pallas_call(
        functools.partial(_ret_kernel, rows=rows, lp=lp, has_s0=has_s0, cdec=cdec, n_chunks=n_chunks),
        grid=(b, n_chunks),
        in_specs=in_specs,
        out_specs=[pl.BlockSpec((None, rows, RV_W), lambda i, c: (i, c, 0)),
                   pl.BlockSpec((None, N_HEADS, RET_QK, RET_V), lambda i, c: (i, 0, 0, 0))],
        out_shape=[jax.ShapeDtypeStruct((b, t, RV_W), BF16),
                   jax.ShapeDtypeStruct((b, N_HEADS, RET_QK, RET_V), F32)],
        scratch_shapes=scratch,
        compiler_params=_params("parallel", "arbitrary"),
    )(*args)


def _attn_p_kernel(q_ref, kc_ref, kp_ref, vc_ref, vp_ref, cc_ref, sc_ref, cp_ref, sp_ref,
                   o_ref, lse_ref, kr_ref, *, n_back):
    j = pl.program_id(2)
    blk = ATTN_BLOCK
    cos_c, sin_c, cos_p, sin_p = cc_ref[...], sc_ref[...], cp_ref[...], sp_ref[...]
    qi = lax.broadcasted_iota(jnp.int32, (blk, 2 * blk), 0)
    ki = lax.broadcasted_iota(jnp.int32, (blk, 2 * blk), 1)
    back = blk + qi - ki
    first_key = jnp.where(j > 0, 0, blk)
    valid = (back >= 0) & (back <= n_back) & (ki >= first_key)
    scale = HEAD_DIM ** -0.5
    for h in range(N_HEADS):
        hs = slice(h * HEAD_DIM, (h + 1) * HEAD_DIM)
        q = _rot(q_ref[:, hs], cos_c, sin_c) * scale
        k_cur = _rot(kc_ref[:, hs], cos_c, sin_c)
        k_prev = _rot(kp_ref[:, hs], cos_p, sin_p)
        kr_ref[:, hs] = k_cur
        keys = jnp.concatenate([k_prev, k_cur], axis=0).astype(BF16)
        vals = jnp.concatenate([vp_ref[:, hs], vc_ref[:, hs]], axis=0).astype(BF16)
        s = jnp.where(valid, _dot_nt(q.astype(BF16), keys), NEG)
        m = jnp.max(s, axis=-1, keepdims=True)
        p = jnp.exp(s - m)
        den = jnp.sum(p, axis=-1, keepdims=True)
        o_ref[:, hs] = _dot((p / den).astype(BF16), vals)
        lse_ref[:, hs] = jnp.broadcast_to(m + jnp.log(den), (blk, HEAD_DIM))


def _attn_prompt(proj3, g, cos, sin):
    window, dil = DIL_GROUPS[g]
    b, t, _ = proj3.shape
    ts = t // dil
    nq = ts // ATTN_BLOCK
    assert nq * ATTN_BLOCK * dil == t
    projv = proj3.reshape(b, ts, dil * N_IN)
    cosv = cos.reshape(ts, dil * HEAD_DIM)
    sinv = sin.reshape(ts, dil * HEAD_DIM)
    cpr = N_IN // HW
    qc, kc, vc = [(OFF_DIL + (3 * g + i) * HW) // HW for i in range(3)]
    blk = ATTN_BLOCK
    prev = lambda j: jnp.maximum(j - 1, 0)

    def col(c0, pj):
        return pl.BlockSpec((None, blk, HW), lambda i, r, j: (i, pj(j), r * cpr + c0))

    def tab(pj):
        return pl.BlockSpec((blk, HEAD_DIM), lambda i, r, j: (pj(j), r))

    cur = lambda j: j
    out_spec = pl.BlockSpec((None, blk, HW), lambda i, r, j: (i, j, r))
    out_shape = jax.ShapeDtypeStruct((b, ts, dil * HW), F32)
    o, lse, kr = pl.pallas_call(
        functools.partial(_attn_p_kernel, n_back=window // dil),
        grid=(b, dil, nq),
        in_specs=[col(qc, cur), col(kc, cur), col(kc, prev), col(vc, cur), col(vc, prev),
                  tab(cur), tab(cur), tab(prev), tab(prev)],
        out_specs=[out_spec, out_spec, out_spec],
        out_shape=[out_shape, out_shape, out_shape],
        compiler_params=_params("parallel", "parallel", "arbitrary"),
    )(projv, projv, projv, projv, projv, cosv, sinv, cosv, sinv)
    return o.reshape(b * t, HW), lse.reshape(b * t, HW), kr.reshape(b, t, HW)


def _combine_kernel(o0, o1, o2, l0, l1, l2, y_ref):
    a0, a1, a2 = l0[...], l1[...], l2[...]
    m = jnp.maximum(jnp.maximum(a0, a1), a2)
    e0, e1, e2 = jnp.exp(a0 - m), jnp.exp(a1 - m), jnp.exp(a2 - m)
    den = e0 + e1 + e2
    y = (e0 / den) * o0[...] + (e1 / den) * o1[...] + (e2 / den) * o2[...]
    y_ref[...] = y.astype(BF16)


def _combine(outs, lses, tm):
    m = outs[0].shape[0]
    spec = pl.BlockSpec((tm, HW), lambda i: (i, 0))
    return pl.pallas_call(
        _combine_kernel,
        grid=(m // tm,),
        in_specs=[spec] * 6,
        out_specs=spec,
        out_shape=jax.ShapeDtypeStruct((m, HW), BF16),
        compiler_params=_params("parallel"),
    )(*outs, *lses)


def _attn_s_kernel(*refs, t, lp, cache_rows):
    ng = len(DIL_GROUPS)
    qkv = refs[0:3 * ng]
    caches = refs[3 * ng:4 * ng]
    cos_ref, sin_ref = refs[4 * ng:4 * ng + 2]
    yc_ref = refs[4 * ng + 2]
    knew = refs[4 * ng + 3:5 * ng + 3]
    qp, kp, vp = refs[5 * ng + 3:5 * ng + 6]
    cos2, sin2 = cos_ref[...], sin_ref[...]
    scale = HEAD_DIM ** -0.5
    qi = lax.broadcasted_iota(jnp.int32, (lp, lp), 0)
    kn_i = lax.broadcasted_iota(jnp.int32, (lp, lp), 1)
    outs = [[None] * ng for _ in range(N_HEADS)]
    lses = [[None] * ng for _ in range(N_HEADS)]
    for g, (window, dil) in enumerate(DIL_GROUPS):
        lb = cache_rows[g]
        for pad, src in ((qp, qkv[3 * g]), (kp, qkv[3 * g + 1]), (vp, qkv[3 * g + 2])):
            pad[...] = jnp.zeros_like(pad)
            pad[0:t, :] = src[...]
        cache = caches[g]
        qc = lax.broadcasted_iota(jnp.int32, (lp, lb), 0)
        kc_i = lax.broadcasted_iota(jnp.int32, (lp, lb), 1)
        back_c = lb + qc - kc_i
        valid_c = ((back_c & (dil - 1)) == 0) & (back_c <= window)
        back_n = qi - kn_i
        valid_n = (back_n >= 0) & ((back_n & (dil - 1)) == 0) & (back_n <= window) & (kn_i < t)
        for h in range(N_HEADS):
            hs = slice(h * HEAD_DIM, (h + 1) * HEAD_DIM)
            vs = slice(HW + h * HEAD_DIM, HW + (h + 1) * HEAD_DIM)
            q = (_rot(qp[:, hs], cos2, sin2) * scale).astype(BF16)
            k_new = _rot(kp[:, hs], cos2, sin2)
            knew[g][:, hs] = k_new[0:t, :]
            s_c = jnp.where(valid_c, _dot_nt(q, cache[:, hs].astype(BF16)), NEG)
            s_n = jnp.where(valid_n, _dot_nt(q, k_new.astype(BF16)), NEG)
            m = jnp.maximum(jnp.max(s_c, axis=-1, keepdims=True), jnp.max(s_n, axis=-1, keepdims=True))
            p_c = jnp.exp(s_c - m)
            p_n = jnp.exp(s_n - m)
            den = jnp.sum(p_c, axis=-1, keepdims=True) + jnp.sum(p_n, axis=-1, keepdims=True)
            o = (_dot((p_c / den).astype(BF16), cache[:, vs].astype(BF16))
                 + _dot((p_n / den).astype(BF16), vp[:, hs].astype(BF16)))
            outs[h][g] = o
            lses[h][g] = m + jnp.log(den)
    for h in range(N_HEADS):
        hs = slice(h * HEAD_DIM, (h + 1) * HEAD_DIM)
        m = functools.reduce(jnp.maximum, lses[h])
        es = [jnp.exp(a - m) for a in lses[h]]
        den = functools.reduce(lambda a, c: a + c, es)
        y = functools.reduce(lambda a, c: a + c, [(e / den) * o for e, o in zip(es, outs[h])])
        yc_ref[:, hs] = y[0:t, :].astype(BF16)


def _attn_sample(proj3, caches, l, cos, sin):
    b, t, _ = proj3.shape
    lp = cos.shape[0]
    ng = len(DIL_GROUPS)
    cache_rows = tuple(c.shape[2] for c in caches)
    tok = lambda c0: pl.BlockSpec((None, t, HW), lambda i: (i, 0, c0))
    in_specs = [tok((OFF_DIL + n * HW) // HW) for n in range(3 * ng)]
    in_specs += [pl.BlockSpec((None, None, lb, 2 * HW), lambda i: (l, i, 0, 0)) for lb in cache_rows]
    tab = pl.BlockSpec((lp, HEAD_DIM), lambda i: (0, 0))
    in_specs += [tab, tab]
    tok_out = pl.BlockSpec((None, t, HW), lambda i: (i, 0, 0))
    res = pl.pallas_call(
        functools.partial(_attn_s_kernel, t=t, lp=lp, cache_rows=cache_rows),
        grid=(b,),
        in_specs=in_specs,
        out_specs=[tok_out] * (1 + ng),
        out_shape=[jax.ShapeDtypeStruct((b, t, HW), BF16)] + [jax.ShapeDtypeStruct((b, t, HW), F32)] * ng,
        scratch_shapes=[pltpu.VMEM((lp, HW), F32)] * 3,
        compiler_params=_params("parallel"),
    )(*([proj3] * (3 * ng)), *caches, cos, sin)
    return res[0], res[1:]


def _merge_kernel(ya, yb, yc, wa, wb, wc, ga, gb, gc, o_ref):
    acc = jax.nn.sigmoid(ga[...]) * _dot(ya[...], wa[...])
    acc = acc + jax.nn.sigmoid(gb[...]) * _dot(yb[...], wb[...])
    acc = acc + jax.nn.sigmoid(gc[...]) * _dot(yc[...], wc[...])
    o_ref[...] = acc.astype(BF16)


def _merge(ya, yb, yc, wa, wb, wc, proj, l, tm, tn):
    m = ya.shape[0]
    act = lambda w: pl.BlockSpec((tm, w), lambda i, j: (i, 0))
    wgt = lambda k: pl.BlockSpec((None, k, tn), lambda i, j: (l, 0, j))
    gate = lambda n: pl.BlockSpec((tm, tn), lambda i, j: (i, (OFF_GATE + n * D_MODEL) // tn + j))
    return pl.pallas_call(
        _merge_kernel,
        grid=(m // tm, D_MODEL // tn),
        in_specs=[act(CONV_DIM), act(RV_W), act(HW), wgt(CONV_DIM), wgt(RV_W), wgt(HW),
                  gate(0), gate(1), gate(2)],
        out_specs=pl.BlockSpec((tm, tn), lambda i, j: (i, j)),
        out_shape=jax.ShapeDtypeStruct((m, D_MODEL), BF16),
        compiler_params=_params("parallel", "parallel"),
    )(ya, yb, yc, wa, wb, wc, proj, proj, proj)


def _post_norm_residual(x, y, gpost, gate):
    r = y * lax.rsqrt(jnp.mean(y * y, axis=-1, keepdims=True) + EPS) * gpost
    return x + gate * r


def _outproj_kernel(m_ref, w_ref, x_ref, gpost_ref, g1_ref, gffn_ref, sc2_ref, sh2_ref, xo_ref, h2_ref):
    y = _dot(m_ref[...], w_ref[...])
    xn = _post_norm_residual(x_ref[...], y, gpost_ref[0], g1_ref[0])
    xo_ref[...] = xn
    h2_ref[...] = _norm_mod(xn, gffn_ref[0], sc2_ref[0], sh2_ref[0]).astype(BF16)


def _outproj(merged, w_out, l, x, gpost, g1, gffn, sc2, sh2, rows):
    m, d = x.shape
    tm = min(rows.tm, 256)
    sub = _Rows(m, tm, None if rows.per_row else rows.tiles_per_b * rows.tm)
    row = pl.BlockSpec((tm, d), lambda i: (i, 0))
    vec = pl.BlockSpec((1, 1, d), lambda i: (l, 0, 0))
    return pl.pallas_call(
        _outproj_kernel,
        grid=(m // tm,),
        in_specs=[row, pl.BlockSpec((None, d, d), lambda i: (l, 0, 0)), row, vec, sub.mod_spec(),
                  vec, sub.mod_spec(), sub.mod_spec()],
        out_specs=[row, row],
        out_shape=[jax.ShapeDtypeStruct((m, d), F32), jax.ShapeDtypeStruct((m, d), BF16)],
        compiler_params=_params("parallel"),
    )(merged, w_out, x, gpost, g1, gffn, sc2, sh2)


def _down_kernel(*refs, nk, with_next):
    if with_next:
        a_ref, w_ref, x_ref, gpost_ref, g2_ref, gn_ref, scn_ref, shn_ref, xo_ref, hn_ref, acc_ref = refs
    else:
        a_ref, w_ref, x_ref, gpost_ref, g2_ref, xo_ref, acc_ref = refs
    k = pl.program_id(1)

    @pl.when(k == 0)
    def _():
        acc_ref[...] = jnp.zeros_like(acc_ref)

    acc_ref[...] += _dot(a_ref[...], w_ref[...])

    @pl.when(k == nk - 1)
    def _():
        xn = _post_norm_residual(x_ref[...], acc_ref[...], gpost_ref[0], g2_ref[0])
        xo_ref[...] = xn
        if with_next:
            hn_ref[...] = _norm_mod(xn, gn_ref[0], scn_ref[0], shn_ref[0]).astype(BF16)


def _down(a, w_down, l, x, gpost, g2, nxt, rows):
    m, d = x.shape
    kdim = a.shape[1]
    tm = min(rows.tm, 512)
    tk = 512
    nk = kdim // tk
    sub = _Rows(m, tm, None if rows.per_row else rows.tiles_per_b * rows.tm)
    row = pl.BlockSpec((tm, d), lambda i, k: (i, 0))
    vec = pl.BlockSpec((1, 1, d), lambda i, k: (l, 0, 0))
    in_specs = [pl.BlockSpec((tm, tk), lambda i, k: (i, k)),
                pl.BlockSpec((None, tk, d), lambda i, k: (l, k, 0)),
                row, vec, sub.mod_spec()]
    args = [a, w_down, x, gpost, g2]
    out_specs = [row]
    out_shape = [jax.ShapeDtypeStruct((m, d), F32)]
    if nxt is not None:
        gn, scn, shn = nxt
        in_specs += [pl.BlockSpec((1, 1, d), lambda i, k: (l + 1, 0, 0)), sub.mod_spec(), sub.mod_spec()]
        args += [gn, scn, shn]
        out_specs.append(row)
        out_shape.append(jax.ShapeDtypeStruct((m, d), BF16))
    res = pl.pallas_call(
        functools.partial(_down_kernel, nk=nk, with_next=nxt is not None),
        grid=(m // tm, nk),
        in_specs=in_specs,
        out_specs=out_specs,
        out_shape=out_shape,
        scratch_shapes=[pltpu.VMEM((tm, d), F32)],
        compiler_params=_params("parallel", "arbitrary"),
    )(*args)
    return (res[0], res[1]) if nxt is not None else (res[0], None)


def kernel(x_prompt, x_sample, c_prompt, c_sample, state_conv, state_retention, cache_win128, cache_win512, cache_win2048, norm_mix_g, norm_mix_post_g, norm_ffn_g, norm_ffn_post_g, w_mod, b_mod, w_in, conv_w, conv_b, conv_ln_g, conv_ln_b, w_branch_a, w_branch_b, w_branch_c, w_out, w_up, w_down):
    depth = w_in.shape[0]
    bp, tp, d = x_prompt.shape
    bs, ts, _ = x_sample.shape
    mp, ms = bp * tp, bs * ts
    ng = len(DIL_GROUPS)

    w_in_b, w_a_b, w_b_b, w_c_b, w_out_b, w_up_b, w_down_b = [
        w.astype(BF16) for w in (w_in, w_branch_a, w_branch_b, w_branch_c, w_out, w_up, w_down)]
    vec3 = lambda a: a.reshape(depth, 1, a.shape[-1])
    g_mix, g_mix_post, g_ffn, g_ffn_post = map(vec3, (norm_mix_g, norm_mix_post_g, norm_ffn_g, norm_ffn_post_g))
    conv_b3, ln_g3, ln_b3 = map(vec3, (conv_b, conv_ln_g, conv_ln_b))
    caches = [c.reshape(c.shape[0], c.shape[1], c.shape[2], 2 * HW)
              for c in (cache_win128, cache_win512, cache_win2048)]

    lp_s = 16
    rope_p = _rope_tables(np.arange(tp), _rope_inv_freq(), tp)
    rope_s = _rope_tables(PAST_LEN + np.arange(ts), _rope_inv_freq(), lp_s)
    retn_p = _rope_tables(np.arange(tp), _retnet_inv_freq(), tp)
    retn_s = _rope_tables(PAST_LEN + np.arange(ts), _retnet_inv_freq(), lp_s)

    n_c = bp + bs
    c_rows = -(-n_c // 8) * 8
    c_all = jnp.concatenate([c_prompt, c_sample, jnp.zeros((c_rows - n_c, d), F32)], axis=0)
    mod = _modulation(c_all, w_mod, b_mod)

    rows_p = _Rows(mp, 1024, tp)
    rows_s = _Rows(ms, ms, None)

    def mods(l):
        parts_p = [mod[l, :bp, i * d:(i + 1) * d].reshape(bp, 1, d) for i in range(6)]
        parts_s = [jnp.repeat(mod[l, bp:n_c, i * d:(i + 1) * d], ts, axis=0).reshape(1, ms, d) for i in range(6)]
        return parts_p, parts_s

    xp = x_prompt.reshape(mp, d)
    xs = x_sample.reshape(ms, d)
    mod_p, mod_s = mods(0)
    hp = _prenorm(xp, g_mix, 0, mod_p[1], mod_p[0], rows_p)
    hs = _prenorm(xs, g_mix, 0, mod_s[1], mod_s[0], rows_s)

    conv_p, conv_s, ret_p, ret_s = [], [], [], []
    win_p = [[] for _ in range(ng)]
    new_kv_s = [[] for _ in range(ng)]

    for l in range(depth):
        sh1p, sc1p, g1p, sh2p, sc2p, g2p = mod_p
        sh1s, sc1s, g1s, sh2s, sc2s, g2s = mod_s

        proj = _matmul(hp, w_in_b, l, F32, 1024, 512)
        proj3 = proj.reshape(bp, tp, N_IN)
        ya, st = _conv_prompt(proj3, l, conv_w, conv_b3, ln_g3, ln_b3)
        conv_p.append(st[:, CONV_HALO - (CONV_WIDTH - 1):])
        yb, s_new = _retention(proj3, retn_p[0], retn_p[1], None, l)
        ret_p.append(s_new)
        outs, lses = [], []
        for g, (window, dil) in enumerate(DIL_GROUPS):
            o, lse, kr = _attn_prompt(proj3, g, rope_p[0], rope_p[1])
            outs.append(o)
            lses.append(lse)
            keep = min(window, tp)
            v_off = OFF_DIL + (3 * g + 2) * HW
            kv = jnp.stack([kr[:, tp - keep:], proj3[:, tp - keep:, v_off:v_off + HW]], axis=2)
            win_p[g].append(kv.reshape(bp, keep, 2, N_HEADS, HEAD_DIM))
        yc = _combine(outs, lses, 1024)
        merged = _merge(ya.reshape(mp, CONV_DIM), yb.reshape(mp, RV_W), yc, w_a_b, w_b_b, w_c_b, proj, l, 1024, 512)
        xp, h2 = _outproj(merged, w_out_b, l, xp, g_mix_post, g1p, g_ffn, sc2p, sh2p, rows_p)
        a = _matmul(h2, w_up_b, l, BF16, 1024, 512, act="relu2")

        proj_s = _matmul(hs, w_in_b, l, F32, ms, 512)
        proj_s3 = proj_s.reshape(bs, ts, N_IN)
        ya_s, st_s = _conv_sample(proj_s3, l, state_conv, conv_w, conv_b3, ln_g3, ln_b3)
        conv_s.append(st_s)
        yb_s, s_new_s = _retention(proj_s3, retn_s[0], retn_s[1], state_retention, l)
        ret_s.append(s_new_s)
        yc_s, k_new = _attn_sample(proj_s3, caches, l, rope_s[0], rope_s[1])
        for g in range(ng):
            v_off = OFF_DIL + (3 * g + 2) * HW
            kv = jnp.stack([k_new[g], proj_s3[:, :, v_off:v_off + HW]], axis=2)
            new_kv_s[g].append(kv.reshape(bs, ts, 2, N_HEADS, HEAD_DIM))
        merged_s = _merge(ya_s.reshape(ms, CONV_DIM), yb_s.reshape(ms, RV_W), yc_s.reshape(ms, HW),
                          w_a_b, w_b_b, w_c_b, proj_s, l, ms, 512)
        xs, h2_s = _outproj(merged_s, w_out_b, l, xs, g_mix_post, g1s, g_ffn, sc2s, sh2s, rows_s)
        a_s = _matmul(h2_s, w_up_b, l, BF16, ms, 512, act="relu2")

        if l + 1 < depth:
            mod_p_n, mod_s_n = mods(l + 1)
            xp, hp = _down(a, w_down_b, l, xp, g_ffn_post, g2p, (g_mix, mod_p_n[1], mod_p_n[0]), rows_p)
            xs, hs = _down(a_s, w_down_b, l, xs, g_ffn_post, g2s, (g_mix, mod_s_n[1], mod_s_n[0]), rows_s)
            mod_p, mod_s = mod_p_n, mod_s_n
        else:
            xp, _ = _down(a, w_down_b, l, xp, g_ffn_post, g2p, None, rows_p)
            xs, _ = _down(a_s, w_down_b, l, xs, g_ffn_post, g2s, None, rows_s)

    win_s = []
    for g, cache in enumerate((cache_win128, cache_win512, cache_win2048)):
        new = jnp.stack(new_kv_s[g])
        win_s.append(jnp.concatenate([cache[:, :, ts:], new], axis=2))

    return (xp.reshape(bp, tp, d), xs.reshape(bs, ts, d),
            jnp.stack(conv_p), jnp.stack(conv_s), jnp.stack(ret_p), jnp.stack(ret_s),
            jnp.stack(win_p[0]), win_s[0], jnp.stack(win_p[1]), win_s[1], jnp.stack(win_p[2]), win_s[2])
```

```python
import functools
import math

import numpy as np
import jax
import jax.numpy as jnp
from jax import lax
from jax.experimental import pallas as pl
from jax.experimental.pallas import tpu as pltpu

F32 = jnp.float32
BF16 = jnp.bfloat16

D_MODEL = 2048
PAST_LEN = 16384
HEAD_DIM = 128
CONV_DIM = D_MODEL // 4
CONV_WIDTH = 31
CONV_HALO = 32
N_HEADS = D_MODEL // 512
RET_QK = 128
RET_V = 256
RET_CHUNK = 128
DIL_GROUPS = ((128, 1), (512, 4), (2048, 16))
ATTN_BLOCK = 128
D_FF = 4 * D_MODEL
ROPE_THETA = 10000.0
EPS = 1e-6
HW = N_HEADS * HEAD_DIM
RV_W = N_HEADS * RET_V

OFF_CONV_A = 0
OFF_CONV_B = CONV_DIM
OFF_RQ = 2 * CONV_DIM
OFF_RK = OFF_RQ + HW
OFF_RV = OFF_RK + HW
OFF_RG = OFF_RV + RV_W
OFF_DIL = OFF_RG + RV_W
OFF_GATE = OFF_DIL + 3 * len(DIL_GROUPS) * HW
N_IN = OFF_GATE + 3 * D_MODEL

NEG = -1e30
VMEM_LIMIT = 52 * 1024 * 1024


def _params(*sem):
    return pltpu.CompilerParams(dimension_semantics=sem, vmem_limit_bytes=VMEM_LIMIT)


def _silu(x):
    return x * jax.nn.sigmoid(x)


def _rot(x, cos2, sin2):
    return x * cos2 + pltpu.roll(x, HEAD_DIM // 2, axis=1) * sin2


def _dot(a, b):
    return jnp.dot(a, b, preferred_element_type=F32)


def _dot_nt(a, b):
    return lax.dot_general(a, b, (((1,), (1,)), ((), ())), preferred_element_type=F32)


def _dot_tn(a, b):
    return lax.dot_general(a, b, (((0,), (0,)), ((), ())), preferred_element_type=F32)


def _rope_tables(pos, inv_freq, rows):
    ang = np.asarray(pos, np.float64)[:, None] * inv_freq[None, :]
    cos = np.concatenate([np.cos(ang), np.cos(ang)], axis=-1)
    sin = np.concatenate([-np.sin(ang), np.sin(ang)], axis=-1)
    pad = rows - cos.shape[0]
    if pad:
        cos = np.pad(cos, ((0, pad), (0, 0)))
        sin = np.pad(sin, ((0, pad), (0, 0)))
    return jnp.asarray(cos, F32), jnp.asarray(sin, F32)


def _rope_inv_freq():
    return 1.0 / (ROPE_THETA ** (np.arange(0, HEAD_DIM, 2, dtype=np.float64) / HEAD_DIM))


def _retnet_inv_freq():
    return 1.0 / (ROPE_THETA ** np.linspace(0.0, 1.0, RET_QK // 2, dtype=np.float64))


def _ret_decay_tables(rows, lp):
    lg = np.log(1.0 - 2.0 ** (-5.0 - np.arange(N_HEADS, dtype=np.float64)))
    i = np.arange(lp, dtype=np.float64)
    rel = i[:, None] - i[None, :]
    real = (i < rows)
    dmask = np.where((rel >= 0) & real[:, None] & real[None, :],
                     np.exp(np.maximum(rel, 0.0)[None] * lg[:, None, None]), 0.0)
    qdec = np.where(real[None, :], np.exp((i[None, :] + 1.0) * lg[:, None]), 0.0)
    kdec = np.where(real[None, :], np.exp((rows - 1.0 - i[None, :]) * lg[:, None]), 0.0)
    cdec = tuple(float(v) for v in np.exp(rows * lg))
    bl = lambda a: jnp.asarray(np.broadcast_to(a[:, :, None], (N_HEADS, lp, HEAD_DIM)), F32)
    return jnp.asarray(dmask, F32), bl(qdec), bl(kdec), cdec


class _Rows:
    def __init__(self, m, tm, rows_per_batch):
        self.m = m
        self.tm = tm
        self.per_row = rows_per_batch is None
        self.tiles_per_b = None if self.per_row else rows_per_batch // tm

    def mod_spec(self):
        if self.per_row:
            return pl.BlockSpec((1, self.tm, D_MODEL), lambda i, *_: (0, i, 0))
        tpb = self.tiles_per_b
        return pl.BlockSpec((1, 1, D_MODEL), lambda i, *_: (i // tpb, 0, 0))


def _mod_kernel(c_ref, w_ref, b_ref, o_ref):
    c = c_ref[...]
    s = _silu(c).astype(BF16)
    o_ref[0] = _dot(s, w_ref[0].astype(BF16)) + b_ref[0]


def _modulation(c_all, w_mod, b_mod):
    depth, d, n = w_mod.shape
    rows = c_all.shape[0]
    tn = 1024
    return pl.pallas_call(
        _mod_kernel,
        grid=(depth, n // tn),
        in_specs=[pl.BlockSpec((rows, d), lambda l, j: (0, 0)),
                  pl.BlockSpec((1, d, tn), lambda l, j: (l, 0, j)),
                  pl.BlockSpec((1, 1, tn), lambda l, j: (l, 0, j))],
        out_specs=pl.BlockSpec((1, rows, tn), lambda l, j: (l, 0, j)),
        out_shape=jax.ShapeDtypeStruct((depth, rows, n), F32),
        compiler_params=_params("parallel", "parallel"),
        name="modulation",
    )(c_all, w_mod, b_mod.reshape(depth, 1, n))


def _norm_mod(x, g, sc, sh):
    y = x * lax.rsqrt(jnp.mean(x * x, axis=-1, keepdims=True) + EPS) * g
    return y * (1.0 + sc) + sh


def _prenorm_kernel(x_ref, g_ref, sc_ref, sh_ref, h_ref):
    h_ref[...] = _norm_mod(x_ref[...], g_ref[0], sc_ref[0], sh_ref[0]).astype(BF16)


def _prenorm(x, g, l, sc, sh, rows):
    m, d = x.shape
    tm = rows.tm
    return pl.pallas_call(
        _prenorm_kernel,
        grid=(m // tm,),
        in_specs=[pl.BlockSpec((tm, d), lambda i: (i, 0)),
                  pl.BlockSpec((1, 1, d), lambda i: (l, 0, 0)),
                  rows.mod_spec(), rows.mod_spec()],
        out_specs=pl.BlockSpec((tm, d), lambda i: (i, 0)),
        out_shape=jax.ShapeDtypeStruct((m, d), BF16),
        compiler_params=_params("parallel"),
        name="prenorm",
    )(x, g, sc, sh)


def _mm_kernel(x_ref, w_ref, o_ref, *, act):
    y = _dot(x_ref[...], w_ref[...])
    if act == "relu2":
        y = jnp.square(jnp.maximum(y, 0.0))
    o_ref[...] = y.astype(o_ref.dtype)


def _matmul(x, w, l, out_dtype, tm, tn, act=None, col0=0, ncols=None, name="matmul"):
    m, k = x.shape
    n = w.shape[-1] if ncols is None else ncols
    c0 = col0 // tn
    assert c0 * tn == col0 and n % tn == 0
    return pl.pallas_call(
        functools.partial(_mm_kernel, act=act),
        grid=(m // tm, n // tn),
        in_specs=[pl.BlockSpec((tm, k), lambda i, j: (i, 0)),
                  pl.BlockSpec((None, k, tn), lambda i, j: (l, 0, c0 + j))],
        out_specs=pl.BlockSpec((tm, tn), lambda i, j: (i, j)),
        out_shape=jax.ShapeDtypeStruct((m, n), out_dtype),
        compiler_params=_params("parallel", "parallel"),
        name=name,
    )(x, w)


def _mm_heads_kernel(x_ref, w_ref, o_ref):
    y = _dot(x_ref[...], w_ref[...])
    for h in range(N_HEADS):
        o_ref[h] = y[:, h * HEAD_DIM:(h + 1) * HEAD_DIM]


def _matmul_heads(x, w, l, tm, col0, ncols):
    m, k = x.shape
    c0 = col0 // HW
    assert c0 * HW == col0 and ncols % HW == 0
    return pl.pallas_call(
        _mm_heads_kernel,
        grid=(m // tm, ncols // HW),
        in_specs=[pl.BlockSpec((tm, k), lambda i, j: (i, 0)),
                  pl.BlockSpec((None, k, HW), lambda i, j: (l, 0, c0 + j))],
        out_specs=pl.BlockSpec((N_HEADS, tm, HEAD_DIM), lambda i, j: (j, i, 0)),
        out_shape=jax.ShapeDtypeStruct((ncols // HEAD_DIM, m, HEAD_DIM), F32),
        compiler_params=_params("parallel", "parallel"),
        name="matmul_heads",
    )(x, w)


def _conv_ln_silu(full_ref, first, rows, w_ref, cb, lg, lb):
    acc = jnp.zeros((rows, CONV_DIM), F32)
    for j in range(CONV_WIDTH):
        acc = acc + full_ref[first + j:first + j + rows, :] * w_ref[j:j + 1, :]
    y = acc + cb
    yc = y - jnp.mean(y, axis=-1, keepdims=True)
    z = yc * lax.rsqrt(jnp.mean(yc * yc, axis=-1, keepdims=True) + EPS) * lg + lb
    return _silu(z)


def _conv_p_kernel(a_ref, b_ref, ha_ref, hb_ref, w_ref, cb_ref, lg_ref, lb_ref, ya_ref, st_ref, full_ref,
                   *, tt, rc):
    t = pl.program_id(1)
    u = a_ref[...] * jax.nn.sigmoid(b_ref[...])
    uh = ha_ref[...] * jax.nn.sigmoid(hb_ref[...])
    full_ref[0:CONV_HALO, :] = jnp.where(t > 0, uh, 0.0)
    full_ref[CONV_HALO:, :] = u
    st_ref[...] = u[tt - CONV_HALO:, :]
    skip = CONV_HALO - (CONV_WIDTH - 1)
    for r0 in range(0, tt, rc):
        y = _conv_ln_silu(full_ref, skip + r0, rc, w_ref, cb_ref[...], lg_ref[...], lb_ref[...])
        ya_ref[r0:r0 + rc, :] = y.astype(BF16)


def _conv_prompt(proj3, l, conv_w, conv_b, ln_g, ln_b):
    b, t, _ = proj3.shape
    tt, rc = 256, 32
    hpt = tt // CONV_HALO
    vec = pl.BlockSpec((None, 1, CONV_DIM), lambda i, j: (l, 0, 0))
    return pl.pallas_call(
        functools.partial(_conv_p_kernel, tt=tt, rc=rc),
        grid=(b, t // tt),
        in_specs=[pl.BlockSpec((None, tt, CONV_DIM), lambda i, j: (i, j, 0)),
                  pl.BlockSpec((None, tt, CONV_DIM), lambda i, j: (i, j, 1)),
                  pl.BlockSpec((None, CONV_HALO, CONV_DIM), lambda i, j: (i, jnp.maximum(j * hpt - 1, 0), 0)),
                  pl.BlockSpec((None, CONV_HALO, CONV_DIM), lambda i, j: (i, jnp.maximum(j * hpt - 1, 0), 1)),
                  pl.BlockSpec((None, CONV_WIDTH, CONV_DIM), lambda i, j: (l, 0, 0)),
                  vec, vec, vec],
        out_specs=[pl.BlockSpec((None, tt, CONV_DIM), lambda i, j: (i, j, 0)),
                   pl.BlockSpec((None, CONV_HALO, CONV_DIM), lambda i, j: (i, 0, 0))],
        out_shape=[jax.ShapeDtypeStruct((b, t, CONV_DIM), BF16),
                   jax.ShapeDtypeStruct((b, CONV_HALO, CONV_DIM), F32)],
        scratch_shapes=[pltpu.VMEM((tt + CONV_HALO, CONV_DIM), F32)],
        compiler_params=_params("parallel", "arbitrary"),
        name="conv_prompt",
    )(proj3, proj3, proj3, proj3, conv_w, conv_b, ln_g, ln_b)


def _conv_s_kernel(a_ref, b_ref, st_in_ref, w_ref, cb_ref, lg_ref, lb_ref, ya_ref, st_ref, full_ref, *, t):
    keep = CONV_WIDTH - 1
    u = a_ref[...] * jax.nn.sigmoid(b_ref[...])
    full_ref[0:keep, :] = st_in_ref[...]
    full_ref[keep:keep + t, :] = u
    st_ref[...] = full_ref[t:t + keep, :]
    y = _conv_ln_silu(full_ref, 0, t, w_ref, cb_ref[...], lg_ref[...], lb_ref[...])
    ya_ref[...] = y.astype(BF16)


def _conv_sample(proj3, l, state_conv, conv_w, conv_b, ln_g, ln_b):
    b, t, _ = proj3.shape
    keep = CONV_WIDTH - 1
    vec = pl.BlockSpec((None, 1, CONV_DIM), lambda i: (l, 0, 0))
    return pl.pallas_call(
        functools.partial(_conv_s_kernel, t=t),
        grid=(b,),
        in_specs=[pl.BlockSpec((None, t, CONV_DIM), lambda i: (i, 0, 0)),
                  pl.BlockSpec((None, t, CONV_DIM), lambda i: (i, 0, 1)),
                  pl.BlockSpec((None, None, keep, CONV_DIM), lambda i: (l, i, 0, 0)),
                  pl.BlockSpec((None, CONV_WIDTH, CONV_DIM), lambda i: (l, 0, 0)),
                  vec, vec, vec],
        out_specs=[pl.BlockSpec((None, t, CONV_DIM), lambda i: (i, 0, 0)),
                   pl.BlockSpec((None, keep, CONV_DIM), lambda i: (i, 0, 0))],
        out_shape=[jax.ShapeDtypeStruct((b, t, CONV_DIM), BF16),
                   jax.ShapeDtypeStruct((b, keep, CONV_DIM), F32)],
        scratch_shapes=[pltpu.VMEM((keep + 8 + t, CONV_DIM), F32)],
        compiler_params=_params("parallel"),
        name="conv_sample",
    )(proj3, proj3, state_conv, conv_w, conv_b, ln_g, ln_b)


def _ret_kernel(*refs, rows, lp, has_s0, cdec, n_chunks):
    if has_s0:
        rq, rk, rv, rg, cos, sin, dm, qd, kd, s0, yb, sout, s_ref, qp, kp, vp, gp = refs
    else:
        rq, rk, rv, rg, cos, sin, dm, qd, kd, yb, sout, s_ref = refs
    c = pl.program_id(1)

    @pl.when(c == 0)
    def _():
        if has_s0:
            s_ref[...] = s0[...]
        else:
            s_ref[...] = jnp.zeros_like(s_ref)

    if rows < lp:
        for pad, src in ((qp, rq), (kp, rk), (vp, rv), (gp, rg)):
            pad[...] = jnp.zeros_like(pad)
            pad[0:rows, :] = src[...]
        rq, rk, rv, rg = qp, kp, vp, gp

    cos2 = cos[...]
    sin2 = sin[...]
    scale = RET_QK ** -0.5
    for h in range(N_HEADS):
        qs = slice(h * RET_QK, (h + 1) * RET_QK)
        vs = slice(h * RET_V, (h + 1) * RET_V)
        q = _rot(rq[:, qs], cos2, sin2) * scale
        k = _rot(rk[:, qs], cos2, sin2)
        v = rv[:, vs].astype(BF16)
        sc = _dot_nt(q.astype(BF16), k.astype(BF16)) * dm[h]
        s_h = s_ref[h]
        o = _dot(sc.astype(BF16), v) + _dot((q * qd[h]).astype(BF16), s_h.astype(BF16))
        s_ref[h] = s_h * cdec[h] + _dot_tn((k * kd[h]).astype(BF16), v)
        o = o * lax.rsqrt(jnp.mean(o * o, axis=-1, keepdims=True) + EPS)
        y = o * _silu(rg[:, vs])
        yb[:, vs] = y[0:rows, :].astype(BF16)

    @pl.when(c == n_chunks - 1)
    def _():
        sout[...] = s_ref[...]


def _retention(proj3, cos, sin, s0, l):
    b, t, _ = proj3.shape
    rows = math.gcd(t, RET_CHUNK)
    lp = max(rows, 16)
    n_chunks = t // rows
    dmask, qdec, kdec, cdec = _ret_decay_tables(rows, lp)
    has_s0 = s0 is not None
    assert has_s0 == (rows < lp)
    in_specs = [pl.BlockSpec((None, rows, HW), lambda i, c: (i, c, OFF_RQ // HW)),
                pl.BlockSpec((None, rows, HW), lambda i, c: (i, c, OFF_RK // HW)),
                pl.BlockSpec((None, rows, RV_W), lambda i, c: (i, c, OFF_RV // RV_W)),
                pl.BlockSpec((None, rows, RV_W), lambda i, c: (i, c, OFF_RG // RV_W)),
                pl.BlockSpec((lp, HEAD_DIM), lambda i, c: (c, 0)),
                pl.BlockSpec((lp, HEAD_DIM), lambda i, c: (c, 0)),
                pl.BlockSpec((N_HEADS, lp, lp), lambda i, c: (0, 0, 0)),
                pl.BlockSpec((N_HEADS, lp, HEAD_DIM), lambda i, c: (0, 0, 0)),
                pl.BlockSpec((N_HEADS, lp, HEAD_DIM), lambda i, c: (0, 0, 0))]
    args = [proj3, proj3, proj3, proj3, cos, sin, dmask, qdec, kdec]
    scratch = [pltpu.VMEM((N_HEADS, RET_QK, RET_V), F32)]
    if has_s0:
        in_specs.append(pl.BlockSpec((None, None, N_HEADS, RET_QK, RET_V), lambda i, c: (l, i, 0, 0, 0)))
        args.append(s0)
        scratch += [pltpu.VMEM((lp, HW), F32), pltpu.VMEM((lp, HW), F32),
                    pltpu.VMEM((lp, RV_W), F32), pltpu.VMEM((lp, RV_W), F32)]
    return pl.pallas_call(
        functools.partial(_ret_kernel, rows=rows, lp=lp, has_s0=has_s0, cdec=cdec, n_chunks=n_chunks),
        grid=(b, n_chunks),
        in_specs=in_specs,
        out_specs=[pl.BlockSpec((None, rows, RV_W), lambda i, c: (i, c, 0)),
                   pl.BlockSpec((None, N_HEADS, RET_QK, RET_V), lambda i, c: (i, 0, 0, 0))],
        out_shape=[jax.ShapeDtypeStruct((b, t, RV_W), BF16),
                   jax.ShapeDtypeStruct((b, N_HEADS, RET_QK, RET_V), F32)],
        scratch_shapes=scratch,
        compiler_params=_params("parallel", "arbitrary"),
        name="retention",
    )(*args)


ATTN_ROWS = 2048
ROPE_CHUNK = 256


def _rows_at(start, size, stride):
    return pl.ds(start, size, stride=stride) if stride > 1 else pl.ds(start, size)


def _softmax_pv(q, keys, vals, valid):
    s = jnp.where(valid, _dot_nt(q, keys), NEG)
    m = jnp.max(s, axis=-1, keepdims=True)
    p = jnp.exp(s - m)
    den = jnp.sum(p, axis=-1, keepdims=True)
    return _dot((p / den).astype(BF16), vals), m + jnp.log(den)


def _attn_p_kernel(*refs):
    ng = len(DIL_GROUPS)
    qkv = refs[0:3 * ng]
    cos_ref, sin_ref = refs[3 * ng:3 * ng + 2]
    yc_ref = refs[3 * ng + 2]
    krs = refs[3 * ng + 3:4 * ng + 3]
    scr = refs[4 * ng + 3:]
    qrot, o_sc, lse_sc, kps, vps = [scr[n * ng:(n + 1) * ng] for n in range(5)]
    qs, ks, vs = qkv[0::3], qkv[1::3], qkv[2::3]
    i = pl.program_id(2)
    rows, blk = ATTN_ROWS, ATTN_BLOCK
    scale = HEAD_DIM ** -0.5

    @pl.when(i == 0)
    def _():
        for g in range(ng):
            kps[g][...] = jnp.zeros_like(kps[g])
            vps[g][...] = jnp.zeros_like(vps[g])

    for c in range(0, rows, ROPE_CHUNK):
        rs = slice(c, c + ROPE_CHUNK)
        cos2, sin2 = cos_ref[rs, :], sin_ref[rs, :]
        for g in range(ng):
            qrot[g][rs, :] = _rot(qs[g][rs, :], cos2, sin2) * scale
            krs[g][rs, :] = _rot(ks[g][rs, :], cos2, sin2)

    qi = lax.broadcasted_iota(jnp.int32, (blk, 2 * blk), 0)
    ki = lax.broadcasted_iota(jnp.int32, (blk, 2 * blk), 1)
    back = blk + qi - ki
    band = (back >= 0) & (back <= blk)
    band_first = band & (ki >= jnp.where(i > 0, 0, blk))

    for g, (window, dil) in enumerate(DIL_GROUPS):
        assert window // dil == blk
        n_sub = rows // (dil * blk)

        def residue(r, carry, g=g, dil=dil, n_sub=n_sub):
            for jb in range(n_sub):
                at = lambda j, n: _rows_at(r + j * blk * dil, n, dil)
                q = qrot[g][at(jb, blk), :].astype(BF16)
                if jb == 0:
                    keys = jnp.concatenate([kps[g][at(0, blk), :], krs[g][at(0, blk), :]], axis=0)
                    vals = jnp.concatenate([vps[g][at(0, blk), :], vs[g][at(0, blk), :]], axis=0)
                    valid = band_first
                else:
                    keys = krs[g][at(jb - 1, 2 * blk), :]
                    vals = vs[g][at(jb - 1, 2 * blk), :]
                    valid = band
                o, lse = _softmax_pv(q, keys.astype(BF16), vals.astype(BF16), valid)
                o_sc[g][at(jb, blk), :] = o
                lse_sc[g][at(jb, blk), :] = jnp.broadcast_to(lse, (blk, HEAD_DIM))
            return carry

        if dil == 1:
            residue(0, 0)
        else:
            lax.fori_loop(0, dil, residue, 0, unroll=2 if n_sub == 1 else 1)

    for g, (window, dil) in enumerate(DIL_GROUPS):
        keep = dil * blk
        kps[g][...] = krs[g][rows - keep:, :]
        vps[g][...] = vs[g][rows - keep:, :]

    for c in range(0, rows, ROPE_CHUNK):
        rs = slice(c, c + ROPE_CHUNK)
        ls = [lse_sc[g][rs, :] for g in range(ng)]
        m = functools.reduce(jnp.maximum, ls)
        es = [jnp.exp(a - m) for a in ls]
        den = functools.reduce(lambda a, e: a + e, es)
        y = functools.reduce(lambda a, e: a + e, [(es[g] / den) * o_sc[g][rs, :] for g in range(ng)])
        yc_ref[rs, :] = y.astype(BF16)


def _attn_prompt(qkv_h, cos, sin, b, t):
    ng = len(DIL_GROUPS)
    m = b * t
    rows = ATTN_ROWS
    nblk = t // rows
    assert nblk * rows == t

    def head(n):
        return pl.BlockSpec((None, rows, HEAD_DIM), lambda i, h, j: (n * N_HEADS + h, i * nblk + j, 0))

    tab = pl.BlockSpec((rows, HEAD_DIM), lambda i, h, j: (j, 0))
    keeps = [dil * ATTN_BLOCK for _, dil in DIL_GROUPS]
    scratch = ([pltpu.VMEM((rows, HEAD_DIM), F32)] * (3 * ng)
               + [pltpu.VMEM((k, HEAD_DIM), F32) for k in keeps] * 2)
    res = pl.pallas_call(
        _attn_p_kernel,
        grid=(b, N_HEADS, nblk),
        in_specs=[head(n) for n in range(3 * ng)] + [tab, tab],
        out_specs=[pl.BlockSpec((rows, HEAD_DIM), lambda i, h, j: (i * nblk + j, h))]
                  + [pl.BlockSpec((None, rows, HEAD_DIM), lambda i, h, j: (h, i * nblk + j, 0))] * ng,
        out_shape=[jax.ShapeDtypeStruct((m, HW), BF16)]
                  + [jax.ShapeDtypeStruct((N_HEADS, m, HEAD_DIM), F32)] * ng,
        scratch_shapes=scratch,
        compiler_params=_params("parallel", "parallel", "arbitrary"),
        name="attn_prompt",
    )(*([qkv_h] * (3 * ng)), cos, sin)
    return res[0], res[1:]


def _attn_s_kernel(*refs, t, lp, cache_rows):
    ng = len(DIL_GROUPS)
    qkv = refs[0:3 * ng]
    caches = refs[3 * ng:4 * ng]
    cos_ref, sin_ref = refs[4 * ng:4 * ng + 2]
    yc_ref = refs[4 * ng + 2]
    knew = refs[4 * ng + 3:5 * ng + 3]
    qp, kp, vp = refs[5 * ng + 3:5 * ng + 6]
    cos2, sin2 = cos_ref[...], sin_ref[...]
    scale = HEAD_DIM ** -0.5
    qi = lax.broadcasted_iota(jnp.int32, (lp, lp), 0)
    kn_i = lax.broadcasted_iota(jnp.int32, (lp, lp), 1)
    outs = [[None] * ng for _ in range(N_HEADS)]
    lses = [[None] * ng for _ in range(N_HEADS)]
    for g, (window, dil) in enumerate(DIL_GROUPS):
        lb = cache_rows[g]
        for pad, src in ((qp, qkv[3 * g]), (kp, qkv[3 * g + 1]), (vp, qkv[3 * g + 2])):
            pad[...] = jnp.zeros_like(pad)
            pad[0:t, :] = src[...]
        cache = caches[g]
        qc = lax.broadcasted_iota(jnp.int32, (lp, lb), 0)
        kc_i = lax.broadcasted_iota(jnp.int32, (lp, lb), 1)
        back_c = lb + qc - kc_i
        valid_c = ((back_c & (dil - 1)) == 0) & (back_c <= window)
        back_n = qi - kn_i
        valid_n = (back_n >= 0) & ((back_n & (dil - 1)) == 0) & (back_n <= window) & (kn_i < t)
        for h in range(N_HEADS):
            hs = slice(h * HEAD_DIM, (h + 1) * HEAD_DIM)
            q = (_rot(qp[:, hs], cos2, sin2) * scale).astype(BF16)
            k_new = _rot(kp[:, hs], cos2, sin2)
            knew[g][:, hs] = k_new[0:t, :]
            s_c = jnp.where(valid_c, _dot_nt(q, cache[:, 0, h, :].astype(BF16)), NEG)
            s_n = jnp.where(valid_n, _dot_nt(q, k_new.astype(BF16)), NEG)
            m = jnp.maximum(jnp.max(s_c, axis=-1, keepdims=True), jnp.max(s_n, axis=-1, keepdims=True))
            p_c = jnp.exp(s_c - m)
            p_n = jnp.exp(s_n - m)
            den = jnp.sum(p_c, axis=-1, keepdims=True) + jnp.sum(p_n, axis=-1, keepdims=True)
            o = (_dot((p_c / den).astype(BF16), cache[:, 1, h, :].astype(BF16))
                 + _dot((p_n / den).astype(BF16), vp[:, hs].astype(BF16)))
            outs[h][g] = o
            lses[h][g] = m + jnp.log(den)
    for h in range(N_HEADS):
        hs = slice(h * HEAD_DIM, (h + 1) * HEAD_DIM)
        m = functools.reduce(jnp.maximum, lses[h])
        es = [jnp.exp(a - m) for a in lses[h]]
        den = functools.reduce(lambda a, c: a + c, es)
        y = functools.reduce(lambda a, c: a + c, [(e / den) * o for e, o in zip(es, outs[h])])
        yc_ref[:, hs] = y[0:t, :].astype(BF16)


def _attn_sample(proj3, caches, l, cos, sin):
    b, t, _ = proj3.shape
    lp = cos.shape[0]
    ng = len(DIL_GROUPS)
    cache_rows = tuple(c.shape[2] for c in caches)
    tok = lambda c0: pl.BlockSpec((None, t, HW), lambda i: (i, 0, c0))
    in_specs = [tok((OFF_DIL + n * HW) // HW) for n in range(3 * ng)]
    in_specs += [pl.BlockSpec((None, None, lb, 2, N_HEADS, HEAD_DIM), lambda i: (l, i, 0, 0, 0, 0),
                              **({"pipeline_mode": pl.Buffered(1)} if lb == max(cache_rows) else {}))
                 for lb in cache_rows]
    tab = pl.BlockSpec((lp, HEAD_DIM), lambda i: (0, 0))
    in_specs += [tab, tab]
    tok_out = pl.BlockSpec((None, t, HW), lambda i: (i, 0, 0))
    res = pl.pallas_call(
        functools.partial(_attn_s_kernel, t=t, lp=lp, cache_rows=cache_rows),
        grid=(b,),
        in_specs=in_specs,
        out_specs=[tok_out] * (1 + ng),
        out_shape=[jax.ShapeDtypeStruct((b, t, HW), BF16)] + [jax.ShapeDtypeStruct((b, t, HW), F32)] * ng,
        scratch_shapes=[pltpu.VMEM((lp, HW), F32)] * 3,
        compiler_params=_params("parallel"),
        name="attn_sample",
    )(*([proj3] * (3 * ng)), *caches, cos, sin)
    return res[0], res[1:]


def _merge_kernel(ya, yb, yc, wa, wb, wc, ga, gb, gc, o_ref):
    acc = jax.nn.sigmoid(ga[...]) * _dot(ya[...], wa[...])
    acc = acc + jax.nn.sigmoid(gb[...]) * _dot(yb[...], wb[...])
    acc = acc + jax.nn.sigmoid(gc[...]) * _dot(yc[...], wc[...])
    o_ref[...] = acc.astype(BF16)


def _merge(ya, yb, yc, wa, wb, wc, gates, gate_off, l, tm, tn):
    m = ya.shape[0]
    act = lambda w: pl.BlockSpec((tm, w), lambda i, j: (i, 0))
    wgt = lambda k: pl.BlockSpec((None, k, tn), lambda i, j: (l, 0, j))
    gate = lambda n: pl.BlockSpec((tm, tn), lambda i, j: (i, (gate_off + n * D_MODEL) // tn + j))
    return pl.pallas_call(
        _merge_kernel,
        grid=(m // tm, D_MODEL // tn),
        in_specs=[act(CONV_DIM), act(RV_W), act(HW), wgt(CONV_DIM), wgt(RV_W), wgt(HW),
                  gate(0), gate(1), gate(2)],
        out_specs=pl.BlockSpec((tm, tn), lambda i, j: (i, j)),
        out_shape=jax.ShapeDtypeStruct((m, D_MODEL), BF16),
        compiler_params=_params("parallel", "parallel"),
        name="merge",
    )(ya, yb, yc, wa, wb, wc, gates, gates, gates)


def _post_norm_residual(x, y, gpost, gate):
    r = y * lax.rsqrt(jnp.mean(y * y, axis=-1, keepdims=True) + EPS) * gpost
    return x + gate * r


def _outproj_kernel(m_ref, w_ref, x_ref, gpost_ref, g1_ref, gffn_ref, sc2_ref, sh2_ref, xo_ref, h2_ref):
    y = _dot(m_ref[...], w_ref[...])
    xn = _post_norm_residual(x_ref[...], y, gpost_ref[0], g1_ref[0])
    xo_ref[...] = xn
    h2_ref[...] = _norm_mod(xn, gffn_ref[0], sc2_ref[0], sh2_ref[0]).astype(BF16)


def _outproj(merged, w_out, l, x, gpost, g1, gffn, sc2, sh2, rows):
    m, d = x.shape
    tm = min(rows.tm, 256)
    sub = _Rows(m, tm, None if rows.per_row else rows.tiles_per_b * rows.tm)
    row = pl.BlockSpec((tm, d), lambda i: (i, 0))
    vec = pl.BlockSpec((1, 1, d), lambda i: (l, 0, 0))
    return pl.pallas_call(
        _outproj_kernel,
        grid=(m // tm,),
        in_specs=[row, pl.BlockSpec((None, d, d), lambda i: (l, 0, 0)), row, vec, sub.mod_spec(),
                  vec, sub.mod_spec(), sub.mod_spec()],
        out_specs=[row, row],
        out_shape=[jax.ShapeDtypeStruct((m, d), F32), jax.ShapeDtypeStruct((m, d), BF16)],
        compiler_params=_params("parallel"),
        name="outproj",
    )(merged, w_out, x, gpost, g1, gffn, sc2, sh2)


def _down_kernel(*refs, nk, with_next):
    if with_next:
        a_ref, w_ref, x_ref, gpost_ref, g2_ref, gn_ref, scn_ref, shn_ref, xo_ref, hn_ref, acc_ref = refs
    else:
        a_ref, w_ref, x_ref, gpost_ref, g2_ref, xo_ref, acc_ref = refs
    k = pl.program_id(1)

    @pl.when(k == 0)
    def _():
        acc_ref[...] = jnp.zeros_like(acc_ref)

    acc_ref[...] += _dot(a_ref[...], w_ref[...])

    @pl.when(k == nk - 1)
    def _():
        xn = _post_norm_residual(x_ref[...], acc_ref[...], gpost_ref[0], g2_ref[0])
        xo_ref[...] = xn
        if with_next:
            hn_ref[...] = _norm_mod(xn, gn_ref[0], scn_ref[0], shn_ref[0]).astype(BF16)


def _down(a, w_down, l, x, gpost, g2, nxt, rows):
    m, d = x.shape
    kdim = a.shape[1]
    tm = min(rows.tm, 1024)
    tk = 512
    nk = kdim // tk
    sub = _Rows(m, tm, None if rows.per_row else rows.tiles_per_b * rows.tm)
    row = pl.BlockSpec((tm, d), lambda i, k: (i, 0))
    row_out = pl.BlockSpec((tm, d), lambda i, k: (i, 0), pipeline_mode=pl.Buffered(1))
    vec = pl.BlockSpec((1, 1, d), lambda i, k: (l, 0, 0))
    in_specs = [pl.BlockSpec((tm, tk), lambda i, k: (i, k)),
                pl.BlockSpec((None, tk, d), lambda i, k: (l, k, 0)),
                row, vec, sub.mod_spec()]
    args = [a, w_down, x, gpost, g2]
    out_specs = [row_out]
    out_shape = [jax.ShapeDtypeStruct((m, d), F32)]
    if nxt is not None:
        gn, scn, shn = nxt
        in_specs += [pl.BlockSpec((1, 1, d), lambda i, k: (l + 1, 0, 0)), sub.mod_spec(), sub.mod_spec()]
        args += [gn, scn, shn]
        out_specs.append(row_out)
        out_shape.append(jax.ShapeDtypeStruct((m, d), BF16))
    res = pl.pallas_call(
        functools.partial(_down_kernel, nk=nk, with_next=nxt is not None),
        grid=(m // tm, nk),
        in_specs=in_specs,
        out_specs=out_specs,
        out_shape=out_shape,
        scratch_shapes=[pltpu.VMEM((tm, d), F32)],
        compiler_params=_params("parallel", "arbitrary"),
        name="mlp_down",
    )(*args)
    return (res[0], res[1]) if nxt is not None else (res[0], None)


def kernel(x_prompt, x_sample, c_prompt, c_sample, state_conv, state_retention, cache_win128, cache_win512, cache_win2048, norm_mix_g, norm_mix_post_g, norm_ffn_g, norm_ffn_post_g, w_mod, b_mod, w_in, conv_w, conv_b, conv_ln_g, conv_ln_b, w_branch_a, w_branch_b, w_branch_c, w_out, w_up, w_down):
    depth = w_in.shape[0]
    bp, tp, d = x_prompt.shape
    bs, ts, _ = x_sample.shape
    mp, ms = bp * tp, bs * ts
    ng = len(DIL_GROUPS)

    w_in_b, w_a_b, w_b_b, w_c_b, w_out_b, w_up_b, w_down_b = [
        w.astype(BF16) for w in (w_in, w_branch_a, w_branch_b, w_branch_c, w_out, w_up, w_down)]
    vec3 = lambda a: a.reshape(depth, 1, a.shape[-1])
    g_mix, g_mix_post, g_ffn, g_ffn_post = map(vec3, (norm_mix_g, norm_mix_post_g, norm_ffn_g, norm_ffn_post_g))
    conv_b3, ln_g3, ln_b3 = map(vec3, (conv_b, conv_ln_g, conv_ln_b))
    caches = [cache_win128, cache_win512, cache_win2048]

    lp_s = 16
    rope_p = _rope_tables(np.arange(tp), _rope_inv_freq(), tp)
    rope_s = _rope_tables(PAST_LEN + np.arange(ts), _rope_inv_freq(), lp_s)
    retn_p = _rope_tables(np.arange(tp), _retnet_inv_freq(), tp)
    retn_s = _rope_tables(PAST_LEN + np.arange(ts), _retnet_inv_freq(), lp_s)

    n_c = bp + bs
    c_rows = -(-n_c // 8) * 8
    c_all = jnp.concatenate([c_prompt, c_sample, jnp.zeros((c_rows - n_c, d), F32)], axis=0)
    mod = _modulation(c_all, w_mod, b_mod)

    rows_p = _Rows(mp, 1024, tp)
    rows_s = _Rows(ms, ms, None)

    def mods(l):
        parts_p = [mod[l, :bp, i * d:(i + 1) * d].reshape(bp, 1, d) for i in range(6)]
        parts_s = [jnp.repeat(mod[l, bp:n_c, i * d:(i + 1) * d], ts, axis=0).reshape(1, ms, d) for i in range(6)]
        return parts_p, parts_s

    xp = x_prompt.reshape(mp, d)
    xs = x_sample.reshape(ms, d)
    mod_p, mod_s = mods(0)
    hp = _prenorm(xp, g_mix, 0, mod_p[1], mod_p[0], rows_p)
    hs = _prenorm(xs, g_mix, 0, mod_s[1], mod_s[0], rows_s)

    conv_p, conv_s, ret_p, ret_s = [], [], [], []
    win_p = [[] for _ in range(ng)]
    new_kv_s = [[] for _ in range(ng)]

    for l in range(depth):
        sh1p, sc1p, g1p, sh2p, sc2p, g2p = mod_p
        sh1s, sc1s, g1s, sh2s, sc2s, g2s = mod_s

        proj_ab = _matmul(hp, w_in_b, l, F32, 1024, 512, ncols=OFF_DIL, name="matmul_in_ab")
        qkv_h = _matmul_heads(hp, w_in_b, l, 1024, OFF_DIL, OFF_GATE - OFF_DIL)
        gates = _matmul(hp, w_in_b, l, F32, 1024, 512, col0=OFF_GATE, ncols=N_IN - OFF_GATE, name="matmul_in_gates")
        proj3 = proj_ab.reshape(bp, tp, OFF_DIL)
        ya, st = _conv_prompt(proj3, l, conv_w, conv_b3, ln_g3, ln_b3)
        conv_p.append(st[:, CONV_HALO - (CONV_WIDTH - 1):])
        yb, s_new = _retention(proj3, retn_p[0], retn_p[1], None, l)
        ret_p.append(s_new)
        yc, k_rot = _attn_prompt(qkv_h, rope_p[0], rope_p[1], bp, tp)
        v_h = qkv_h.reshape(3 * ng, N_HEADS, bp, tp, HEAD_DIM)
        for g, (window, dil) in enumerate(DIL_GROUPS):
            keep = min(window, tp)
            tails = [a[:, :, tp - keep:].transpose(1, 2, 0, 3)
                     for a in (k_rot[g].reshape(N_HEADS, bp, tp, HEAD_DIM), v_h[3 * g + 2])]
            win_p[g].append(jnp.stack(tails, axis=2))
        merged = _merge(ya.reshape(mp, CONV_DIM), yb.reshape(mp, RV_W), yc, w_a_b, w_b_b, w_c_b, gates, 0, l, 1024, 512)
        xp, h2 = _outproj(merged, w_out_b, l, xp, g_mix_post, g1p, g_ffn, sc2p, sh2p, rows_p)
        a = _matmul(h2, w_up_b, l, BF16, 1024, 512, act="relu2")

        proj_s = _matmul(hs, w_in_b, l, F32, ms, 512)
        proj_s3 = proj_s.reshape(bs, ts, N_IN)
        ya_s, st_s = _conv_sample(proj_s3, l, state_conv, conv_w, conv_b3, ln_g3, ln_b3)
        conv_s.append(st_s)
        yb_s, s_new_s = _retention(proj_s3, retn_s[0], retn_s[1], state_retention, l)
        ret_s.append(s_new_s)
        yc_s, k_new = _attn_sample(proj_s3, caches, l, rope_s[0], rope_s[1])
        for g in range(ng):
            v_off = OFF_DIL + (3 * g + 2) * HW
            kv = jnp.stack([k_new[g], proj_s3[:, :, v_off:v_off + HW]], axis=2)
            new_kv_s[g].append(kv.reshape(bs, ts, 2, N_HEADS, HEAD_DIM))
        merged_s = _merge(ya_s.reshape(ms, CONV_DIM), yb_s.reshape(ms, RV_W), yc_s.reshape(ms, HW),
                          w_a_b, w_b_b, w_c_b, proj_s, OFF_GATE, l, ms, 512)
        xs, h2_s = _outproj(merged_s, w_out_b, l, xs, g_mix_post, g1s, g_ffn, sc2s, sh2s, rows_s)
        a_s = _matmul(h2_s, w_up_b, l, BF16, ms, 512, act="relu2")

        if l + 1 < depth:
            mod_p_n, mod_s_n = mods(l + 1)
            xp, hp = _down(a, w_down_b, l, xp, g_ffn_post, g2p, (g_mix, mod_p_n[1], mod_p_n[0]), rows_p)
            xs, hs = _down(a_s, w_down_b, l, xs, g_ffn_post, g2s, (g_mix, mod_s_n[1], mod_s_n[0]), rows_s)
            mod_p, mod_s = mod_p_n, mod_s_n
        else:
            xp, _ = _down(a, w_down_b, l, xp, g_ffn_post, g2p, None, rows_p)
            xs, _ = _down(a_s, w_down_b, l, xs, g_ffn_post, g2s, None, rows_s)

    win_s = []
    for g, cache in enumerate((cache_win128, cache_win512, cache_win2048)):
        new = jnp.stack(new_kv_s[g])
        win_s.append(jnp.concatenate([cache[:, :, ts:], new], axis=2))

    return (xp.reshape(bp, tp, d), xs.reshape(bs, ts, d),
            jnp.stack(conv_p), jnp.stack(conv_s), jnp.stack(ret_p), jnp.stack(ret_s),
            jnp.stack(win_p[0]), win_s[0], jnp.stack(win_p[1]), win_s[1], jnp.stack(win_p[2]), win_s[2])
```

```python
import functools
import math

import numpy as np
import jax
import jax.numpy as jnp
from jax import lax
from jax.experimental import pallas as pl
from jax.experimental.pallas import tpu as pltpu

F32 = jnp.float32
BF16 = jnp.bfloat16

D_MODEL = 2048
PAST_LEN = 16384
HEAD_DIM = 128
CONV_DIM = D_MODEL // 4
CONV_WIDTH = 31
CONV_HALO = 32
N_HEADS = D_MODEL // 512
RET_QK = 128
RET_V = 256
RET_CHUNK = 128
DIL_GROUPS = ((128, 1), (512, 4), (2048, 16))
ATTN_BLOCK = 128
D_FF = 4 * D_MODEL
ROPE_THETA = 10000.0
EPS = 1e-6
HW = N_HEADS * HEAD_DIM
RV_W = N_HEADS * RET_V
KV_ROWS = 2 * N_HEADS

OFF_CONV_A = 0
OFF_CONV_B = CONV_DIM
OFF_RQ = 2 * CONV_DIM
OFF_RK = OFF_RQ + HW
OFF_RV = OFF_RK + HW
OFF_RG = OFF_RV + RV_W
OFF_DIL = OFF_RG + RV_W
OFF_GATE = OFF_DIL + 3 * len(DIL_GROUPS) * HW
N_IN = OFF_GATE + 3 * D_MODEL

NEG = -1e30
VMEM_LIMIT = 52 * 1024 * 1024


def _params(*sem):
    return pltpu.CompilerParams(dimension_semantics=sem, vmem_limit_bytes=VMEM_LIMIT)


def _silu(x):
    return x * jax.nn.sigmoid(x)


def _rot(x, cos2, sin2):
    return x * cos2 + pltpu.roll(x, HEAD_DIM // 2, axis=1) * sin2


def _dot(a, b):
    return jnp.dot(a, b, preferred_element_type=F32)


def _dot_nt(a, b):
    return lax.dot_general(a, b, (((1,), (1,)), ((), ())), preferred_element_type=F32)


def _dot_tn(a, b):
    return lax.dot_general(a, b, (((0,), (0,)), ((), ())), preferred_element_type=F32)


def _rope_tables(pos, inv_freq, rows):
    ang = np.asarray(pos, np.float64)[:, None] * inv_freq[None, :]
    cos = np.concatenate([np.cos(ang), np.cos(ang)], axis=-1)
    sin = np.concatenate([-np.sin(ang), np.sin(ang)], axis=-1)
    pad = rows - cos.shape[0]
    if pad:
        cos = np.pad(cos, ((0, pad), (0, 0)))
        sin = np.pad(sin, ((0, pad), (0, 0)))
    return jnp.asarray(cos, F32), jnp.asarray(sin, F32)


def _rope_inv_freq():
    return 1.0 / (ROPE_THETA ** (np.arange(0, HEAD_DIM, 2, dtype=np.float64) / HEAD_DIM))


def _retnet_inv_freq():
    return 1.0 / (ROPE_THETA ** np.linspace(0.0, 1.0, RET_QK // 2, dtype=np.float64))


def _ret_decay_tables(rows, lp):
    lg = np.log(1.0 - 2.0 ** (-5.0 - np.arange(N_HEADS, dtype=np.float64)))
    i = np.arange(lp, dtype=np.float64)
    rel = i[:, None] - i[None, :]
    real = (i < rows)
    dmask = np.where((rel >= 0) & real[:, None] & real[None, :],
                     np.exp(np.maximum(rel, 0.0)[None] * lg[:, None, None]), 0.0)
    qdec = np.where(real[None, :], np.exp((i[None, :] + 1.0) * lg[:, None]), 0.0)
    kdec = np.where(real[None, :], np.exp((rows - 1.0 - i[None, :]) * lg[:, None]), 0.0)
    cdec = tuple(float(v) for v in np.exp(rows * lg))
    bl = lambda a: jnp.asarray(np.broadcast_to(a[:, :, None], (N_HEADS, lp, HEAD_DIM)), F32)
    return jnp.asarray(dmask, F32), bl(qdec), bl(kdec), cdec


class _Rows:
    def __init__(self, m, tm, rows_per_batch):
        self.m = m
        self.tm = tm
        self.per_row = rows_per_batch is None
        self.tiles_per_b = None if self.per_row else rows_per_batch // tm

    def mod_spec(self):
        if self.per_row:
            return pl.BlockSpec((1, self.tm, D_MODEL), lambda i, *_: (0, i, 0))
        tpb = self.tiles_per_b
        return pl.BlockSpec((1, 1, D_MODEL), lambda i, *_: (i // tpb, 0, 0))


def _mod_kernel(c_ref, w_ref, b_ref, o_ref):
    c = c_ref[...]
    s = _silu(c).astype(BF16)
    o_ref[0] = _dot(s, w_ref[0].astype(BF16)) + b_ref[0]


def _modulation(c_all, w_mod, b_mod):
    depth, d, n = w_mod.shape
    rows = c_all.shape[0]
    tn = 1024
    return pl.pallas_call(
        _mod_kernel,
        grid=(depth, n // tn),
        in_specs=[pl.BlockSpec((rows, d), lambda l, j: (0, 0)),
                  pl.BlockSpec((1, d, tn), lambda l, j: (l, 0, j)),
                  pl.BlockSpec((1, 1, tn), lambda l, j: (l, 0, j))],
        out_specs=pl.BlockSpec((1, rows, tn), lambda l, j: (l, 0, j)),
        out_shape=jax.ShapeDtypeStruct((depth, rows, n), F32),
        compiler_params=_params("parallel", "parallel"),
        name="modulation",
    )(c_all, w_mod, b_mod.reshape(depth, 1, n))


def _norm_mod(x, g, sc, sh):
    y = x * lax.rsqrt(jnp.mean(x * x, axis=-1, keepdims=True) + EPS) * g
    return y * (1.0 + sc) + sh


def _prenorm_kernel(x_ref, g_ref, sc_ref, sh_ref, h_ref):
    h_ref[...] = _norm_mod(x_ref[...], g_ref[0], sc_ref[0], sh_ref[0]).astype(BF16)


def _prenorm(x, g, l, sc, sh, rows):
    m, d = x.shape
    tm = rows.tm
    return pl.pallas_call(
        _prenorm_kernel,
        grid=(m // tm,),
        in_specs=[pl.BlockSpec((tm, d), lambda i: (i, 0)),
                  pl.BlockSpec((1, 1, d), lambda i: (l, 0, 0)),
                  rows.mod_spec(), rows.mod_spec()],
        out_specs=pl.BlockSpec((tm, d), lambda i: (i, 0)),
        out_shape=jax.ShapeDtypeStruct((m, d), BF16),
        compiler_params=_params("parallel"),
        name="prenorm",
    )(x, g, sc, sh)


def _mm_kernel(x_ref, w_ref, o_ref, *, act):
    y = _dot(x_ref[...], w_ref[...])
    if act == "relu2":
        y = jnp.square(jnp.maximum(y, 0.0))
    o_ref[...] = y.astype(o_ref.dtype)


def _matmul(x, w, l, out_dtype, tm, tn, act=None, col0=0, ncols=None, name="matmul"):
    m, k = x.shape
    n = w.shape[-1] if ncols is None else ncols
    c0 = col0 // tn
    assert c0 * tn == col0 and n % tn == 0
    return pl.pallas_call(
        functools.partial(_mm_kernel, act=act),
        grid=(m // tm, n // tn),
        in_specs=[pl.BlockSpec((tm, k), lambda i, j: (i, 0)),
                  pl.BlockSpec((None, k, tn), lambda i, j: (l, 0, c0 + j))],
        out_specs=pl.BlockSpec((tm, tn), lambda i, j: (i, j)),
        out_shape=jax.ShapeDtypeStruct((m, n), out_dtype),
        compiler_params=_params("parallel", "parallel"),
        name=name,
    )(x, w)


def _mm_heads_kernel(x_ref, w_ref, o_ref):
    y = _dot(x_ref[...], w_ref[...])
    for h in range(N_HEADS):
        o_ref[h] = y[:, h * HEAD_DIM:(h + 1) * HEAD_DIM]


def _matmul_heads(x, w, l, tm, col0, ncols):
    m, k = x.shape
    c0 = col0 // HW
    assert c0 * HW == col0 and ncols % HW == 0
    return pl.pallas_call(
        _mm_heads_kernel,
        grid=(m // tm, ncols // HW),
        in_specs=[pl.BlockSpec((tm, k), lambda i, j: (i, 0)),
                  pl.BlockSpec((None, k, HW), lambda i, j: (l, 0, c0 + j))],
        out_specs=pl.BlockSpec((N_HEADS, tm, HEAD_DIM), lambda i, j: (j, i, 0)),
        out_shape=jax.ShapeDtypeStruct((ncols // HEAD_DIM, m, HEAD_DIM), F32),
        compiler_params=_params("parallel", "parallel"),
        name="matmul_heads",
    )(x, w)


def _conv_ln_silu(full_ref, first, rows, w_ref, cb, lg, lb):
    acc = jnp.zeros((rows, CONV_DIM), F32)
    for j in range(CONV_WIDTH):
        acc = acc + full_ref[first + j:first + j + rows, :] * w_ref[j:j + 1, :]
    y = acc + cb
    yc = y - jnp.mean(y, axis=-1, keepdims=True)
    z = yc * lax.rsqrt(jnp.mean(yc * yc, axis=-1, keepdims=True) + EPS) * lg + lb
    return _silu(z)


def _conv_p_kernel(a_ref, b_ref, ha_ref, hb_ref, w_ref, cb_ref, lg_ref, lb_ref, ya_ref, st_ref, full_ref,
                   *, tt, rc):
    t = pl.program_id(1)
    u = a_ref[...] * jax.nn.sigmoid(b_ref[...])
    uh = ha_ref[...] * jax.nn.sigmoid(hb_ref[...])
    full_ref[0:CONV_HALO, :] = jnp.where(t > 0, uh, 0.0)
    full_ref[CONV_HALO:, :] = u
    st_ref[...] = u[tt - CONV_HALO:, :]
    skip = CONV_HALO - (CONV_WIDTH - 1)
    for r0 in range(0, tt, rc):
        y = _conv_ln_silu(full_ref, skip + r0, rc, w_ref, cb_ref[...], lg_ref[...], lb_ref[...])
        ya_ref[r0:r0 + rc, :] = y.astype(BF16)


def _conv_prompt(proj3, l, conv_w, conv_b, ln_g, ln_b):
    b, t, _ = proj3.shape
    tt, rc = 256, 32
    hpt = tt // CONV_HALO
    vec = pl.BlockSpec((None, 1, CONV_DIM), lambda i, j: (l, 0, 0))
    return pl.pallas_call(
        functools.partial(_conv_p_kernel, tt=tt, rc=rc),
        grid=(b, t // tt),
        in_specs=[pl.BlockSpec((None, tt, CONV_DIM), lambda i, j: (i, j, 0)),
                  pl.BlockSpec((None, tt, CONV_DIM), lambda i, j: (i, j, 1)),
                  pl.BlockSpec((None, CONV_HALO, CONV_DIM), lambda i, j: (i, jnp.maximum(j * hpt - 1, 0), 0)),
                  pl.BlockSpec((None, CONV_HALO, CONV_DIM), lambda i, j: (i, jnp.maximum(j * hpt - 1, 0), 1)),
                  pl.BlockSpec((None, CONV_WIDTH, CONV_DIM), lambda i, j: (l, 0, 0)),
                  vec, vec, vec],
        out_specs=[pl.BlockSpec((None, tt, CONV_DIM), lambda i, j: (i, j, 0)),
                   pl.BlockSpec((None, CONV_HALO, CONV_DIM), lambda i, j: (i, 0, 0))],
        out_shape=[jax.ShapeDtypeStruct((b, t, CONV_DIM), BF16),
                   jax.ShapeDtypeStruct((b, CONV_HALO, CONV_DIM), F32)],
        scratch_shapes=[pltpu.VMEM((tt + CONV_HALO, CONV_DIM), F32)],
        compiler_params=_params("parallel", "arbitrary"),
        name="conv_prompt",
    )(proj3, proj3, proj3, proj3, conv_w, conv_b, ln_g, ln_b)


def _conv_s_kernel(a_ref, b_ref, st_in_ref, w_ref, cb_ref, lg_ref, lb_ref, ya_ref, st_ref, full_ref, *, t):
    keep = CONV_WIDTH - 1
    u = a_ref[...] * jax.nn.sigmoid(b_ref[...])
    full_ref[0:keep, :] = st_in_ref[...]
    full_ref[keep:keep + t, :] = u
    st_ref[...] = full_ref[t:t + keep, :]
    y = _conv_ln_silu(full_ref, 0, t, w_ref, cb_ref[...], lg_ref[...], lb_ref[...])
    ya_ref[...] = y.astype(BF16)


def _conv_sample(proj3, l, state_conv, conv_w, conv_b, ln_g, ln_b):
    b, t, _ = proj3.shape
    keep = CONV_WIDTH - 1
    vec = pl.BlockSpec((None, 1, CONV_DIM), lambda i: (l, 0, 0))
    return pl.pallas_call(
        functools.partial(_conv_s_kernel, t=t),
        grid=(b,),
        in_specs=[pl.BlockSpec((None, t, CONV_DIM), lambda i: (i, 0, 0)),
                  pl.BlockSpec((None, t, CONV_DIM), lambda i: (i, 0, 1)),
                  pl.BlockSpec((None, None, keep, CONV_DIM), lambda i: (l, i, 0, 0)),
                  pl.BlockSpec((None, CONV_WIDTH, CONV_DIM), lambda i: (l, 0, 0)),
                  vec, vec, vec],
        out_specs=[pl.BlockSpec((None, t, CONV_DIM), lambda i: (i, 0, 0)),
                   pl.BlockSpec((None, keep, CONV_DIM), lambda i: (i, 0, 0))],
        out_shape=[jax.ShapeDtypeStruct((b, t, CONV_DIM), BF16),
                   jax.ShapeDtypeStruct((b, keep, CONV_DIM), F32)],
        scratch_shapes=[pltpu.VMEM((keep + 8 + t, CONV_DIM), F32)],
        compiler_params=_params("parallel"),
        name="conv_sample",
    )(proj3, proj3, state_conv, conv_w, conv_b, ln_g, ln_b)


def _ret_kernel(*refs, rows, lp, has_s0, cdec, n_chunks):
    if has_s0:
        rq, rk, rv, rg, cos, sin, dm, qd, kd, s0, yb, sout, s_ref, qp, kp, vp, gp = refs
    else:
        rq, rk, rv, rg, cos, sin, dm, qd, kd, yb, sout, s_ref = refs
    c = pl.program_id(1)

    @pl.when(c == 0)
    def _():
        if has_s0:
            s_ref[...] = s0[...]
        else:
            s_ref[...] = jnp.zeros_like(s_ref)

    if rows < lp:
        for pad, src in ((qp, rq), (kp, rk), (vp, rv), (gp, rg)):
            pad[...] = jnp.zeros_like(pad)
            pad[0:rows, :] = src[...]
        rq, rk, rv, rg = qp, kp, vp, gp

    cos2 = cos[...]
    sin2 = sin[...]
    scale = RET_QK ** -0.5
    for h in range(N_HEADS):
        qs = slice(h * RET_QK, (h + 1) * RET_QK)
        vs = slice(h * RET_V, (h + 1) * RET_V)
        q = _rot(rq[:, qs], cos2, sin2) * scale
        k = _rot(rk[:, qs], cos2, sin2)
        v = rv[:, vs].astype(BF16)
        sc = _dot_nt(q.astype(BF16), k.astype(BF16)) * dm[h]
        s_h = s_ref[h]
        o = _dot(sc.astype(BF16), v) + _dot((q * qd[h]).astype(BF16), s_h.astype(BF16))
        s_ref[h] = s_h * cdec[h] + _dot_tn((k * kd[h]).astype(BF16), v)
        o = o * lax.rsqrt(jnp.mean(o * o, axis=-1, keepdims=True) + EPS)
        y = o * _silu(rg[:, vs])
        yb[:, vs] = y[0:rows, :].astype(BF16)

    @pl.when(c == n_chunks - 1)
    def _():
        sout[...] = s_ref[...]


def _retention(proj3, cos, sin, s0, l):
    b, t, _ = proj3.shape
    rows = math.gcd(t, RET_CHUNK)
    lp = max(rows, 16)
    n_chunks = t // rows
    dmask, qdec, kdec, cdec = _ret_decay_tables(rows, lp)
    has_s0 = s0 is not None
    assert has_s0 == (rows < lp)
    in_specs = [pl.BlockSpec((None, rows, HW), lambda i, c: (i, c, OFF_RQ // HW)),
                pl.BlockSpec((None, rows, HW), lambda i, c: (i, c, OFF_RK // HW)),
                pl.BlockSpec((None, rows, RV_W), lambda i, c: (i, c, OFF_RV // RV_W)),
                pl.BlockSpec((None, rows, RV_W), lambda i, c: (i, c, OFF_RG // RV_W)),
                pl.BlockSpec((lp, HEAD_DIM), lambda i, c: (c, 0)),
                pl.BlockSpec((lp, HEAD_DIM), lambda i, c: (c, 0)),
                pl.BlockSpec((N_HEADS, lp, lp), lambda i, c: (0, 0, 0)),
                pl.BlockSpec((N_HEADS, lp, HEAD_DIM), lambda i, c: (0, 0, 0)),
                pl.BlockSpec((N_HEADS, lp, HEAD_DIM), lambda i, c: (0, 0, 0))]
    args = [proj3, proj3, proj3, proj3, cos, sin, dmask, qdec, kdec]
    scratch = [pltpu.VMEM((N_HEADS, RET_QK, RET_V), F32)]
    if has_s0:
        in_specs.append(pl.BlockSpec((None, None, N_HEADS, RET_QK, RET_V), lambda i, c: (l, i, 0, 0, 0)))
        args.append(s0)
        scratch += [pltpu.VMEM((lp, HW), F32), pltpu.VMEM((lp, HW), F32),
                    pltpu.VMEM((lp, RV_W), F32), pltpu.VMEM((lp, RV_W), F32)]
    return pl.pallas_call(
        functools.partial(_ret_kernel, rows=rows, lp=lp, has_s0=has_s0, cdec=cdec, n_chunks=n_chunks),
        grid=(b, n_chunks),
        in_specs=in_specs,
        out_specs=[pl.BlockSpec((None, rows, RV_W), lambda i, c: (i, c, 0)),
                   pl.BlockSpec((None, N_HEADS, RET_QK, RET_V), lambda i, c: (i, 0, 0, 0))],
        out_shape=[jax.ShapeDtypeStruct((b, t, RV_W), BF16),
                   jax.ShapeDtypeStruct((b, N_HEADS, RET_QK, RET_V), F32)],
        scratch_shapes=scratch,
        compiler_params=_params("parallel", "arbitrary"),
        name="retention",
    )(*args)


ATTN_ROWS = 2048
ROPE_CHUNK = 256


def _rows_at(start, size, stride):
    return pl.ds(start, size, stride=stride) if stride > 1 else pl.ds(start, size)


def _softmax_pv(q, keys, vals, valid):
    s = jnp.where(valid, _dot_nt(q, keys), NEG)
    m = jnp.max(s, axis=-1, keepdims=True)
    p = jnp.exp(s - m)
    den = jnp.sum(p, axis=-1, keepdims=True)
    return _dot((p / den).astype(BF16), vals), m + jnp.log(den)


def _attn_p_kernel(*refs):
    ng = len(DIL_GROUPS)
    qkv = refs[0:3 * ng]
    cos_ref, sin_ref = refs[3 * ng:3 * ng + 2]
    yc_ref = refs[3 * ng + 2]
    krs = refs[3 * ng + 3:4 * ng + 3]
    scr = refs[4 * ng + 3:]
    qrot, o_sc, lse_sc, kps, vps = [scr[n * ng:(n + 1) * ng] for n in range(5)]
    qs, ks, vs = qkv[0::3], qkv[1::3], qkv[2::3]
    i = pl.program_id(2)
    rows, blk = ATTN_ROWS, ATTN_BLOCK
    scale = HEAD_DIM ** -0.5

    @pl.when(i == 0)
    def _():
        for g in range(ng):
            kps[g][...] = jnp.zeros_like(kps[g])
            vps[g][...] = jnp.zeros_like(vps[g])

    for c in range(0, rows, ROPE_CHUNK):
        rs = slice(c, c + ROPE_CHUNK)
        cos2, sin2 = cos_ref[rs, :], sin_ref[rs, :]
        for g in range(ng):
            qrot[g][rs, :] = _rot(qs[g][rs, :], cos2, sin2) * scale
            krs[g][rs, :] = _rot(ks[g][rs, :], cos2, sin2)

    qi = lax.broadcasted_iota(jnp.int32, (blk, 2 * blk), 0)
    ki = lax.broadcasted_iota(jnp.int32, (blk, 2 * blk), 1)
    back = blk + qi - ki
    band = (back >= 0) & (back <= blk)
    band_first = band & (ki >= jnp.where(i > 0, 0, blk))

    for g, (window, dil) in enumerate(DIL_GROUPS):
        assert window // dil == blk
        n_sub = rows // (dil * blk)

        for r in range(dil):
            for jb in range(n_sub):
                at = lambda j, n: _rows_at(r + j * blk * dil, n, dil)
                q = qrot[g][at(jb, blk), :].astype(BF16)
                if jb == 0:
                    keys = jnp.concatenate([kps[g][at(0, blk), :], krs[g][at(0, blk), :]], axis=0)
                    vals = jnp.concatenate([vps[g][at(0, blk), :], vs[g][at(0, blk), :]], axis=0)
                    valid = band_first
                else:
                    keys = krs[g][at(jb - 1, 2 * blk), :]
                    vals = vs[g][at(jb - 1, 2 * blk), :]
                    valid = band
                o, lse = _softmax_pv(q, keys.astype(BF16), vals.astype(BF16), valid)
                o_sc[g][at(jb, blk), :] = o
                lse_sc[g][at(jb, blk), :] = jnp.broadcast_to(lse, (blk, HEAD_DIM))

    for g, (window, dil) in enumerate(DIL_GROUPS):
        keep = dil * blk
        kps[g][...] = krs[g][rows - keep:, :]
        vps[g][...] = vs[g][rows - keep:, :]

    for c in range(0, rows, ROPE_CHUNK):
        rs = slice(c, c + ROPE_CHUNK)
        ls = [lse_sc[g][rs, :] for g in range(ng)]
        m = functools.reduce(jnp.maximum, ls)
        es = [jnp.exp(a - m) for a in ls]
        den = functools.reduce(lambda a, e: a + e, es)
        y = functools.reduce(lambda a, e: a + e, [(es[g] / den) * o_sc[g][rs, :] for g in range(ng)])
        yc_ref[rs, :] = y.astype(BF16)


def _attn_prompt(qkv_h, cos, sin, b, t):
    ng = len(DIL_GROUPS)
    m = b * t
    rows = ATTN_ROWS
    nblk = t // rows
    assert nblk * rows == t

    def head(n):
        return pl.BlockSpec((None, rows, HEAD_DIM), lambda i, h, j: (n * N_HEADS + h, i * nblk + j, 0))

    tab = pl.BlockSpec((rows, HEAD_DIM), lambda i, h, j: (j, 0))
    keeps = [dil * ATTN_BLOCK for _, dil in DIL_GROUPS]
    scratch = ([pltpu.VMEM((rows, HEAD_DIM), F32)] * (3 * ng)
               + [pltpu.VMEM((k, HEAD_DIM), F32) for k in keeps] * 2)
    res = pl.pallas_call(
        _attn_p_kernel,
        grid=(b, N_HEADS, nblk),
        in_specs=[head(n) for n in range(3 * ng)] + [tab, tab],
        out_specs=[pl.BlockSpec((rows, HEAD_DIM), lambda i, h, j: (i * nblk + j, h))]
                  + [pl.BlockSpec((None, rows, HEAD_DIM), lambda i, h, j: (h, i * nblk + j, 0))] * ng,
        out_shape=[jax.ShapeDtypeStruct((m, HW), BF16)]
                  + [jax.ShapeDtypeStruct((N_HEADS, m, HEAD_DIM), F32)] * ng,
        scratch_shapes=scratch,
        compiler_params=_params("parallel", "parallel", "arbitrary"),
        name="attn_prompt",
    )(*([qkv_h] * (3 * ng)), cos, sin)
    return res[0], res[1:]


def _attn_s_kernel(*refs, t, lp, cache_rows):
    ng = len(DIL_GROUPS)
    qkv = refs[0:3 * ng]
    caches = refs[3 * ng:4 * ng]
    cos_ref, sin_ref = refs[4 * ng:4 * ng + 2]
    yc_ref = refs[4 * ng + 2]
    knew = refs[4 * ng + 3:5 * ng + 3]
    qp, kp, vp = refs[5 * ng + 3:5 * ng + 6]
    cos2, sin2 = cos_ref[...], sin_ref[...]
    scale = HEAD_DIM ** -0.5
    qi = lax.broadcasted_iota(jnp.int32, (lp, lp), 0)
    kn_i = lax.broadcasted_iota(jnp.int32, (lp, lp), 1)
    outs = [[None] * ng for _ in range(N_HEADS)]
    lses = [[None] * ng for _ in range(N_HEADS)]
    for g, (window, dil) in enumerate(DIL_GROUPS):
        lb = cache_rows[g]
        for pad, src in ((qp, qkv[3 * g]), (kp, qkv[3 * g + 1]), (vp, qkv[3 * g + 2])):
            pad[...] = jnp.zeros_like(pad)
            pad[0:t, :] = src[...]
        cache = caches[g]
        qc = lax.broadcasted_iota(jnp.int32, (lp, lb), 0)
        kc_i = lax.broadcasted_iota(jnp.int32, (lp, lb), 1)
        back_c = lb + qc - kc_i
        valid_c = ((back_c & (dil - 1)) == 0) & (back_c <= window)
        back_n = qi - kn_i
        valid_n = (back_n >= 0) & ((back_n & (dil - 1)) == 0) & (back_n <= window) & (kn_i < t)
        for h in range(N_HEADS):
            hs = slice(h * HEAD_DIM, (h + 1) * HEAD_DIM)
            q = (_rot(qp[:, hs], cos2, sin2) * scale).astype(BF16)
            k_new = _rot(kp[:, hs], cos2, sin2)
            v_new = vp[:, hs]
            for tt in range(t):
                knew[g][tt * KV_ROWS + h:tt * KV_ROWS + h + 1, :] = k_new[tt:tt + 1, :]
                knew[g][tt * KV_ROWS + N_HEADS + h:tt * KV_ROWS + N_HEADS + h + 1, :] = v_new[tt:tt + 1, :]
            k_old = cache[pl.ds(h, lb, stride=KV_ROWS), :].astype(BF16)
            v_old = cache[pl.ds(N_HEADS + h, lb, stride=KV_ROWS), :].astype(BF16)
            s_c = jnp.where(valid_c, _dot_nt(q, k_old), NEG)
            s_n = jnp.where(valid_n, _dot_nt(q, k_new.astype(BF16)), NEG)
            m = jnp.maximum(jnp.max(s_c, axis=-1, keepdims=True), jnp.max(s_n, axis=-1, keepdims=True))
            p_c = jnp.exp(s_c - m)
            p_n = jnp.exp(s_n - m)
            den = jnp.sum(p_c, axis=-1, keepdims=True) + jnp.sum(p_n, axis=-1, keepdims=True)
            o = (_dot((p_c / den).astype(BF16), v_old)
                 + _dot((p_n / den).astype(BF16), v_new.astype(BF16)))
            outs[h][g] = o
            lses[h][g] = m + jnp.log(den)
    for h in range(N_HEADS):
        hs = slice(h * HEAD_DIM, (h + 1) * HEAD_DIM)
        m = functools.reduce(jnp.maximum, lses[h])
        es = [jnp.exp(a - m) for a in lses[h]]
        den = functools.reduce(lambda a, c: a + c, es)
        y = functools.reduce(lambda a, c: a + c, [(e / den) * o for e, o in zip(es, outs[h])])
        yc_ref[:, hs] = y[0:t, :].astype(BF16)


def _attn_sample(proj3, caches, l, cos, sin):
    b, t, _ = proj3.shape
    lp = cos.shape[0]
    ng = len(DIL_GROUPS)
    cache_rows = tuple(c.shape[2] // KV_ROWS for c in caches)
    tok = lambda c0: pl.BlockSpec((None, t, HW), lambda i: (i, 0, c0))
    in_specs = [tok((OFF_DIL + n * HW) // HW) for n in range(3 * ng)]
    in_specs += [pl.BlockSpec((None, None, lb * KV_ROWS, HEAD_DIM), lambda i: (l, i, 0, 0)) for lb in cache_rows]
    tab = pl.BlockSpec((lp, HEAD_DIM), lambda i: (0, 0))
    in_specs += [tab, tab]
    res = pl.pallas_call(
        functools.partial(_attn_s_kernel, t=t, lp=lp, cache_rows=cache_rows),
        grid=(b,),
        in_specs=in_specs,
        out_specs=[pl.BlockSpec((None, t, HW), lambda i: (i, 0, 0))]
                  + [pl.BlockSpec((None, t * KV_ROWS, HEAD_DIM), lambda i: (i, 0, 0))] * ng,
        out_shape=[jax.ShapeDtypeStruct((b, t, HW), BF16)]
                  + [jax.ShapeDtypeStruct((b, t * KV_ROWS, HEAD_DIM), F32)] * ng,
        scratch_shapes=[pltpu.VMEM((lp, HW), F32)] * 3,
        compiler_params=_params("parallel"),
        name="attn_sample",
    )(*([proj3] * (3 * ng)), *caches, cos, sin)
    return res[0], res[1:]


def _cache_shift_kernel(cur_ref, nxt_ref, new_ref, o_ref, *, ch, shift, n_chunks):
    c = pl.program_id(2)
    o_ref[0:ch - shift, :] = cur_ref[shift:ch, :]
    o_ref[ch - shift:ch, :] = jnp.where(c == n_chunks - 1, new_ref[...], nxt_ref[...])


def _cache_shift(cache, new):
    depth, b, rows, d = cache.shape
    shift = new.shape[2]
    ch = min(rows, 2048)
    n_chunks = rows // ch
    assert n_chunks * ch == rows and ch % shift == 0 and shift % 8 == 0
    per = ch // shift
    last = rows // shift - 1
    return pl.pallas_call(
        functools.partial(_cache_shift_kernel, ch=ch, shift=shift, n_chunks=n_chunks),
        grid=(depth, b, n_chunks),
        in_specs=[pl.BlockSpec((None, None, ch, d), lambda l, i, c: (l, i, c, 0)),
                  pl.BlockSpec((None, None, shift, d), lambda l, i, c: (l, i, jnp.minimum((c + 1) * per, last), 0)),
                  pl.BlockSpec((None, None, shift, d), lambda l, i, c: (l, i, 0, 0))],
        out_specs=pl.BlockSpec((None, None, ch, d), lambda l, i, c: (l, i, c, 0)),
        out_shape=jax.ShapeDtypeStruct(cache.shape, cache.dtype),
        compiler_params=_params("parallel", "parallel", "arbitrary"),
        name="cache_shift",
    )(cache, cache, new)


def _win_pack_kernel(k_ref, v_ref, o_ref, *, rows):
    for h in range(N_HEADS):
        o_ref[pl.ds(h, rows, stride=KV_ROWS), :] = k_ref[h]
        o_ref[pl.ds(N_HEADS + h, rows, stride=KV_ROWS), :] = v_ref[h]


def _win_pack(k_rot, qkv_h, g, b, t, keep):
    ch = min(keep, 512)
    first = (t - keep) // ch
    per_b = t // ch
    v_blk = 3 * g + 2
    return pl.pallas_call(
        functools.partial(_win_pack_kernel, rows=ch),
        grid=(b, keep // ch),
        in_specs=[pl.BlockSpec((N_HEADS, ch, HEAD_DIM), lambda i, c: (0, i * per_b + first + c, 0)),
                  pl.BlockSpec((N_HEADS, ch, HEAD_DIM), lambda i, c: (v_blk, i * per_b + first + c, 0))],
        out_specs=pl.BlockSpec((None, ch * KV_ROWS, HEAD_DIM), lambda i, c: (i, c, 0)),
        out_shape=jax.ShapeDtypeStruct((b, keep * KV_ROWS, HEAD_DIM), F32),
        compiler_params=_params("parallel", "parallel"),
        name="win_pack",
    )(k_rot, qkv_h)


def _merge_kernel(ya, yb, yc, wa, wb, wc, ga, gb, gc, o_ref):
    acc = jax.nn.sigmoid(ga[...]) * _dot(ya[...], wa[...])
    acc = acc + jax.nn.sigmoid(gb[...]) * _dot(yb[...], wb[...])
    acc = acc + jax.nn.sigmoid(gc[...]) * _dot(yc[...], wc[...])
    o_ref[...] = acc.astype(BF16)


def _merge(ya, yb, yc, wa, wb, wc, gates, gate_off, l, tm, tn):
    m = ya.shape[0]
    act = lambda w: pl.BlockSpec((tm, w), lambda i, j: (i, 0))
    wgt = lambda k: pl.BlockSpec((None, k, tn), lambda i, j: (l, 0, j))
    gate = lambda n: pl.BlockSpec((tm, tn), lambda i, j: (i, (gate_off + n * D_MODEL) // tn + j))
    return pl.pallas_call(
        _merge_kernel,
        grid=(m // tm, D_MODEL // tn),
        in_specs=[act(CONV_DIM), act(RV_W), act(HW), wgt(CONV_DIM), wgt(RV_W), wgt(HW),
                  gate(0), gate(1), gate(2)],
        out_specs=pl.BlockSpec((tm, tn), lambda i, j: (i, j)),
        out_shape=jax.ShapeDtypeStruct((m, D_MODEL), BF16),
        compiler_params=_params("parallel", "parallel"),
        name="merge",
    )(ya, yb, yc, wa, wb, wc, gates, gates, gates)


def _post_norm_residual(x, y, gpost, gate):
    r = y * lax.rsqrt(jnp.mean(y * y, axis=-1, keepdims=True) + EPS) * gpost
    return x + gate * r


def _outproj_kernel(m_ref, w_ref, x_ref, gpost_ref, g1_ref, gffn_ref, sc2_ref, sh2_ref, xo_ref, h2_ref):
    y = _dot(m_ref[...], w_ref[...])
    xn = _post_norm_residual(x_ref[...], y, gpost_ref[0], g1_ref[0])
    xo_ref[...] = xn
    h2_ref[...] = _norm_mod(xn, gffn_ref[0], sc2_ref[0], sh2_ref[0]).astype(BF16)


def _outproj(merged, w_out, l, x, gpost, g1, gffn, sc2, sh2, rows):
    m, d = x.shape
    tm = min(rows.tm, 256)
    sub = _Rows(m, tm, None if rows.per_row else rows.tiles_per_b * rows.tm)
    row = pl.BlockSpec((tm, d), lambda i: (i, 0))
    vec = pl.BlockSpec((1, 1, d), lambda i: (l, 0, 0))
    return pl.pallas_call(
        _outproj_kernel,
        grid=(m // tm,),
        in_specs=[row, pl.BlockSpec((None, d, d), lambda i: (l, 0, 0)), row, vec, sub.mod_spec(),
                  vec, sub.mod_spec(), sub.mod_spec()],
        out_specs=[row, row],
        out_shape=[jax.ShapeDtypeStruct((m, d), F32), jax.ShapeDtypeStruct((m, d), BF16)],
        compiler_params=_params("parallel"),
        name="outproj",
    )(merged, w_out, x, gpost, g1, gffn, sc2, sh2)


def _down_kernel(*refs, nk, with_next):
    if with_next:
        a_ref, w_ref, x_ref, gpost_ref, g2_ref, gn_ref, scn_ref, shn_ref, xo_ref, hn_ref, acc_ref = refs
    else:
        a_ref, w_ref, x_ref, gpost_ref, g2_ref, xo_ref, acc_ref = refs
    k = pl.program_id(1)

    @pl.when(k == 0)
    def _():
        acc_ref[...] = jnp.zeros_like(acc_ref)

    acc_ref[...] += _dot(a_ref[...], w_ref[...])

    @pl.when(k == nk - 1)
    def _():
        xn = _post_norm_residual(x_ref[...], acc_ref[...], gpost_ref[0], g2_ref[0])
        xo_ref[...] = xn
        if with_next:
            hn_ref[...] = _norm_mod(xn, gn_ref[0], scn_ref[0], shn_ref[0]).astype(BF16)


def _down(a, w_down, l, x, gpost, g2, nxt, rows):
    m, d = x.shape
    kdim = a.shape[1]
    tm = min(rows.tm, 1024)
    tk = 512
    nk = kdim // tk
    sub = _Rows(m, tm, None if rows.per_row else rows.tiles_per_b * rows.tm)
    row = pl.BlockSpec((tm, d), lambda i, k: (i, 0))
    row_out = pl.BlockSpec((tm, d), lambda i, k: (i, 0), pipeline_mode=pl.Buffered(1))
    vec = pl.BlockSpec((1, 1, d), lambda i, k: (l, 0, 0))
    in_specs = [pl.BlockSpec((tm, tk), lambda i, k: (i, k)),
                pl.BlockSpec((None, tk, d), lambda i, k: (l, k, 0)),
                row, vec, sub.mod_spec()]
    args = [a, w_down, x, gpost, g2]
    out_specs = [row_out]
    out_shape = [jax.ShapeDtypeStruct((m, d), F32)]
    if nxt is not None:
        gn, scn, shn = nxt
        in_specs += [pl.BlockSpec((1, 1, d), lambda i, k: (l + 1, 0, 0)), sub.mod_spec(), sub.mod_spec()]
        args += [gn, scn, shn]
        out_specs.append(row_out)
        out_shape.append(jax.ShapeDtypeStruct((m, d), BF16))
    res = pl.pallas_call(
        functools.partial(_down_kernel, nk=nk, with_next=nxt is not None),
        grid=(m // tm, nk),
        in_specs=in_specs,
        out_specs=out_specs,
        out_shape=out_shape,
        scratch_shapes=[pltpu.VMEM((tm, d), F32)],
        compiler_params=_params("parallel", "arbitrary"),
        name="mlp_down",
    )(*args)
    return (res[0], res[1]) if nxt is not None else (res[0], None)


def kernel(x_prompt, x_sample, c_prompt, c_sample, state_conv, state_retention, cache_win128, cache_win512, cache_win2048, norm_mix_g, norm_mix_post_g, norm_ffn_g, norm_ffn_post_g, w_mod, b_mod, w_in, conv_w, conv_b, conv_ln_g, conv_ln_b, w_branch_a, w_branch_b, w_branch_c, w_out, w_up, w_down):
    depth = w_in.shape[0]
    bp, tp, d = x_prompt.shape
    bs, ts, _ = x_sample.shape
    mp, ms = bp * tp, bs * ts
    ng = len(DIL_GROUPS)

    w_in_b, w_a_b, w_b_b, w_c_b, w_out_b, w_up_b, w_down_b = [
        w.astype(BF16) for w in (w_in, w_branch_a, w_branch_b, w_branch_c, w_out, w_up, w_down)]
    vec3 = lambda a: a.reshape(depth, 1, a.shape[-1])
    g_mix, g_mix_post, g_ffn, g_ffn_post = map(vec3, (norm_mix_g, norm_mix_post_g, norm_ffn_g, norm_ffn_post_g))
    conv_b3, ln_g3, ln_b3 = map(vec3, (conv_b, conv_ln_g, conv_ln_b))
    caches = [c.reshape(c.shape[0], c.shape[1], c.shape[2] * KV_ROWS, HEAD_DIM)
              for c in (cache_win128, cache_win512, cache_win2048)]

    lp_s = 16
    rope_p = _rope_tables(np.arange(tp), _rope_inv_freq(), tp)
    rope_s = _rope_tables(PAST_LEN + np.arange(ts), _rope_inv_freq(), lp_s)
    retn_p = _rope_tables(np.arange(tp), _retnet_inv_freq(), tp)
    retn_s = _rope_tables(PAST_LEN + np.arange(ts), _retnet_inv_freq(), lp_s)

    n_c = bp + bs
    c_rows = -(-n_c // 8) * 8
    c_all = jnp.concatenate([c_prompt, c_sample, jnp.zeros((c_rows - n_c, d), F32)], axis=0)
    mod = _modulation(c_all, w_mod, b_mod)

    rows_p = _Rows(mp, 1024, tp)
    rows_s = _Rows(ms, ms, None)

    def mods(l):
        parts_p = [mod[l, :bp, i * d:(i + 1) * d].reshape(bp, 1, d) for i in range(6)]
        parts_s = [jnp.repeat(mod[l, bp:n_c, i * d:(i + 1) * d], ts, axis=0).reshape(1, ms, d) for i in range(6)]
        return parts_p, parts_s

    xp = x_prompt.reshape(mp, d)
    xs = x_sample.reshape(ms, d)
    mod_p, mod_s = mods(0)
    hp = _prenorm(xp, g_mix, 0, mod_p[1], mod_p[0], rows_p)
    hs = _prenorm(xs, g_mix, 0, mod_s[1], mod_s[0], rows_s)

    conv_p, conv_s, ret_p, ret_s = [], [], [], []
    win_p = [[] for _ in range(ng)]
    new_kv_s = [[] for _ in range(ng)]

    for l in range(depth):
        sh1p, sc1p, g1p, sh2p, sc2p, g2p = mod_p
        sh1s, sc1s, g1s, sh2s, sc2s, g2s = mod_s

        proj_ab = _matmul(hp, w_in_b, l, F32, 2048, 512, ncols=OFF_DIL, name="matmul_in_ab")
        qkv_h = _matmul_heads(hp, w_in_b, l, 2048, OFF_DIL, OFF_GATE - OFF_DIL)
        gates = _matmul(hp, w_in_b, l, F32, 2048, 512, col0=OFF_GATE, ncols=N_IN - OFF_GATE, name="matmul_in_gates")
        proj3 = proj_ab.reshape(bp, tp, OFF_DIL)
        ya, st = _conv_prompt(proj3, l, conv_w, conv_b3, ln_g3, ln_b3)
        conv_p.append(st[:, CONV_HALO - (CONV_WIDTH - 1):])
        yb, s_new = _retention(proj3, retn_p[0], retn_p[1], None, l)
        ret_p.append(s_new)
        yc, k_rot = _attn_prompt(qkv_h, rope_p[0], rope_p[1], bp, tp)
        for g, (window, dil) in enumerate(DIL_GROUPS):
            keep = min(window, tp)
            win_p[g].append(_win_pack(k_rot[g], qkv_h, g, bp, tp, keep))
        merged = _merge(ya.reshape(mp, CONV_DIM), yb.reshape(mp, RV_W), yc, w_a_b, w_b_b, w_c_b, gates, 0, l, 1024, 512)
        xp, h2 = _outproj(merged, w_out_b, l, xp, g_mix_post, g1p, g_ffn, sc2p, sh2p, rows_p)
        a = _matmul(h2, w_up_b, l, BF16, 2048, 512, act="relu2", name="mlp_up")

        proj_s = _matmul(hs, w_in_b, l, F32, ms, 512)
        proj_s3 = proj_s.reshape(bs, ts, N_IN)
        ya_s, st_s = _conv_sample(proj_s3, l, state_conv, conv_w, conv_b3, ln_g3, ln_b3)
        conv_s.append(st_s)
        yb_s, s_new_s = _retention(proj_s3, retn_s[0], retn_s[1], state_retention, l)
        ret_s.append(s_new_s)
        yc_s, kv_new = _attn_sample(proj_s3, caches, l, rope_s[0], rope_s[1])
        for g in range(ng):
            new_kv_s[g].append(kv_new[g])
        merged_s = _merge(ya_s.reshape(ms, CONV_DIM), yb_s.reshape(ms, RV_W), yc_s.reshape(ms, HW),
                          w_a_b, w_b_b, w_c_b, proj_s, OFF_GATE, l, ms, 512)
        xs, h2_s = _outproj(merged_s, w_out_b, l, xs, g_mix_post, g1s, g_ffn, sc2s, sh2s, rows_s)
        a_s = _matmul(h2_s, w_up_b, l, BF16, ms, 512, act="relu2")

        if l + 1 < depth:
            mod_p_n, mod_s_n = mods(l + 1)
            xp, hp = _down(a, w_down_b, l, xp, g_ffn_post, g2p, (g_mix, mod_p_n[1], mod_p_n[0]), rows_p)
            xs, hs = _down(a_s, w_down_b, l, xs, g_ffn_post, g2s, (g_mix, mod_s_n[1], mod_s_n[0]), rows_s)
            mod_p, mod_s = mod_p_n, mod_s_n
        else:
            xp, _ = _down(a, w_down_b, l, xp, g_ffn_post, g2p, None, rows_p)
            xs, _ = _down(a_s, w_down_b, l, xs, g_ffn_post, g2s, None, rows_s)

    win_s = [_cache_shift(caches[g], jnp.stack(new_kv_s[g])) for g in range(ng)]
    kv5 = lambda a: a.reshape(a.shape[0], a.shape[1], a.shape[2] // KV_ROWS, 2, N_HEADS, HEAD_DIM)
    win_p = [kv5(jnp.stack(w)) for w in win_p]
    win_s = [kv5(w) for w in win_s]

    return (xp.reshape(bp, tp, d), xs.reshape(bs, ts, d),
            jnp.stack(conv_p), jnp.stack(conv_s), jnp.stack(ret_p), jnp.stack(ret_s),
            win_p[0], win_s[0], win_p[1], win_s[1], win_p[2], win_s[2])
```

```python
import functools
import math

import numpy as np
import jax
import jax.numpy as jnp
from jax import lax
from jax.experimental import pallas as pl
from jax.experimental.pallas import tpu as pltpu

F32 = jnp.float32
BF16 = jnp.bfloat16

D_MODEL = 2048
PAST_LEN = 16384
HEAD_DIM = 128
CONV_DIM = D_MODEL // 4
CONV_WIDTH = 31
CONV_HALO = 32
N_HEADS = D_MODEL // 512
RET_QK = 128
RET_V = 256
RET_CHUNK = 128
DIL_GROUPS = ((128, 1), (512, 4), (2048, 16))
ATTN_BLOCK = 128
D_FF = 4 * D_MODEL
ROPE_THETA = 10000.0
EPS = 1e-6
HW = N_HEADS * HEAD_DIM
RV_W = N_HEADS * RET_V
KV_ROWS = 2 * N_HEADS
SUBLANES = 8

OFF_CONV_A = 0
OFF_CONV_B = CONV_DIM
OFF_RQ = 2 * CONV_DIM
OFF_RK = OFF_RQ + HW
OFF_RV = OFF_RK + HW
OFF_RG = OFF_RV + RV_W
OFF_DIL = OFF_RG + RV_W
OFF_GATE = OFF_DIL + 3 * len(DIL_GROUPS) * HW
N_IN = OFF_GATE + 3 * D_MODEL

NEG = -1e30
VMEM_LIMIT = 52 * 1024 * 1024
VMEM_LIMIT_MAX = 58 * 1024 * 1024


def _params(*sem, vmem=VMEM_LIMIT):
    return pltpu.CompilerParams(dimension_semantics=sem, vmem_limit_bytes=vmem)


def _silu(x):
    return x * jax.nn.sigmoid(x)


def _rot(x, cos2, sin2):
    return x * cos2 + pltpu.roll(x, HEAD_DIM // 2, axis=1) * sin2


def _dot(a, b):
    return jnp.dot(a, b, preferred_element_type=F32)


def _dot_nt(a, b):
    return lax.dot_general(a, b, (((1,), (1,)), ((), ())), preferred_element_type=F32)


def _dot_tn(a, b):
    return lax.dot_general(a, b, (((0,), (0,)), ((), ())), preferred_element_type=F32)


def _rope_tables(pos, inv_freq, rows):
    ang = np.asarray(pos, np.float64)[:, None] * inv_freq[None, :]
    cos = np.concatenate([np.cos(ang), np.cos(ang)], axis=-1)
    sin = np.concatenate([-np.sin(ang), np.sin(ang)], axis=-1)
    pad = rows - cos.shape[0]
    if pad:
        cos = np.pad(cos, ((0, pad), (0, 0)))
        sin = np.pad(sin, ((0, pad), (0, 0)))
    return jnp.asarray(cos, F32), jnp.asarray(sin, F32)


def _rope_inv_freq():
    return 1.0 / (ROPE_THETA ** (np.arange(0, HEAD_DIM, 2, dtype=np.float64) / HEAD_DIM))


def _retnet_inv_freq():
    return 1.0 / (ROPE_THETA ** np.linspace(0.0, 1.0, RET_QK // 2, dtype=np.float64))


def _ret_decay_tables(rows, lp):
    lg = np.log(1.0 - 2.0 ** (-5.0 - np.arange(N_HEADS, dtype=np.float64)))
    i = np.arange(lp, dtype=np.float64)
    rel = i[:, None] - i[None, :]
    real = (i < rows)
    dmask = np.where((rel >= 0) & real[:, None] & real[None, :],
                     np.exp(np.maximum(rel, 0.0)[None] * lg[:, None, None]), 0.0)
    qdec = np.where(real[None, :], np.exp((i[None, :] + 1.0) * lg[:, None]), 0.0)
    kdec = np.where(real[None, :], np.exp((rows - 1.0 - i[None, :]) * lg[:, None]), 0.0)
    cdec = tuple(float(v) for v in np.exp(rows * lg))
    bl = lambda a: jnp.asarray(np.broadcast_to(a[:, :, None], (N_HEADS, lp, HEAD_DIM)), F32)
    return jnp.asarray(dmask, F32), bl(qdec), bl(kdec), cdec


class _Rows:
    def __init__(self, m, tm, rows_per_batch):
        self.m = m
        self.tm = tm
        self.per_row = rows_per_batch is None
        self.tiles_per_b = None if self.per_row else rows_per_batch // tm

    def mod_spec(self):
        if self.per_row:
            return pl.BlockSpec((1, self.tm, D_MODEL), lambda i, *_: (0, i, 0))
        tpb = self.tiles_per_b
        return pl.BlockSpec((1, 1, D_MODEL), lambda i, *_: (i // tpb, 0, 0))


def _mod_kernel(c_ref, w_ref, b_ref, o_ref):
    c = c_ref[...]
    s = _silu(c).astype(BF16)
    o_ref[0] = _dot(s, w_ref[0].astype(BF16)) + b_ref[0]


def _modulation(c_all, w_mod, b_mod):
    depth, d, n = w_mod.shape
    rows = c_all.shape[0]
    tn = 1024
    return pl.pallas_call(
        _mod_kernel,
        grid=(depth, n // tn),
        in_specs=[pl.BlockSpec((rows, d), lambda l, j: (0, 0)),
                  pl.BlockSpec((1, d, tn), lambda l, j: (l, 0, j)),
                  pl.BlockSpec((1, 1, tn), lambda l, j: (l, 0, j))],
        out_specs=pl.BlockSpec((1, rows, tn), lambda l, j: (l, 0, j)),
        out_shape=jax.ShapeDtypeStruct((depth, rows, n), F32),
        compiler_params=_params("parallel", "parallel"),
        name="modulation",
    )(c_all, w_mod, b_mod.reshape(depth, 1, n))


def _norm_mod(x, g, sc, sh):
    y = x * lax.rsqrt(jnp.mean(x * x, axis=-1, keepdims=True) + EPS) * g
    return y * (1.0 + sc) + sh


def _prenorm_kernel(x_ref, g_ref, sc_ref, sh_ref, h_ref):
    h_ref[...] = _norm_mod(x_ref[...], g_ref[0], sc_ref[0], sh_ref[0]).astype(BF16)


def _prenorm(x, g, l, sc, sh, rows):
    m, d = x.shape
    tm = rows.tm
    return pl.pallas_call(
        _prenorm_kernel,
        grid=(m // tm,),
        in_specs=[pl.BlockSpec((tm, d), lambda i: (i, 0)),
                  pl.BlockSpec((1, 1, d), lambda i: (l, 0, 0)),
                  rows.mod_spec(), rows.mod_spec()],
        out_specs=pl.BlockSpec((tm, d), lambda i: (i, 0)),
        out_shape=jax.ShapeDtypeStruct((m, d), BF16),
        compiler_params=_params("parallel"),
        name="prenorm",
    )(x, g, sc, sh)


def _mm_kernel(x_ref, xs_ref, w_ref, o_ref, os_ref, *, act, heads):
    w = w_ref[...].astype(BF16)
    y = _dot(x_ref[...], w)
    ys = _dot(xs_ref[...], w)
    if act == "relu2":
        y = jnp.square(jnp.maximum(y, 0.0))
        ys = jnp.square(jnp.maximum(ys, 0.0))
    if heads:
        for h in range(N_HEADS):
            o_ref[h] = y[:, h * HEAD_DIM:(h + 1) * HEAD_DIM]
    else:
        o_ref[...] = y.astype(o_ref.dtype)
    os_ref[...] = ys.astype(os_ref.dtype)


def _matmul(x, xs, w, l, out_dtype, tm, tn, act=None, col0=0, ncols=None, heads=False, name="matmul"):
    m, k = x.shape
    ms = xs.shape[0]
    n = w.shape[-1] if ncols is None else ncols
    c0 = col0 // tn
    assert c0 * tn == col0 and n % tn == 0 and (not heads or tn == HW)
    if heads:
        out_spec = pl.BlockSpec((N_HEADS, tm, HEAD_DIM), lambda i, j: (j, i, 0))
        out_shape = jax.ShapeDtypeStruct((n // HEAD_DIM, m, HEAD_DIM), out_dtype)
    else:
        out_spec = pl.BlockSpec((tm, tn), lambda i, j: (i, j))
        out_shape = jax.ShapeDtypeStruct((m, n), out_dtype)
    return pl.pallas_call(
        functools.partial(_mm_kernel, act=act, heads=heads),
        grid=(m // tm, n // tn),
        in_specs=[pl.BlockSpec((tm, k), lambda i, j: (i, 0)),
                  pl.BlockSpec((ms, k), lambda i, j: (0, 0)),
                  pl.BlockSpec((None, k, tn), lambda i, j: (l, 0, c0 + j))],
        out_specs=[out_spec, pl.BlockSpec((ms, tn), lambda i, j: (0, j))],
        out_shape=[out_shape, jax.ShapeDtypeStruct((ms, n), out_dtype)],
        compiler_params=_params("arbitrary", "arbitrary"),
        name=name,
    )(x, xs, w)


def _conv_ln_silu(full_ref, first, rows, w_ref, cb, lg, lb):
    acc = jnp.zeros((rows, CONV_DIM), F32)
    for j in range(CONV_WIDTH):
        acc = acc + full_ref[first + j:first + j + rows, :] * w_ref[j:j + 1, :]
    y = acc + cb
    yc = y - jnp.mean(y, axis=-1, keepdims=True)
    z = yc * lax.rsqrt(jnp.mean(yc * yc, axis=-1, keepdims=True) + EPS) * lg + lb
    return _silu(z)


def _conv_p_kernel(a_ref, b_ref, ha_ref, hb_ref, w_ref, cb_ref, lg_ref, lb_ref, ya_ref, st_ref, sh_ref,
                   *, tt, rc):
    t = pl.program_id(1)
    u = a_ref[...] * jax.nn.sigmoid(b_ref[...])
    uh = ha_ref[...] * jax.nn.sigmoid(hb_ref[...])
    sh_ref[0, 0:CONV_HALO, :] = jnp.where(t > 0, uh, 0.0)
    sh_ref[0, CONV_HALO:, :] = u
    st_ref[...] = u[tt - CONV_HALO:, :]
    n_sh = tt + CONV_HALO - SUBLANES
    for s in range(1, SUBLANES):
        sh_ref[s, 0:n_sh, :] = sh_ref[0, s:s + n_sh, :]
    skip = CONV_HALO - (CONV_WIDTH - 1)
    cb, lg, lb = cb_ref[...], lg_ref[...], lb_ref[...]
    for r0 in range(0, tt, rc):
        acc = jnp.zeros((rc, CONV_DIM), F32)
        for j in range(CONV_WIDTH):
            first = r0 + skip + j
            s = first % SUBLANES
            wj = jnp.tile(w_ref[j], (rc // SUBLANES, 1))
            acc = acc + sh_ref[s, first - s:first - s + rc, :] * wj
        y = acc + cb
        yc = y - jnp.mean(y, axis=-1, keepdims=True)
        z = yc * lax.rsqrt(jnp.mean(yc * yc, axis=-1, keepdims=True) + EPS) * lg + lb
        ya_ref[r0:r0 + rc, :] = _silu(z).astype(BF16)


def _conv_prompt(proj3, l, conv_w8, conv_b, ln_g, ln_b):
    b, t, _ = proj3.shape
    tt, rc = 256, 32
    hpt = tt // CONV_HALO
    vec = pl.BlockSpec((None, 1, CONV_DIM), lambda i, j: (l, 0, 0))
    return pl.pallas_call(
        functools.partial(_conv_p_kernel, tt=tt, rc=rc),
        grid=(b, t // tt),
        in_specs=[pl.BlockSpec((None, tt, CONV_DIM), lambda i, j: (i, j, 0)),
                  pl.BlockSpec((None, tt, CONV_DIM), lambda i, j: (i, j, 1)),
                  pl.BlockSpec((None, CONV_HALO, CONV_DIM), lambda i, j: (i, jnp.maximum(j * hpt - 1, 0), 0)),
                  pl.BlockSpec((None, CONV_HALO, CONV_DIM), lambda i, j: (i, jnp.maximum(j * hpt - 1, 0), 1)),
                  pl.BlockSpec((None, CONV_WIDTH, SUBLANES, CONV_DIM), lambda i, j: (l, 0, 0, 0)),
                  vec, vec, vec],
        out_specs=[pl.BlockSpec((None, tt, CONV_DIM), lambda i, j: (i, j, 0)),
                   pl.BlockSpec((None, CONV_HALO, CONV_DIM), lambda i, j: (i, 0, 0))],
        out_shape=[jax.ShapeDtypeStruct((b, t, CONV_DIM), BF16),
                   jax.ShapeDtypeStruct((b, CONV_HALO, CONV_DIM), F32)],
        scratch_shapes=[pltpu.VMEM((SUBLANES, tt + CONV_HALO, CONV_DIM), F32)],
        compiler_params=_params("parallel", "arbitrary"),
        name="conv_prompt",
    )(proj3, proj3, proj3, proj3, conv_w8, conv_b, ln_g, ln_b)


def _conv_s_kernel(a_ref, b_ref, st_in_ref, w_ref, cb_ref, lg_ref, lb_ref, ya_ref, st_ref, full_ref, *, t):
    keep = CONV_WIDTH - 1
    u = a_ref[...] * jax.nn.sigmoid(b_ref[...])
    full_ref[0:keep, :] = st_in_ref[...]
    full_ref[keep:keep + t, :] = u
    st_ref[...] = full_ref[t:t + keep, :]
    y = _conv_ln_silu(full_ref, 0, t, w_ref, cb_ref[...], lg_ref[...], lb_ref[...])
    ya_ref[...] = y.astype(BF16)


def _conv_sample(proj3, l, state_conv, conv_w, conv_b, ln_g, ln_b):
    b, t, _ = proj3.shape
    keep = CONV_WIDTH - 1
    vec = pl.BlockSpec((None, 1, CONV_DIM), lambda i: (l, 0, 0))
    return pl.pallas_call(
        functools.partial(_conv_s_kernel, t=t),
        grid=(b,),
        in_specs=[pl.BlockSpec((None, t, CONV_DIM), lambda i: (i, 0, 0)),
                  pl.BlockSpec((None, t, CONV_DIM), lambda i: (i, 0, 1)),
                  pl.BlockSpec((None, None, keep, CONV_DIM), lambda i: (l, i, 0, 0)),
                  pl.BlockSpec((None, CONV_WIDTH, CONV_DIM), lambda i: (l, 0, 0)),
                  vec, vec, vec],
        out_specs=[pl.BlockSpec((None, t, CONV_DIM), lambda i: (i, 0, 0)),
                   pl.BlockSpec((None, keep, CONV_DIM), lambda i: (i, 0, 0))],
        out_shape=[jax.ShapeDtypeStruct((b, t, CONV_DIM), BF16),
                   jax.ShapeDtypeStruct((b, keep, CONV_DIM), F32)],
        scratch_shapes=[pltpu.VMEM((keep + 8 + t, CONV_DIM), F32)],
        compiler_params=_params("parallel"),
        name="conv_sample",
    )(proj3, proj3, state_conv, conv_w, conv_b, ln_g, ln_b)


def _ret_kernel(*refs, rows, lp, has_s0, cdec, n_chunks):
    if has_s0:
        rq, rk, rv, rg, cos, sin, dm, qd, kd, s0, yb, sout, s_ref, qp, kp, vp, gp = refs
    else:
        rq, rk, rv, rg, cos, sin, dm, qd, kd, yb, sout, s_ref = refs
    c = pl.program_id(1)

    @pl.when(c == 0)
    def _():
        if has_s0:
            s_ref[...] = s0[...]
        else:
            s_ref[...] = jnp.zeros_like(s_ref)

    if rows < lp:
        for pad, src in ((qp, rq), (kp, rk), (vp, rv), (gp, rg)):
            pad[...] = jnp.zeros_like(pad)
            pad[0:rows, :] = src[...]
        rq, rk, rv, rg = qp, kp, vp, gp

    cos2 = cos[...]
    sin2 = sin[...]
    scale = RET_QK ** -0.5
    for h in range(N_HEADS):
        qs = slice(h * RET_QK, (h + 1) * RET_QK)
        vs = slice(h * RET_V, (h + 1) * RET_V)
        q = _rot(rq[:, qs], cos2, sin2) * scale
        k = _rot(rk[:, qs], cos2, sin2)
        v = rv[:, vs].astype(BF16)
        sc = _dot_nt(q.astype(BF16), k.astype(BF16)) * dm[h]
        s_h = s_ref[h]
        o = _dot(sc.astype(BF16), v) + _dot((q * qd[h]).astype(BF16), s_h.astype(BF16))
        s_ref[h] = s_h * cdec[h] + _dot_tn((k * kd[h]).astype(BF16), v)
        o = o * lax.rsqrt(jnp.mean(o * o, axis=-1, keepdims=True) + EPS)
        y = o * _silu(rg[:, vs])
        yb[:, vs] = y[0:rows, :].astype(BF16)

    @pl.when(c == n_chunks - 1)
    def _():
        sout[...] = s_ref[...]


def _retention(proj3, cos, sin, s0, l):
    b, t, _ = proj3.shape
    rows = math.gcd(t, RET_CHUNK)
    lp = max(rows, 16)
    n_chunks = t // rows
    dmask, qdec, kdec, cdec = _ret_decay_tables(rows, lp)
    has_s0 = s0 is not None
    assert has_s0 == (rows < lp)
    in_specs = [pl.BlockSpec((None, rows, HW), lambda i, c: (i, c, OFF_RQ // HW)),
                pl.BlockSpec((None, rows, HW), lambda i, c: (i, c, OFF_RK // HW)),
                pl.BlockSpec((None, rows, RV_W), lambda i, c: (i, c, OFF_RV // RV_W)),
                pl.BlockSpec((None, rows, RV_W), lambda i, c: (i, c, OFF_RG // RV_W)),
                pl.BlockSpec((lp, HEAD_DIM), lambda i, c: (c, 0)),
                pl.BlockSpec((lp, HEAD_DIM), lambda i, c: (c, 0)),
                pl.BlockSpec((N_HEADS, lp, lp), lambda i, c: (0, 0, 0)),
                pl.BlockSpec((N_HEADS, lp, HEAD_DIM), lambda i, c: (0, 0, 0)),
                pl.BlockSpec((N_HEADS, lp, HEAD_DIM), lambda i, c: (0, 0, 0))]
    args = [proj3, proj3, proj3, proj3, cos, sin, dmask, qdec, kdec]
    scratch = [pltpu.VMEM((N_HEADS, RET_QK, RET_V), F32)]
    if has_s0:
        in_specs.append(pl.BlockSpec((None, None, N_HEADS, RET_QK, RET_V), lambda i, c: (l, i, 0, 0, 0)))
        args.append(s0)
        scratch += [pltpu.VMEM((lp, HW), F32), pltpu.VMEM((lp, HW), F32),
                    pltpu.VMEM((lp, RV_W), F32), pltpu.VMEM((lp, RV_W), F32)]
    return pl.pallas_call(
        functools.partial(_ret_kernel, rows=rows, lp=lp, has_s0=has_s0, cdec=cdec, n_chunks=n_chunks),
        grid=(b, n_chunks),
        in_specs=in_specs,
        out_specs=[pl.BlockSpec((None, rows, RV_W), lambda i, c: (i, c, 0)),
                   pl.BlockSpec((None, N_HEADS, RET_QK, RET_V), lambda i, c: (i, 0, 0, 0))],
        out_shape=[jax.ShapeDtypeStruct((b, t, RV_W), BF16),
                   jax.ShapeDtypeStruct((b, N_HEADS, RET_QK, RET_V), F32)],
        scratch_shapes=scratch,
        compiler_params=_params("parallel", "arbitrary"),
        name="retention",
    )(*args)


ATTN_ROWS = 2048
ROPE_CHUNK = 256


def _rows_at(start, size, stride):
    return pl.ds(start, size, stride=stride) if stride > 1 else pl.ds(start, size)


def _softmax_pv(q, keys, vals, valid):
    s = jnp.where(valid, _dot_nt(q, keys), NEG)
    m = jnp.max(s, axis=-1, keepdims=True)
    p = jnp.exp(s - m)
    den = jnp.sum(p, axis=-1, keepdims=True)
    return _dot((p / den).astype(BF16), vals), m + jnp.log(den)


def _attn_p_kernel(*refs):
    ng = len(DIL_GROUPS)
    qkv = refs[0:3 * ng]
    cos_ref, sin_ref = refs[3 * ng:3 * ng + 2]
    yc_ref = refs[3 * ng + 2]
    krs = refs[3 * ng + 3:4 * ng + 3]
    scr = refs[4 * ng + 3:]
    qrot, o_sc, lse_sc, kps, vps = [scr[n * ng:(n + 1) * ng] for n in range(5)]
    qs, ks, vs = qkv[0::3], qkv[1::3], qkv[2::3]
    i = pl.program_id(2)
    rows, blk = ATTN_ROWS, ATTN_BLOCK
    scale = HEAD_DIM ** -0.5

    @pl.when(i == 0)
    def _():
        for g in range(ng):
            kps[g][...] = jnp.zeros_like(kps[g])
            vps[g][...] = jnp.zeros_like(vps[g])

    for c in range(0, rows, ROPE_CHUNK):
        rs = slice(c, c + ROPE_CHUNK)
        cos2, sin2 = cos_ref[rs, :], sin_ref[rs, :]
        for g in range(ng):
            qrot[g][rs, :] = _rot(qs[g][rs, :], cos2, sin2) * scale
            krs[g][rs, :] = _rot(ks[g][rs, :], cos2, sin2)

    qi = lax.broadcasted_iota(jnp.int32, (blk, 2 * blk), 0)
    ki = lax.broadcasted_iota(jnp.int32, (blk, 2 * blk), 1)
    back = blk + qi - ki
    band = (back >= 0) & (back <= blk)
    band_first = band & (ki >= jnp.where(i > 0, 0, blk))

    for g, (window, dil) in enumerate(DIL_GROUPS):
        assert window // dil == blk
        n_sub = rows // (dil * blk)

        for r in range(dil):
            for jb in range(n_sub):
                at = lambda j, n: _rows_at(r + j * blk * dil, n, dil)
                q = qrot[g][at(jb, blk), :].astype(BF16)
                if jb == 0:
                    keys = jnp.concatenate([kps[g][at(0, blk), :], krs[g][at(0, blk), :]], axis=0)
                    vals = jnp.concatenate([vps[g][at(0, blk), :], vs[g][at(0, blk), :]], axis=0)
                    valid = band_first
                else:
                    keys = krs[g][at(jb - 1, 2 * blk), :]
                    vals = vs[g][at(jb - 1, 2 * blk), :]
                    valid = band
                o, lse = _softmax_pv(q, keys.astype(BF16), vals.astype(BF16), valid)
                o_sc[g][at(jb, blk), :] = o
                lse_sc[g][at(jb, blk), :] = jnp.broadcast_to(lse, (blk, HEAD_DIM))

    for g, (window, dil) in enumerate(DIL_GROUPS):
        keep = dil * blk
        kps[g][...] = krs[g][rows - keep:, :]
        vps[g][...] = vs[g][rows - keep:, :]

    for c in range(0, rows, ROPE_CHUNK):
        rs = slice(c, c + ROPE_CHUNK)
        ls = [lse_sc[g][rs, :] for g in range(ng)]
        m = functools.reduce(jnp.maximum, ls)
        es = [jnp.exp(a - m) for a in ls]
        den = functools.reduce(lambda a, e: a + e, es)
        y = functools.reduce(lambda a, e: a + e, [(es[g] / den) * o_sc[g][rs, :] for g in range(ng)])
        yc_ref[rs, :] = y.astype(BF16)


def _attn_prompt(qkv_h, cos, sin, b, t):
    ng = len(DIL_GROUPS)
    m = b * t
    rows = ATTN_ROWS
    nblk = t // rows
    assert nblk * rows == t

    def head(n):
        return pl.BlockSpec((None, rows, HEAD_DIM), lambda i, h, j: (n * N_HEADS + h, i * nblk + j, 0))

    tab = pl.BlockSpec((rows, HEAD_DIM), lambda i, h, j: (j, 0))
    keeps = [dil * ATTN_BLOCK for _, dil in DIL_GROUPS]
    scratch = ([pltpu.VMEM((rows, HEAD_DIM), F32)] * (3 * ng)
               + [pltpu.VMEM((k, HEAD_DIM), F32) for k in keeps] * 2)
    res = pl.pallas_call(
        _attn_p_kernel,
        grid=(b, N_HEADS, nblk),
        in_specs=[head(n) for n in range(3 * ng)] + [tab, tab],
        out_specs=[pl.BlockSpec((rows, HEAD_DIM), lambda i, h, j: (i * nblk + j, h))]
                  + [pl.BlockSpec((None, rows, HEAD_DIM), lambda i, h, j: (h, i * nblk + j, 0))] * ng,
        out_shape=[jax.ShapeDtypeStruct((m, HW), BF16)]
                  + [jax.ShapeDtypeStruct((N_HEADS, m, HEAD_DIM), F32)] * ng,
        scratch_shapes=scratch,
        compiler_params=_params("parallel", "parallel", "arbitrary"),
        name="attn_prompt",
    )(*([qkv_h] * (3 * ng)), cos, sin)
    return res[0], res[1:]


def _attn_s_kernel(*refs, t, lp, cache_rows):
    ng = len(DIL_GROUPS)
    qkv = refs[0:3 * ng]
    caches = refs[3 * ng:4 * ng]
    cos_ref, sin_ref = refs[4 * ng:4 * ng + 2]
    yc_ref = refs[4 * ng + 2]
    knew = refs[4 * ng + 3:5 * ng + 3]
    qp, kp, vp = refs[5 * ng + 3:5 * ng + 6]
    cos2, sin2 = cos_ref[...], sin_ref[...]
    scale = HEAD_DIM ** -0.5
    qi = lax.broadcasted_iota(jnp.int32, (lp, lp), 0)
    kn_i = lax.broadcasted_iota(jnp.int32, (lp, lp), 1)
    outs = [[None] * ng for _ in range(N_HEADS)]
    lses = [[None] * ng for _ in range(N_HEADS)]
    for g, (window, dil) in enumerate(DIL_GROUPS):
        lb = cache_rows[g]
        for pad, src in ((qp, qkv[3 * g]), (kp, qkv[3 * g + 1]), (vp, qkv[3 * g + 2])):
            pad[...] = jnp.zeros_like(pad)
            pad[0:t, :] = src[...]
        cache = caches[g]
        qc = lax.broadcasted_iota(jnp.int32, (lp, lb), 0)
        kc_i = lax.broadcasted_iota(jnp.int32, (lp, lb), 1)
        back_c = lb + qc - kc_i
        valid_c = ((back_c & (dil - 1)) == 0) & (back_c <= window)
        back_n = qi - kn_i
        valid_n = (back_n >= 0) & ((back_n & (dil - 1)) == 0) & (back_n <= window) & (kn_i < t)
        for h in range(N_HEADS):
            hs = slice(h * HEAD_DIM, (h + 1) * HEAD_DIM)
            q = (_rot(qp[:, hs], cos2, sin2) * scale).astype(BF16)
            k_new = _rot(kp[:, hs], cos2, sin2)
            v_new = vp[:, hs]
            for tt in range(t):
                knew[g][tt * KV_ROWS + h:tt * KV_ROWS + h + 1, :] = k_new[tt:tt + 1, :]
                knew[g][tt * KV_ROWS + N_HEADS + h:tt * KV_ROWS + N_HEADS + h + 1, :] = v_new[tt:tt + 1, :]
            k_old = cache[pl.ds(h, lb, stride=KV_ROWS), :].astype(BF16)
            v_old = cache[pl.ds(N_HEADS + h, lb, stride=KV_ROWS), :].astype(BF16)
            s_c = jnp.where(valid_c, _dot_nt(q, k_old), NEG)
            s_n = jnp.where(valid_n, _dot_nt(q, k_new.astype(BF16)), NEG)
            m = jnp.maximum(jnp.max(s_c, axis=-1, keepdims=True), jnp.max(s_n, axis=-1, keepdims=True))
            p_c = jnp.exp(s_c - m)
            p_n = jnp.exp(s_n - m)
            den = jnp.sum(p_c, axis=-1, keepdims=True) + jnp.sum(p_n, axis=-1, keepdims=True)
            o = (_dot((p_c / den).astype(BF16), v_old)
                 + _dot((p_n / den).astype(BF16), v_new.astype(BF16)))
            outs[h][g] = o
            lses[h][g] = m + jnp.log(den)
    for h in range(N_HEADS):
        hs = slice(h * HEAD_DIM, (h + 1) * HEAD_DIM)
        m = functools.reduce(jnp.maximum, lses[h])
        es = [jnp.exp(a - m) for a in lses[h]]
        den = functools.reduce(lambda a, c: a + c, es)
        y = functools.reduce(lambda a, c: a + c, [(e / den) * o for e, o in zip(es, outs[h])])
        yc_ref[:, hs] = y[0:t, :].astype(BF16)


def _attn_sample(proj3, caches, l, cos, sin):
    b, t, _ = proj3.shape
    lp = cos.shape[0]
    ng = len(DIL_GROUPS)
    cache_rows = tuple(c.shape[2] // KV_ROWS for c in caches)
    tok = lambda c0: pl.BlockSpec((None, t, HW), lambda i: (i, 0, c0))
    in_specs = [tok(n) for n in range(3 * ng)]
    in_specs += [pl.BlockSpec((None, None, lb * KV_ROWS, HEAD_DIM), lambda i: (l, i, 0, 0)) for lb in cache_rows]
    tab = pl.BlockSpec((lp, HEAD_DIM), lambda i: (0, 0))
    in_specs += [tab, tab]
    res = pl.pallas_call(
        functools.partial(_attn_s_kernel, t=t, lp=lp, cache_rows=cache_rows),
        grid=(b,),
        in_specs=in_specs,
        out_specs=[pl.BlockSpec((None, t, HW), lambda i: (i, 0, 0))]
                  + [pl.BlockSpec((None, t * KV_ROWS, HEAD_DIM), lambda i: (i, 0, 0))] * ng,
        out_shape=[jax.ShapeDtypeStruct((b, t, HW), BF16)]
                  + [jax.ShapeDtypeStruct((b, t * KV_ROWS, HEAD_DIM), F32)] * ng,
        scratch_shapes=[pltpu.VMEM((lp, HW), F32)] * 3,
        compiler_params=_params("parallel"),
        name="attn_sample",
    )(*([proj3] * (3 * ng)), *caches, cos, sin)
    return res[0], res[1:]


def _cache_shift_kernel(cur_ref, nxt_ref, new_ref, o_ref, *, ch, shift, n_chunks):
    c = pl.program_id(2)
    o_ref[0:ch - shift, :] = cur_ref[shift:ch, :]
    o_ref[ch - shift:ch, :] = jnp.where(c == n_chunks - 1, new_ref[...], nxt_ref[...])


def _cache_shift(cache, new):
    depth, b, rows, d = cache.shape
    shift = new.shape[2]
    ch = min(rows, 2048)
    n_chunks = rows // ch
    assert n_chunks * ch == rows and ch % shift == 0 and shift % 8 == 0
    per = ch // shift
    last = rows // shift - 1
    return pl.pallas_call(
        functools.partial(_cache_shift_kernel, ch=ch, shift=shift, n_chunks=n_chunks),
        grid=(depth, b, n_chunks),
        in_specs=[pl.BlockSpec((None, None, ch, d), lambda l, i, c: (l, i, c, 0)),
                  pl.BlockSpec((None, None, shift, d), lambda l, i, c: (l, i, jnp.minimum((c + 1) * per, last), 0)),
                  pl.BlockSpec((None, None, shift, d), lambda l, i, c: (l, i, 0, 0))],
        out_specs=pl.BlockSpec((None, None, ch, d), lambda l, i, c: (l, i, c, 0)),
        out_shape=jax.ShapeDtypeStruct(cache.shape, cache.dtype),
        compiler_params=_params("parallel", "parallel", "arbitrary"),
        name="cache_shift",
    )(cache, cache, new)


def _win_pack_kernel(k_ref, v_ref, o_ref, *, rows):
    for h in range(N_HEADS):
        o_ref[pl.ds(h, rows, stride=KV_ROWS), :] = k_ref[h]
        o_ref[pl.ds(N_HEADS + h, rows, stride=KV_ROWS), :] = v_ref[h]


def _win_pack(k_rot, qkv_h, g, b, t, keep):
    ch = min(keep, 512)
    first = (t - keep) // ch
    per_b = t // ch
    v_blk = 3 * g + 2
    return pl.pallas_call(
        functools.partial(_win_pack_kernel, rows=ch),
        grid=(b, keep // ch),
        in_specs=[pl.BlockSpec((N_HEADS, ch, HEAD_DIM), lambda i, c: (0, i * per_b + first + c, 0)),
                  pl.BlockSpec((N_HEADS, ch, HEAD_DIM), lambda i, c: (v_blk, i * per_b + first + c, 0))],
        out_specs=pl.BlockSpec((None, ch * KV_ROWS, HEAD_DIM), lambda i, c: (i, c, 0)),
        out_shape=jax.ShapeDtypeStruct((b, keep * KV_ROWS, HEAD_DIM), F32),
        compiler_params=_params("parallel", "parallel"),
        name="win_pack",
    )(k_rot, qkv_h)


def _merge_kernel(h_ref, ya, yb, yc, wga, wgb, wgc, wa, wb, wc, o_ref):
    h = h_ref[...]
    acc = None
    for y_ref, wg_ref, w_ref in ((ya, wga, wa), (yb, wgb, wb), (yc, wgc, wc)):
        gate = jax.nn.sigmoid(_dot(h, wg_ref[...].astype(BF16)))
        term = gate * _dot(y_ref[...], w_ref[...])
        acc = term if acc is None else acc + term
    o_ref[...] = acc.astype(BF16)


def _merge(h, ya, yb, yc, w_in, wa, wb, wc, l, tm, tn):
    m = h.shape[0]
    act = lambda w: pl.BlockSpec((tm, w), lambda i, j: (i, 0))
    wgt = lambda k: pl.BlockSpec((None, k, tn), lambda i, j: (l, 0, j))
    wgate = lambda n: pl.BlockSpec((None, D_MODEL, tn), lambda i, j: (l, 0, (OFF_GATE + n * D_MODEL) // tn + j))
    return pl.pallas_call(
        _merge_kernel,
        grid=(m // tm, D_MODEL // tn),
        in_specs=[act(D_MODEL), act(CONV_DIM), act(RV_W), act(HW), wgate(0), wgate(1), wgate(2),
                  wgt(CONV_DIM), wgt(RV_W), wgt(HW)],
        out_specs=pl.BlockSpec((tm, tn), lambda i, j: (i, j)),
        out_shape=jax.ShapeDtypeStruct((m, D_MODEL), BF16),
        compiler_params=_params("parallel", "parallel"),
        name="merge",
    )(h, ya, yb, yc, w_in, w_in, w_in, wa, wb, wc)


def _post_norm_residual(x, y, gpost, gate):
    r = y * lax.rsqrt(jnp.mean(y * y, axis=-1, keepdims=True) + EPS) * gpost
    return x + gate * r


def _mod_rows(ref, rs):
    v = ref[0]
    return v if v.shape[0] == 1 else v[rs, :]


def _outproj_kernel(m_ref, w_ref, x_ref, gpost_ref, g1_ref, gffn_ref, sc2_ref, sh2_ref, xo_ref, h2_ref, *, sub):
    w = w_ref[...]
    tm = m_ref.shape[0]
    for r in range(0, tm, sub):
        rs = slice(r, r + sub)
        y = _dot(m_ref[rs, :], w)
        xn = _post_norm_residual(x_ref[rs, :], y, gpost_ref[0], _mod_rows(g1_ref, rs))
        xo_ref[rs, :] = xn
        h2_ref[rs, :] = _norm_mod(xn, gffn_ref[0], _mod_rows(sc2_ref, rs), _mod_rows(sh2_ref, rs)).astype(BF16)


def _outproj(merged, w_out, l, x, gpost, g1, gffn, sc2, sh2, rows):
    m, d = x.shape
    tm = min(rows.tm, 512)
    sub = _Rows(m, tm, None if rows.per_row else rows.tiles_per_b * rows.tm)
    row = pl.BlockSpec((tm, d), lambda i: (i, 0))
    vec = pl.BlockSpec((1, 1, d), lambda i: (l, 0, 0))
    return pl.pallas_call(
        functools.partial(_outproj_kernel, sub=min(tm, 256)),
        grid=(m // tm,),
        in_specs=[row, pl.BlockSpec((None, d, d), lambda i: (l, 0, 0), pipeline_mode=pl.Buffered(1)),
                  row, vec, sub.mod_spec(), vec, sub.mod_spec(), sub.mod_spec()],
        out_specs=[row, row],
        out_shape=[jax.ShapeDtypeStruct((m, d), F32), jax.ShapeDtypeStruct((m, d), BF16)],
        compiler_params=_params("parallel"),
        name="outproj",
    )(merged, w_out, x, gpost, g1, gffn, sc2, sh2)


def _down_kernel(*refs, nk, with_next):
    n_in = 13 if with_next else 8
    n_out = 4 if with_next else 2
    ins, outs, (acc_ref, accs_ref) = refs[:n_in], refs[n_in:n_in + n_out], refs[n_in + n_out:]
    a_ref, as_ref, w_ref, x_ref, xs_ref, gpost_ref, g2_ref, g2s_ref = ins[:8]
    k = pl.program_id(1)

    @pl.when(k == 0)
    def _():
        acc_ref[...] = jnp.zeros_like(acc_ref)
        accs_ref[...] = jnp.zeros_like(accs_ref)

    w = w_ref[...]
    acc_ref[...] += _dot(a_ref[...], w)
    accs_ref[...] += _dot(as_ref[...], w)

    @pl.when(k == nk - 1)
    def _():
        for x_r, acc_r, g2_r, pos in ((x_ref, acc_ref, g2_ref, 0), (xs_ref, accs_ref, g2s_ref, 1)):
            xn = _post_norm_residual(x_r[...], acc_r[...], gpost_ref[0], g2_r[0])
            if with_next:
                gn_ref, scn_ref, shn_ref = ins[8], ins[9 + pos], ins[11 + pos]
                outs[pos][...] = xn
                outs[2 + pos][...] = _norm_mod(xn, gn_ref[0], scn_ref[0], shn_ref[0]).astype(BF16)
            else:
                outs[pos][...] = xn


def _down(a, a_s, w_down, l, x, x_s, gpost, g2, g2_s, nxt, rows):
    m, d = x.shape
    ms = x_s.shape[0]
    kdim = a.shape[1]
    tm = min(rows.tm, 1024)
    tk = 1024
    nk = kdim // tk
    sub = _Rows(m, tm, None if rows.per_row else rows.tiles_per_b * rows.tm)
    one = lambda shape, imap: pl.BlockSpec(shape, imap, pipeline_mode=pl.Buffered(1))
    row = one((tm, d), lambda i, k: (i, 0))
    row_s = pl.BlockSpec((ms, d), lambda i, k: (0, 0))
    mod_s = pl.BlockSpec((1, ms, d), lambda i, k: (0, 0, 0))
    vec = pl.BlockSpec((1, 1, d), lambda i, k: (l, 0, 0))
    in_specs = [pl.BlockSpec((tm, tk), lambda i, k: (i, k)),
                pl.BlockSpec((ms, tk), lambda i, k: (0, k)),
                pl.BlockSpec((None, tk, d), lambda i, k: (l, k, 0)),
                row, row_s, vec, sub.mod_spec(), mod_s]
    args = [a, a_s, w_down, x, x_s, gpost, g2, g2_s]
    out_specs = [row, row_s]
    out_shape = [jax.ShapeDtypeStruct((m, d), F32), jax.ShapeDtypeStruct((ms, d), F32)]
    if nxt is not None:
        gn, scn, shn, scn_s, shn_s = nxt
        in_specs += [pl.BlockSpec((1, 1, d), lambda i, k: (l + 1, 0, 0)), sub.mod_spec(), mod_s, sub.mod_spec(), mod_s]
        args += [gn, scn, scn_s, shn, shn_s]
        out_specs += [row, row_s]
        out_shape += [jax.ShapeDtypeStruct((m, d), BF16), jax.ShapeDtypeStruct((ms, d), BF16)]
    return pl.pallas_call(
        functools.partial(_down_kernel, nk=nk, with_next=nxt is not None),
        grid=(m // tm, nk),
        in_specs=in_specs,
        out_specs=out_specs,
        out_shape=out_shape,
        scratch_shapes=[pltpu.VMEM((tm, d), F32), pltpu.VMEM((ms, d), F32)],
        compiler_params=_params("arbitrary", "arbitrary", vmem=VMEM_LIMIT_MAX),
        name="mlp_down",
    )(*args)


def kernel(x_prompt, x_sample, c_prompt, c_sample, state_conv, state_retention, cache_win128, cache_win512, cache_win2048, norm_mix_g, norm_mix_post_g, norm_ffn_g, norm_ffn_post_g, w_mod, b_mod, w_in, conv_w, conv_b, conv_ln_g, conv_ln_b, w_branch_a, w_branch_b, w_branch_c, w_out, w_up, w_down):
    depth = w_in.shape[0]
    bp, tp, d = x_prompt.shape
    bs, ts, _ = x_sample.shape
    mp, ms = bp * tp, bs * ts
    ng = len(DIL_GROUPS)

    w_a_b, w_b_b, w_c_b, w_out_b, w_down_b = [
        w.astype(BF16) for w in (w_branch_a, w_branch_b, w_branch_c, w_out, w_down)]
    conv_w8 = jnp.broadcast_to(conv_w[:, :, None, :], (depth, CONV_WIDTH, SUBLANES, CONV_DIM))
    vec3 = lambda a: a.reshape(depth, 1, a.shape[-1])
    g_mix, g_mix_post, g_ffn, g_ffn_post = map(vec3, (norm_mix_g, norm_mix_post_g, norm_ffn_g, norm_ffn_post_g))
    conv_b3, ln_g3, ln_b3 = map(vec3, (conv_b, conv_ln_g, conv_ln_b))
    caches = [c.reshape(c.shape[0], c.shape[1], c.shape[2] * KV_ROWS, HEAD_DIM)
              for c in (cache_win128, cache_win512, cache_win2048)]

    lp_s = 16
    rope_p = _rope_tables(np.arange(tp), _rope_inv_freq(), tp)
    rope_s = _rope_tables(PAST_LEN + np.arange(ts), _rope_inv_freq(), lp_s)
    retn_p = _rope_tables(np.arange(tp), _retnet_inv_freq(), tp)
    retn_s = _rope_tables(PAST_LEN + np.arange(ts), _retnet_inv_freq(), lp_s)

    n_c = bp + bs
    c_rows = -(-n_c // 8) * 8
    c_all = jnp.concatenate([c_prompt, c_sample, jnp.zeros((c_rows - n_c, d), F32)], axis=0)
    mod = _modulation(c_all, w_mod, b_mod)

    rows_p = _Rows(mp, 1024, tp)
    rows_s = _Rows(ms, ms, None)

    def mods(l):
        parts_p = [mod[l, :bp, i * d:(i + 1) * d].reshape(bp, 1, d) for i in range(6)]
        parts_s = [jnp.repeat(mod[l, bp:n_c, i * d:(i + 1) * d], ts, axis=0).reshape(1, ms, d) for i in range(6)]
        return parts_p, parts_s

    xp = x_prompt.reshape(mp, d)
    xs = x_sample.reshape(ms, d)
    mod_p, mod_s = mods(0)
    hp = _prenorm(xp, g_mix, 0, mod_p[1], mod_p[0], rows_p)
    hs = _prenorm(xs, g_mix, 0, mod_s[1], mod_s[0], rows_s)

    conv_p, conv_s, ret_p, ret_s = [], [], [], []
    win_p = [[] for _ in range(ng)]
    new_kv_s = [[] for _ in range(ng)]

    for l in range(depth):
        sh1p, sc1p, g1p, sh2p, sc2p, g2p = mod_p
        sh1s, sc1s, g1s, sh2s, sc2s, g2s = mod_s

        proj_ab, proj_ab_s = _matmul(hp, hs, w_in, l, F32, 2048, 512, ncols=OFF_DIL, name="matmul_in_ab")
        qkv_h, qkv_s = _matmul(hp, hs, w_in, l, F32, 2048, HW, col0=OFF_DIL, ncols=OFF_GATE - OFF_DIL,
                               heads=True, name="matmul_in_qkv")

        proj3 = proj_ab.reshape(bp, tp, OFF_DIL)
        ya, st = _conv_prompt(proj3, l, conv_w8, conv_b3, ln_g3, ln_b3)
        conv_p.append(st[:, CONV_HALO - (CONV_WIDTH - 1):])
        yb, s_new = _retention(proj3, retn_p[0], retn_p[1], None, l)
        ret_p.append(s_new)
        yc, k_rot = _attn_prompt(qkv_h, rope_p[0], rope_p[1], bp, tp)
        for g, (window, dil) in enumerate(DIL_GROUPS):
            keep = min(window, tp)
            win_p[g].append(_win_pack(k_rot[g], qkv_h, g, bp, tp, keep))
        merged = _merge(hp, ya.reshape(mp, CONV_DIM), yb.reshape(mp, RV_W), yc, w_in, w_a_b, w_b_b, w_c_b, l, 1024, 256)
        xp, h2 = _outproj(merged, w_out_b, l, xp, g_mix_post, g1p, g_ffn, sc2p, sh2p, rows_p)

        proj_s3 = proj_ab_s.reshape(bs, ts, OFF_DIL)
        ya_s, st_s = _conv_sample(proj_s3, l, state_conv, conv_w, conv_b3, ln_g3, ln_b3)
        conv_s.append(st_s)
        yb_s, s_new_s = _retention(proj_s3, retn_s[0], retn_s[1], state_retention, l)
        ret_s.append(s_new_s)
        yc_s, kv_new = _attn_sample(qkv_s.reshape(bs, ts, OFF_GATE - OFF_DIL), caches, l, rope_s[0], rope_s[1])
        for g in range(ng):
            new_kv_s[g].append(kv_new[g])
        merged_s = _merge(hs, ya_s.reshape(ms, CONV_DIM), yb_s.reshape(ms, RV_W), yc_s.reshape(ms, HW),
                          w_in, w_a_b, w_b_b, w_c_b, l, ms, 256)
        xs, h2_s = _outproj(merged_s, w_out_b, l, xs, g_mix_post, g1s, g_ffn, sc2s, sh2s, rows_s)

        a, a_s = _matmul(h2, h2_s, w_up, l, BF16, 2048, 512, act="relu2", name="mlp_up")
        if l + 1 < depth:
            mod_p_n, mod_s_n = mods(l + 1)
            nxt = (g_mix, mod_p_n[1], mod_p_n[0], mod_s_n[1], mod_s_n[0])
            xp, xs, hp, hs = _down(a, a_s, w_down_b, l, xp, xs, g_ffn_post, g2p, g2s, nxt, rows_p)
            mod_p, mod_s = mod_p_n, mod_s_n
        else:
            xp, xs = _down(a, a_s, w_down_b, l, xp, xs, g_ffn_post, g2p, g2s, None, rows_p)

    win_s = [_cache_shift(caches[g], jnp.stack(new_kv_s[g])) for g in range(ng)]
    kv5 = lambda a: a.reshape(a.shape[0], a.shape[1], a.shape[2] // KV_ROWS, 2, N_HEADS, HEAD_DIM)
    win_p = [kv5(jnp.stack(w)) for w in win_p]
    win_s = [kv5(w) for w in win_s]

    return (xp.reshape(bp, tp, d), xs.reshape(bs, ts, d),
            jnp.stack(conv_p), jnp.stack(conv_s), jnp.stack(ret_p), jnp.stack(ret_s),
            win_p[0], win_s[0], win_p[1], win_s[1], win_p[2], win_s[2])
```

```python
import functools
import math

import numpy as np
import jax
import jax.numpy as jnp
from jax import lax
from jax.experimental import pallas as pl
from jax.experimental.pallas import tpu as pltpu

F32 = jnp.float32
BF16 = jnp.bfloat16

D_MODEL = 2048
PAST_LEN = 16384
HEAD_DIM = 128
CONV_DIM = D_MODEL // 4
CONV_WIDTH = 31
CONV_HALO = 32
N_HEADS = D_MODEL // 512
RET_QK = 128
RET_V = 256
RET_CHUNK = 128
DIL_GROUPS = ((128, 1), (512, 4), (2048, 16))
ATTN_BLOCK = 128
D_FF = 4 * D_MODEL
ROPE_THETA = 10000.0
EPS = 1e-6
HW = N_HEADS * HEAD_DIM
RV_W = N_HEADS * RET_V
KV_ROWS = 2 * N_HEADS
SUBLANES = 8

OFF_CONV_A = 0
OFF_CONV_B = CONV_DIM
OFF_RQ = 2 * CONV_DIM
OFF_RK = OFF_RQ + HW
OFF_RV = OFF_RK + HW
OFF_RG = OFF_RV + RV_W
OFF_DIL = OFF_RG + RV_W
OFF_GATE = OFF_DIL + 3 * len(DIL_GROUPS) * HW
N_IN = OFF_GATE + 3 * D_MODEL

NEG = -1e30
VMEM_LIMIT = 52 * 1024 * 1024
VMEM_LIMIT_MAX = 58 * 1024 * 1024


def _params(*sem, vmem=VMEM_LIMIT):
    return pltpu.CompilerParams(dimension_semantics=sem, vmem_limit_bytes=vmem)


def _silu(x):
    return x * jax.nn.sigmoid(x)


def _rot(x, cos2, sin2):
    return x * cos2 + pltpu.roll(x, HEAD_DIM // 2, axis=1) * sin2


def _dot(a, b):
    return jnp.dot(a, b, preferred_element_type=F32)


def _dot_nt(a, b):
    return lax.dot_general(a, b, (((1,), (1,)), ((), ())), preferred_element_type=F32)


def _dot_tn(a, b):
    return lax.dot_general(a, b, (((0,), (0,)), ((), ())), preferred_element_type=F32)


def _rope_tables(pos, inv_freq, rows):
    ang = np.asarray(pos, np.float64)[:, None] * inv_freq[None, :]
    cos = np.concatenate([np.cos(ang), np.cos(ang)], axis=-1)
    sin = np.concatenate([-np.sin(ang), np.sin(ang)], axis=-1)
    pad = rows - cos.shape[0]
    if pad:
        cos = np.pad(cos, ((0, pad), (0, 0)))
        sin = np.pad(sin, ((0, pad), (0, 0)))
    return jnp.asarray(cos, F32), jnp.asarray(sin, F32)


def _rope_inv_freq():
    return 1.0 / (ROPE_THETA ** (np.arange(0, HEAD_DIM, 2, dtype=np.float64) / HEAD_DIM))


def _retnet_inv_freq():
    return 1.0 / (ROPE_THETA ** np.linspace(0.0, 1.0, RET_QK // 2, dtype=np.float64))


def _ret_decay_tables(rows, lp):
    lg = np.log(1.0 - 2.0 ** (-5.0 - np.arange(N_HEADS, dtype=np.float64)))
    i = np.arange(lp, dtype=np.float64)
    rel = i[:, None] - i[None, :]
    real = (i < rows)
    dmask = np.where((rel >= 0) & real[:, None] & real[None, :],
                     np.exp(np.maximum(rel, 0.0)[None] * lg[:, None, None]), 0.0)
    qdec = np.where(real[None, :], np.exp((i[None, :] + 1.0) * lg[:, None]), 0.0)
    kdec = np.where(real[None, :], np.exp((rows - 1.0 - i[None, :]) * lg[:, None]), 0.0)
    cdec = tuple(float(v) for v in np.exp(rows * lg))
    bl = lambda a: jnp.asarray(np.broadcast_to(a[:, :, None], (N_HEADS, lp, HEAD_DIM)), F32)
    return jnp.asarray(dmask, F32), bl(qdec), bl(kdec), cdec


class _Rows:
    def __init__(self, m, tm, rows_per_batch):
        self.m = m
        self.tm = tm
        self.per_row = rows_per_batch is None
        self.tiles_per_b = None if self.per_row else rows_per_batch // tm

    def mod_spec(self):
        if self.per_row:
            return pl.BlockSpec((1, self.tm, D_MODEL), lambda i, *_: (0, i, 0))
        tpb = self.tiles_per_b
        return pl.BlockSpec((1, 1, D_MODEL), lambda i, *_: (i // tpb, 0, 0))


def _mod_kernel(c_ref, w_ref, b_ref, o_ref):
    c = c_ref[...]
    s = _silu(c).astype(BF16)
    o_ref[0] = _dot(s, w_ref[0].astype(BF16)) + b_ref[0]


def _modulation(c_all, w_mod, b_mod):
    depth, d, n = w_mod.shape
    rows = c_all.shape[0]
    tn = 1024
    return pl.pallas_call(
        _mod_kernel,
        grid=(depth, n // tn),
        in_specs=[pl.BlockSpec((rows, d), lambda l, j: (0, 0)),
                  pl.BlockSpec((1, d, tn), lambda l, j: (l, 0, j)),
                  pl.BlockSpec((1, 1, tn), lambda l, j: (l, 0, j))],
        out_specs=pl.BlockSpec((1, rows, tn), lambda l, j: (l, 0, j)),
        out_shape=jax.ShapeDtypeStruct((depth, rows, n), F32),
        compiler_params=_params("parallel", "parallel"),
        name="modulation",
    )(c_all, w_mod, b_mod.reshape(depth, 1, n))


def _norm_mod(x, g, sc, sh):
    y = x * lax.rsqrt(jnp.mean(x * x, axis=-1, keepdims=True) + EPS) * g
    return y * (1.0 + sc) + sh


def _prenorm_kernel(x_ref, g_ref, sc_ref, sh_ref, h_ref):
    h_ref[...] = _norm_mod(x_ref[...], g_ref[0], sc_ref[0], sh_ref[0]).astype(BF16)


def _prenorm(x, g, l, sc, sh, rows):
    m, d = x.shape
    tm = rows.tm
    return pl.pallas_call(
        _prenorm_kernel,
        grid=(m // tm,),
        in_specs=[pl.BlockSpec((tm, d), lambda i: (i, 0)),
                  pl.BlockSpec((1, 1, d), lambda i: (l, 0, 0)),
                  rows.mod_spec(), rows.mod_spec()],
        out_specs=pl.BlockSpec((tm, d), lambda i: (i, 0)),
        out_shape=jax.ShapeDtypeStruct((m, d), BF16),
        compiler_params=_params("parallel"),
        name="prenorm",
    )(x, g, sc, sh)


def _mm_kernel(x_ref, xs_ref, w_ref, o_ref, os_ref, *, act, heads):
    def product(rows_ref):
        y = _dot(rows_ref[...], w_ref[...].astype(BF16))
        return jnp.square(jnp.maximum(y, 0.0)) if act == "relu2" else y

    y = product(x_ref)
    if heads:
        for h in range(N_HEADS):
            o_ref[h] = y[:, h * HEAD_DIM:(h + 1) * HEAD_DIM]
    else:
        o_ref[...] = y.astype(o_ref.dtype)

    @pl.when(pl.program_id(0) == 0)
    def _():
        os_ref[0] = product(xs_ref).astype(os_ref.dtype)

    @pl.when(pl.program_id(0) > 0)
    def _():
        os_ref[...] = jnp.zeros_like(os_ref)


def _matmul(x, xs, w, l, out_dtype, tm, tn, act=None, col0=0, ncols=None, heads=False, name="matmul"):
    m, k = x.shape
    ms = xs.shape[0]
    n = w.shape[-1] if ncols is None else ncols
    c0 = col0 // tn
    assert c0 * tn == col0 and n % tn == 0 and (not heads or tn == HW)
    if heads:
        out_spec = pl.BlockSpec((N_HEADS, tm, HEAD_DIM), lambda i, j: (j, i, 0))
        out_shape = jax.ShapeDtypeStruct((n // HEAD_DIM, m, HEAD_DIM), out_dtype)
    else:
        out_spec = pl.BlockSpec((tm, tn), lambda i, j: (i, j))
        out_shape = jax.ShapeDtypeStruct((m, n), out_dtype)
    res = pl.pallas_call(
        functools.partial(_mm_kernel, act=act, heads=heads),
        grid=(m // tm, n // tn),
        in_specs=[pl.BlockSpec((tm, k), lambda i, j: (i, 0)),
                  pl.BlockSpec((ms, k), lambda i, j: (0, 0)),
                  pl.BlockSpec((None, k, tn), lambda i, j: (l, 0, c0 + j))],
        out_specs=[out_spec, pl.BlockSpec((1, ms, tn), lambda i, j: (i, 0, j))],
        out_shape=[out_shape, jax.ShapeDtypeStruct((m // tm, ms, n), out_dtype)],
        compiler_params=_params("parallel", "parallel"),
        name=name,
    )(x, xs, w)
    return res[0], res[1][0]


def _conv_ln_silu(full_ref, first, rows, w_ref, cb, lg, lb):
    acc = jnp.zeros((rows, CONV_DIM), F32)
    for j in range(CONV_WIDTH):
        acc = acc + full_ref[first + j:first + j + rows, :] * w_ref[j:j + 1, :]
    y = acc + cb
    yc = y - jnp.mean(y, axis=-1, keepdims=True)
    z = yc * lax.rsqrt(jnp.mean(yc * yc, axis=-1, keepdims=True) + EPS) * lg + lb
    return _silu(z)


def _conv_p_kernel(a_ref, b_ref, ha_ref, hb_ref, w_ref, cb_ref, lg_ref, lb_ref, ya_ref, st_ref, sh_ref,
                   *, tt, rc):
    t = pl.program_id(1)
    u = a_ref[...] * jax.nn.sigmoid(b_ref[...])
    uh = ha_ref[...] * jax.nn.sigmoid(hb_ref[...])
    sh_ref[0, 0:CONV_HALO, :] = jnp.where(t > 0, uh, 0.0)
    sh_ref[0, CONV_HALO:, :] = u
    st_ref[...] = u[tt - CONV_HALO:, :]
    n_sh = tt + CONV_HALO - SUBLANES
    for s in range(1, SUBLANES):
        sh_ref[s, 0:n_sh, :] = sh_ref[0, s:s + n_sh, :]
    skip = CONV_HALO - (CONV_WIDTH - 1)
    cb, lg, lb = cb_ref[...], lg_ref[...], lb_ref[...]
    for r0 in range(0, tt, rc):
        acc = jnp.zeros((rc, CONV_DIM), F32)
        for j in range(CONV_WIDTH):
            first = r0 + skip + j
            s = first % SUBLANES
            wj = jnp.tile(w_ref[j], (rc // SUBLANES, 1))
            acc = acc + sh_ref[s, first - s:first - s + rc, :] * wj
        y = acc + cb
        yc = y - jnp.mean(y, axis=-1, keepdims=True)
        z = yc * lax.rsqrt(jnp.mean(yc * yc, axis=-1, keepdims=True) + EPS) * lg + lb
        ya_ref[r0:r0 + rc, :] = _silu(z).astype(BF16)


def _conv_prompt(proj3, l, conv_w8, conv_b, ln_g, ln_b):
    b, t, _ = proj3.shape
    tt, rc = 256, 32
    hpt = tt // CONV_HALO
    vec = pl.BlockSpec((None, 1, CONV_DIM), lambda i, j: (l, 0, 0))
    return pl.pallas_call(
        functools.partial(_conv_p_kernel, tt=tt, rc=rc),
        grid=(b, t // tt),
        in_specs=[pl.BlockSpec((None, tt, CONV_DIM), lambda i, j: (i, j, 0)),
                  pl.BlockSpec((None, tt, CONV_DIM), lambda i, j: (i, j, 1)),
                  pl.BlockSpec((None, CONV_HALO, CONV_DIM), lambda i, j: (i, jnp.maximum(j * hpt - 1, 0), 0)),
                  pl.BlockSpec((None, CONV_HALO, CONV_DIM), lambda i, j: (i, jnp.maximum(j * hpt - 1, 0), 1)),
                  pl.BlockSpec((None, CONV_WIDTH, SUBLANES, CONV_DIM), lambda i, j: (l, 0, 0, 0)),
                  vec, vec, vec],
        out_specs=[pl.BlockSpec((None, tt, CONV_DIM), lambda i, j: (i, j, 0)),
                   pl.BlockSpec((None, CONV_HALO, CONV_DIM), lambda i, j: (i, 0, 0))],
        out_shape=[jax.ShapeDtypeStruct((b, t, CONV_DIM), BF16),
                   jax.ShapeDtypeStruct((b, CONV_HALO, CONV_DIM), F32)],
        scratch_shapes=[pltpu.VMEM((SUBLANES, tt + CONV_HALO, CONV_DIM), F32)],
        compiler_params=_params("parallel", "arbitrary"),
        name="conv_prompt",
    )(proj3, proj3, proj3, proj3, conv_w8, conv_b, ln_g, ln_b)


def _conv_s_kernel(a_ref, b_ref, st_in_ref, w_ref, cb_ref, lg_ref, lb_ref, ya_ref, st_ref, full_ref, *, t):
    keep = CONV_WIDTH - 1
    u = a_ref[...] * jax.nn.sigmoid(b_ref[...])
    full_ref[0:keep, :] = st_in_ref[...]
    full_ref[keep:keep + t, :] = u
    st_ref[...] = full_ref[t:t + keep, :]
    y = _conv_ln_silu(full_ref, 0, t, w_ref, cb_ref[...], lg_ref[...], lb_ref[...])
    ya_ref[...] = y.astype(BF16)


def _conv_sample(proj3, l, state_conv, conv_w, conv_b, ln_g, ln_b):
    b, t, _ = proj3.shape
    keep = CONV_WIDTH - 1
    vec = pl.BlockSpec((None, 1, CONV_DIM), lambda i: (l, 0, 0))
    return pl.pallas_call(
        functools.partial(_conv_s_kernel, t=t),
        grid=(b,),
        in_specs=[pl.BlockSpec((None, t, CONV_DIM), lambda i: (i, 0, 0)),
                  pl.BlockSpec((None, t, CONV_DIM), lambda i: (i, 0, 1)),
                  pl.BlockSpec((None, None, keep, CONV_DIM), lambda i: (l, i, 0, 0)),
                  pl.BlockSpec((None, CONV_WIDTH, CONV_DIM), lambda i: (l, 0, 0)),
                  vec, vec, vec],
        out_specs=[pl.BlockSpec((None, t, CONV_DIM), lambda i: (i, 0, 0)),
                   pl.BlockSpec((None, keep, CONV_DIM), lambda i: (i, 0, 0))],
        out_shape=[jax.ShapeDtypeStruct((b, t, CONV_DIM), BF16),
                   jax.ShapeDtypeStruct((b, keep, CONV_DIM), F32)],
        scratch_shapes=[pltpu.VMEM((keep + 8 + t, CONV_DIM), F32)],
        compiler_params=_params("parallel"),
        name="conv_sample",
    )(proj3, proj3, state_conv, conv_w, conv_b, ln_g, ln_b)


def _ret_kernel(*refs, rows, lp, has_s0, cdec, n_steps, cpb):
    if has_s0:
        rq, rk, rv, rg, cos, sin, dm, qd, kd, s0, yb, sout, s_ref, qp, kp, vp, gp = refs
    else:
        rq, rk, rv, rg, cos, sin, dm, qd, kd, yb, sout, s_ref = refs
    c = pl.program_id(1)

    @pl.when(c == 0)
    def _():
        if has_s0:
            s_ref[...] = s0[...]
        else:
            s_ref[...] = jnp.zeros_like(s_ref)

    if rows < lp:
        for pad, src in ((qp, rq), (kp, rk), (vp, rv), (gp, rg)):
            pad[...] = jnp.zeros_like(pad)
            pad[0:rows, :] = src[...]
        rq, rk, rv, rg = qp, kp, vp, gp

    scale = RET_QK ** -0.5
    for ci in range(cpb):
        rs = slice(ci * lp, (ci + 1) * lp)
        cos2 = cos[rs, :]
        sin2 = sin[rs, :]
        for h in range(N_HEADS):
            qs = slice(h * RET_QK, (h + 1) * RET_QK)
            vs = slice(h * RET_V, (h + 1) * RET_V)
            q = _rot(rq[rs, qs], cos2, sin2) * scale
            k = _rot(rk[rs, qs], cos2, sin2)
            v = rv[rs, vs].astype(BF16)
            sc = _dot_nt(q.astype(BF16), k.astype(BF16)) * dm[h]
            s_h = s_ref[h]
            o = _dot(sc.astype(BF16), v) + _dot((q * qd[h]).astype(BF16), s_h.astype(BF16))
            s_ref[h] = s_h * cdec[h] + _dot_tn((k * kd[h]).astype(BF16), v)
            o = o * lax.rsqrt(jnp.mean(o * o, axis=-1, keepdims=True) + EPS)
            y = o * _silu(rg[rs, vs])
            yb[ci * rows:(ci + 1) * rows, vs] = y[0:rows, :].astype(BF16)

    @pl.when(c == n_steps - 1)
    def _():
        sout[...] = s_ref[...]


def _retention(proj3, cos, sin, s0, l):
    b, t, _ = proj3.shape
    rows = math.gcd(t, RET_CHUNK)
    lp = max(rows, 16)
    n_chunks = t // rows
    dmask, qdec, kdec, cdec = _ret_decay_tables(rows, lp)
    has_s0 = s0 is not None
    assert has_s0 == (rows < lp)
    cpb = 4 if (rows == lp and n_chunks % 4 == 0) else 1
    n_steps = n_chunks // cpb
    br = rows * cpb
    in_specs = [pl.BlockSpec((None, br, HW), lambda i, c: (i, c, OFF_RQ // HW)),
                pl.BlockSpec((None, br, HW), lambda i, c: (i, c, OFF_RK // HW)),
                pl.BlockSpec((None, br, RV_W), lambda i, c: (i, c, OFF_RV // RV_W)),
                pl.BlockSpec((None, br, RV_W), lambda i, c: (i, c, OFF_RG // RV_W)),
                pl.BlockSpec((lp * cpb, HEAD_DIM), lambda i, c: (c, 0)),
                pl.BlockSpec((lp * cpb, HEAD_DIM), lambda i, c: (c, 0)),
                pl.BlockSpec((N_HEADS, lp, lp), lambda i, c: (0, 0, 0)),
                pl.BlockSpec((N_HEADS, lp, HEAD_DIM), lambda i, c: (0, 0, 0)),
                pl.BlockSpec((N_HEADS, lp, HEAD_DIM), lambda i, c: (0, 0, 0))]
    args = [proj3, proj3, proj3, proj3, cos, sin, dmask, qdec, kdec]
    scratch = [pltpu.VMEM((N_HEADS, RET_QK, RET_V), F32)]
    if has_s0:
        in_specs.append(pl.BlockSpec((None, None, N_HEADS, RET_QK, RET_V), lambda i, c: (l, i, 0, 0, 0)))
        args.append(s0)
        scratch += [pltpu.VMEM((lp, HW), F32), pltpu.VMEM((lp, HW), F32),
                    pltpu.VMEM((lp, RV_W), F32), pltpu.VMEM((lp, RV_W), F32)]
    return pl.pallas_call(
        functools.partial(_ret_kernel, rows=rows, lp=lp, has_s0=has_s0, cdec=cdec, n_steps=n_steps, cpb=cpb),
        grid=(b, n_steps),
        in_specs=in_specs,
        out_specs=[pl.BlockSpec((None, br, RV_W), lambda i, c: (i, c, 0)),
                   pl.BlockSpec((None, N_HEADS, RET_QK, RET_V), lambda i, c: (i, 0, 0, 0))],
        out_shape=[jax.ShapeDtypeStruct((b, t, RV_W), BF16),
                   jax.ShapeDtypeStruct((b, N_HEADS, RET_QK, RET_V), F32)],
        scratch_shapes=scratch,
        compiler_params=_params("parallel", "arbitrary"),
        name="retention",
    )(*args)


ATTN_ROWS = 2048
ROPE_CHUNK = 256


def _rows_at(start, size, stride):
    return pl.ds(start, size, stride=stride) if stride > 1 else pl.ds(start, size)


def _softmax_pv(q, keys, vals, valid):
    s = jnp.where(valid, _dot_nt(q, keys), NEG)
    m = jnp.max(s, axis=-1, keepdims=True)
    p = jnp.exp(s - m)
    den = jnp.sum(p, axis=-1, keepdims=True)
    return _dot((p / den).astype(BF16), vals), m + jnp.log(den)


def _attn_p_kernel(*refs):
    ng = len(DIL_GROUPS)
    qkv = refs[0:3 * ng]
    cos_ref, sin_ref = refs[3 * ng:3 * ng + 2]
    yc_ref = refs[3 * ng + 2]
    krs = refs[3 * ng + 3:4 * ng + 3]
    scr = refs[4 * ng + 3:]
    qrot, o_sc, lse_sc, kps, vps = [scr[n * ng:(n + 1) * ng] for n in range(5)]
    qs, ks, vs = qkv[0::3], qkv[1::3], qkv[2::3]
    i = pl.program_id(2)
    rows, blk = ATTN_ROWS, ATTN_BLOCK
    scale = HEAD_DIM ** -0.5

    @pl.when(i == 0)
    def _():
        for g in range(ng):
            kps[g][...] = jnp.zeros_like(kps[g])
            vps[g][...] = jnp.zeros_like(vps[g])

    for c in range(0, rows, ROPE_CHUNK):
        rs = slice(c, c + ROPE_CHUNK)
        cos2, sin2 = cos_ref[rs, :], sin_ref[rs, :]
        for g in range(ng):
            qrot[g][rs, :] = _rot(qs[g][rs, :], cos2, sin2) * scale
            krs[g][rs, :] = _rot(ks[g][rs, :], cos2, sin2)

    qi = lax.broadcasted_iota(jnp.int32, (blk, 2 * blk), 0)
    ki = lax.broadcasted_iota(jnp.int32, (blk, 2 * blk), 1)
    back = blk + qi - ki
    band = (back >= 0) & (back <= blk)
    band_first = band & (ki >= jnp.where(i > 0, 0, blk))

    for g, (window, dil) in enumerate(DIL_GROUPS):
        assert window // dil == blk
        n_sub = rows // (dil * blk)

        for r in range(dil):
            for jb in range(n_sub):
                at = lambda j, n: _rows_at(r + j * blk * dil, n, dil)
                q = qrot[g][at(jb, blk), :].astype(BF16)
                if jb == 0:
                    keys = jnp.concatenate([kps[g][at(0, blk), :], krs[g][at(0, blk), :]], axis=0)
                    vals = jnp.concatenate([vps[g][at(0, blk), :], vs[g][at(0, blk), :]], axis=0)
                    valid = band_first
                else:
                    keys = krs[g][at(jb - 1, 2 * blk), :]
                    vals = vs[g][at(jb - 1, 2 * blk), :]
                    valid = band
                o, lse = _softmax_pv(q, keys.astype(BF16), vals.astype(BF16), valid)
                o_sc[g][at(jb, blk), :] = o
                lse_sc[g][at(jb, blk), :] = jnp.broadcast_to(lse, (blk, HEAD_DIM))

    for g, (window, dil) in enumerate(DIL_GROUPS):
        keep = dil * blk
        kps[g][...] = krs[g][rows - keep:, :]
        vps[g][...] = vs[g][rows - keep:, :]

    for c in range(0, rows, ROPE_CHUNK):
        rs = slice(c, c + ROPE_CHUNK)
        ls = [lse_sc[g][rs, :] for g in range(ng)]
        m = functools.reduce(jnp.maximum, ls)
        es = [jnp.exp(a - m) for a in ls]
        den = functools.reduce(lambda a, e: a + e, es)
        y = functools.reduce(lambda a, e: a + e, [(es[g] / den) * o_sc[g][rs, :] for g in range(ng)])
        yc_ref[rs, :] = y.astype(BF16)


def _attn_prompt(qkv_h, cos, sin, b, t):
    ng = len(DIL_GROUPS)
    m = b * t
    rows = ATTN_ROWS
    nblk = t // rows
    assert nblk * rows == t

    def head(n):
        return pl.BlockSpec((None, rows, HEAD_DIM), lambda i, h, j: (n * N_HEADS + h, i * nblk + j, 0))

    tab = pl.BlockSpec((rows, HEAD_DIM), lambda i, h, j: (j, 0))
    keeps = [dil * ATTN_BLOCK for _, dil in DIL_GROUPS]
    scratch = ([pltpu.VMEM((rows, HEAD_DIM), F32)] * (3 * ng)
               + [pltpu.VMEM((k, HEAD_DIM), F32) for k in keeps] * 2)
    res = pl.pallas_call(
        _attn_p_kernel,
        grid=(b, N_HEADS, nblk),
        in_specs=[head(n) for n in range(3 * ng)] + [tab, tab],
        out_specs=[pl.BlockSpec((rows, HEAD_DIM), lambda i, h, j: (i * nblk + j, h))]
                  + [pl.BlockSpec((None, rows, HEAD_DIM), lambda i, h, j: (h, i * nblk + j, 0))] * ng,
        out_shape=[jax.ShapeDtypeStruct((m, HW), BF16)]
                  + [jax.ShapeDtypeStruct((N_HEADS, m, HEAD_DIM), F32)] * ng,
        scratch_shapes=scratch,
        compiler_params=_params("parallel", "parallel", "arbitrary"),
        name="attn_prompt",
    )(*([qkv_h] * (3 * ng)), cos, sin)
    return res[0], res[1:]


def _attn_s_kernel(*refs, t, lp, cache_rows):
    ng = len(DIL_GROUPS)
    qkv = refs[0:3 * ng]
    caches = refs[3 * ng:4 * ng]
    cos_ref, sin_ref = refs[4 * ng:4 * ng + 2]
    yc_ref = refs[4 * ng + 2]
    knew = refs[4 * ng + 3:5 * ng + 3]
    qp, kp, vp = refs[5 * ng + 3:5 * ng + 6]
    cos2, sin2 = cos_ref[...], sin_ref[...]
    scale = HEAD_DIM ** -0.5
    qi = lax.broadcasted_iota(jnp.int32, (lp, lp), 0)
    kn_i = lax.broadcasted_iota(jnp.int32, (lp, lp), 1)
    outs = [[None] * ng for _ in range(N_HEADS)]
    lses = [[None] * ng for _ in range(N_HEADS)]
    for g, (window, dil) in enumerate(DIL_GROUPS):
        lb = cache_rows[g]
        for pad, src in ((qp, qkv[3 * g]), (kp, qkv[3 * g + 1]), (vp, qkv[3 * g + 2])):
            pad[...] = jnp.zeros_like(pad)
            pad[0:t, :] = src[...]
        cache = caches[g]
        qc = lax.broadcasted_iota(jnp.int32, (lp, lb), 0)
        kc_i = lax.broadcasted_iota(jnp.int32, (lp, lb), 1)
        back_c = lb + qc - kc_i
        valid_c = ((back_c & (dil - 1)) == 0) & (back_c <= window)
        back_n = qi - kn_i
        valid_n = (back_n >= 0) & ((back_n & (dil - 1)) == 0) & (back_n <= window) & (kn_i < t)
        for h in range(N_HEADS):
            hs = slice(h * HEAD_DIM, (h + 1) * HEAD_DIM)
            q = (_rot(qp[:, hs], cos2, sin2) * scale).astype(BF16)
            k_new = _rot(kp[:, hs], cos2, sin2)
            v_new = vp[:, hs]
            for tt in range(t):
                knew[g][tt * KV_ROWS + h:tt * KV_ROWS + h + 1, :] = k_new[tt:tt + 1, :]
                knew[g][tt * KV_ROWS + N_HEADS + h:tt * KV_ROWS + N_HEADS + h + 1, :] = v_new[tt:tt + 1, :]
            k_old = cache[pl.ds(h, lb, stride=KV_ROWS), :].astype(BF16)
            v_old = cache[pl.ds(N_HEADS + h, lb, stride=KV_ROWS), :].astype(BF16)
            s_c = jnp.where(valid_c, _dot_nt(q, k_old), NEG)
            s_n = jnp.where(valid_n, _dot_nt(q, k_new.astype(BF16)), NEG)
            m = jnp.maximum(jnp.max(s_c, axis=-1, keepdims=True), jnp.max(s_n, axis=-1, keepdims=True))
            p_c = jnp.exp(s_c - m)
            p_n = jnp.exp(s_n - m)
            den = jnp.sum(p_c, axis=-1, keepdims=True) + jnp.sum(p_n, axis=-1, keepdims=True)
            o = (_dot((p_c / den).astype(BF16), v_old)
                 + _dot((p_n / den).astype(BF16), v_new.astype(BF16)))
            outs[h][g] = o
            lses[h][g] = m + jnp.log(den)
    for h in range(N_HEADS):
        hs = slice(h * HEAD_DIM, (h + 1) * HEAD_DIM)
        m = functools.reduce(jnp.maximum, lses[h])
        es = [jnp.exp(a - m) for a in lses[h]]
        den = functools.reduce(lambda a, c: a + c, es)
        y = functools.reduce(lambda a, c: a + c, [(e / den) * o for e, o in zip(es, outs[h])])
        yc_ref[:, hs] = y[0:t, :].astype(BF16)


def _attn_sample(proj3, caches, l, cos, sin):
    b, t, _ = proj3.shape
    lp = cos.shape[0]
    ng = len(DIL_GROUPS)
    cache_rows = tuple(c.shape[2] // KV_ROWS for c in caches)
    tok = lambda c0: pl.BlockSpec((None, t, HW), lambda i: (i, 0, c0))
    in_specs = [tok(n) for n in range(3 * ng)]
    in_specs += [pl.BlockSpec((None, None, lb * KV_ROWS, HEAD_DIM), lambda i: (l, i, 0, 0)) for lb in cache_rows]
    tab = pl.BlockSpec((lp, HEAD_DIM), lambda i: (0, 0))
    in_specs += [tab, tab]
    res = pl.pallas_call(
        functools.partial(_attn_s_kernel, t=t, lp=lp, cache_rows=cache_rows),
        grid=(b,),
        in_specs=in_specs,
        out_specs=[pl.BlockSpec((None, t, HW), lambda i: (i, 0, 0))]
                  + [pl.BlockSpec((None, t * KV_ROWS, HEAD_DIM), lambda i: (i, 0, 0))] * ng,
        out_shape=[jax.ShapeDtypeStruct((b, t, HW), BF16)]
                  + [jax.ShapeDtypeStruct((b, t * KV_ROWS, HEAD_DIM), F32)] * ng,
        scratch_shapes=[pltpu.VMEM((lp, HW), F32)] * 3,
        compiler_params=_params("parallel"),
        name="attn_sample",
    )(*([proj3] * (3 * ng)), *caches, cos, sin)
    return res[0], res[1:]


def _cache_shift_kernel(cur_ref, nxt_ref, new_ref, o_ref, *, ch, shift, n_chunks):
    c = pl.program_id(2)
    o_ref[0:ch - shift, :] = cur_ref[shift:ch, :]
    o_ref[ch - shift:ch, :] = jnp.where(c == n_chunks - 1, new_ref[...], nxt_ref[...])


def _cache_shift(cache, new):
    depth, b, rows, d = cache.shape
    shift = new.shape[2]
    ch = min(rows, 8192)
    n_chunks = rows // ch
    assert n_chunks * ch == rows and ch % shift == 0 and shift % 8 == 0
    per = ch // shift
    last = rows // shift - 1
    return pl.pallas_call(
        functools.partial(_cache_shift_kernel, ch=ch, shift=shift, n_chunks=n_chunks),
        grid=(depth, b, n_chunks),
        in_specs=[pl.BlockSpec((None, None, ch, d), lambda l, i, c: (l, i, c, 0)),
                  pl.BlockSpec((None, None, shift, d), lambda l, i, c: (l, i, jnp.minimum((c + 1) * per, last), 0)),
                  pl.BlockSpec((None, None, shift, d), lambda l, i, c: (l, i, 0, 0))],
        out_specs=pl.BlockSpec((None, None, ch, d), lambda l, i, c: (l, i, c, 0)),
        out_shape=jax.ShapeDtypeStruct(cache.shape, cache.dtype),
        compiler_params=_params("parallel", "parallel", "arbitrary"),
        name="cache_shift",
    )(cache, cache, new)


def _win_pack_kernel(k_ref, v_ref, o_ref, *, rows):
    for h in range(N_HEADS):
        o_ref[pl.ds(h, rows, stride=KV_ROWS), :] = k_ref[h]
        o_ref[pl.ds(N_HEADS + h, rows, stride=KV_ROWS), :] = v_ref[h]


def _win_pack(k_rot, qkv_h, g, b, t, keep):
    ch = min(keep, 512)
    first = (t - keep) // ch
    per_b = t // ch
    v_blk = 3 * g + 2
    return pl.pallas_call(
        functools.partial(_win_pack_kernel, rows=ch),
        grid=(b, keep // ch),
        in_specs=[pl.BlockSpec((N_HEADS, ch, HEAD_DIM), lambda i, c: (0, i * per_b + first + c, 0)),
                  pl.BlockSpec((N_HEADS, ch, HEAD_DIM), lambda i, c: (v_blk, i * per_b + first + c, 0))],
        out_specs=pl.BlockSpec((None, ch * KV_ROWS, HEAD_DIM), lambda i, c: (i, c, 0)),
        out_shape=jax.ShapeDtypeStruct((b, keep * KV_ROWS, HEAD_DIM), F32),
        compiler_params=_params("parallel", "parallel"),
        name="win_pack",
    )(k_rot, qkv_h)


def _merge_kernel(h_ref, ya, yb, yc, wga, wgb, wgc, wa, wb, wc, o_ref, wg_ref):
    tn = o_ref.shape[1]
    for n, src in enumerate((wga, wgb, wgc)):
        wg_ref[:, n * tn:(n + 1) * tn] = src[...].astype(BF16)
    gates = jax.nn.sigmoid(_dot(h_ref[...], wg_ref[...]))
    acc = gates[:, 0:tn] * _dot(ya[...], wa[...])
    acc = acc + gates[:, tn:2 * tn] * _dot(yb[...], wb[...])
    acc = acc + gates[:, 2 * tn:3 * tn] * _dot(yc[...], wc[...])
    o_ref[...] = acc.astype(BF16)


def _merge(h, ya, yb, yc, w_in, wa, wb, wc, l, tm, tn):
    m = h.shape[0]
    act = lambda w: pl.BlockSpec((tm, w), lambda i, j: (i, 0))
    wgt = lambda k: pl.BlockSpec((None, k, tn), lambda i, j: (l, 0, j))
    wgate = lambda n: pl.BlockSpec((None, D_MODEL, tn), lambda i, j: (l, 0, (OFF_GATE + n * D_MODEL) // tn + j))
    return pl.pallas_call(
        _merge_kernel,
        grid=(m // tm, D_MODEL // tn),
        in_specs=[act(D_MODEL), act(CONV_DIM), act(RV_W), act(HW), wgate(0), wgate(1), wgate(2),
                  wgt(CONV_DIM), wgt(RV_W), wgt(HW)],
        out_specs=pl.BlockSpec((tm, tn), lambda i, j: (i, j)),
        out_shape=jax.ShapeDtypeStruct((m, D_MODEL), BF16),
        scratch_shapes=[pltpu.VMEM((D_MODEL, 3 * tn), BF16)],
        compiler_params=_params("parallel", "parallel"),
        name="merge",
    )(h, ya, yb, yc, w_in, w_in, w_in, wa, wb, wc)


def _post_norm_residual(x, y, gpost, gate):
    r = y * lax.rsqrt(jnp.mean(y * y, axis=-1, keepdims=True) + EPS) * gpost
    return x + gate * r


def _mod_rows(ref, rs):
    v = ref[0]
    return v if v.shape[0] == 1 else v[rs, :]


def _outproj_kernel(m_ref, w_ref, x_ref, gpost_ref, g1_ref, gffn_ref, sc2_ref, sh2_ref, xo_ref, h2_ref, *, sub):
    w = w_ref[...]
    tm = m_ref.shape[0]
    for r in range(0, tm, sub):
        rs = slice(r, r + sub)
        y = _dot(m_ref[rs, :], w)
        xn = _post_norm_residual(x_ref[rs, :], y, gpost_ref[0], _mod_rows(g1_ref, rs))
        xo_ref[rs, :] = xn
        h2_ref[rs, :] = _norm_mod(xn, gffn_ref[0], _mod_rows(sc2_ref, rs), _mod_rows(sh2_ref, rs)).astype(BF16)


def _outproj(merged, w_out, l, x, gpost, g1, gffn, sc2, sh2, rows):
    m, d = x.shape
    tm = min(rows.tm, 512)
    sub = _Rows(m, tm, None if rows.per_row else rows.tiles_per_b * rows.tm)
    row = pl.BlockSpec((tm, d), lambda i: (i, 0))
    vec = pl.BlockSpec((1, 1, d), lambda i: (l, 0, 0))
    return pl.pallas_call(
        functools.partial(_outproj_kernel, sub=min(tm, 256)),
        grid=(m // tm,),
        in_specs=[row, pl.BlockSpec((None, d, d), lambda i: (l, 0, 0), pipeline_mode=pl.Buffered(1)),
                  row, vec, sub.mod_spec(), vec, sub.mod_spec(), sub.mod_spec()],
        out_specs=[row, row],
        out_shape=[jax.ShapeDtypeStruct((m, d), F32), jax.ShapeDtypeStruct((m, d), BF16)],
        compiler_params=_params("parallel"),
        name="outproj",
    )(merged, w_out, x, gpost, g1, gffn, sc2, sh2)


def _down_kernel(*refs, nk, with_next):
    n_in = 13 if with_next else 8
    n_out = 4 if with_next else 2
    ins, outs, (acc_ref, accs_ref) = refs[:n_in], refs[n_in:n_in + n_out], refs[n_in + n_out:]
    a_ref, as_ref, w_ref, x_ref, xs_ref, gpost_ref, g2_ref, g2s_ref = ins[:8]
    i = pl.program_id(0)
    k = pl.program_id(1)

    def finish(x_r, acc_r, g2_r, pos):
        xn = _post_norm_residual(x_r[...], acc_r[...], gpost_ref[0], g2_r[0])
        outs[pos][...] = xn
        if with_next:
            gn_ref, scn_ref, shn_ref = ins[8], ins[9 + pos], ins[11 + pos]
            outs[2 + pos][...] = _norm_mod(xn, gn_ref[0], scn_ref[0], shn_ref[0]).astype(BF16)

    @pl.when(k == 0)
    def _():
        acc_ref[...] = jnp.zeros_like(acc_ref)

    acc_ref[...] += _dot(a_ref[...], w_ref[...])

    @pl.when(k == nk - 1)
    def _():
        finish(x_ref, acc_ref, g2_ref, 0)

    @pl.when(i == 0)
    def _():
        @pl.when(k == 0)
        def _():
            accs_ref[...] = jnp.zeros_like(accs_ref)

        accs_ref[...] += _dot(as_ref[...], w_ref[...])

        @pl.when(k == nk - 1)
        def _():
            finish(xs_ref, accs_ref, g2s_ref, 1)


def _down(a, a_s, w_down, l, x, x_s, gpost, g2, g2_s, nxt, rows):
    m, d = x.shape
    ms = x_s.shape[0]
    kdim = a.shape[1]
    tm = min(rows.tm, 1024)
    tk = 1024
    nk = kdim // tk
    sub = _Rows(m, tm, None if rows.per_row else rows.tiles_per_b * rows.tm)
    one = lambda shape, imap: pl.BlockSpec(shape, imap, pipeline_mode=pl.Buffered(1))
    row = one((tm, d), lambda i, k: (i, 0))
    row_s = pl.BlockSpec((ms, d), lambda i, k: (0, 0))
    mod_s = pl.BlockSpec((1, ms, d), lambda i, k: (0, 0, 0))
    vec = pl.BlockSpec((1, 1, d), lambda i, k: (l, 0, 0))
    in_specs = [pl.BlockSpec((tm, tk), lambda i, k: (i, k)),
                pl.BlockSpec((ms, tk), lambda i, k: (0, k)),
                pl.BlockSpec((None, tk, d), lambda i, k: (l, k, 0)),
                row, row_s, vec, sub.mod_spec(), mod_s]
    args = [a, a_s, w_down, x, x_s, gpost, g2, g2_s]
    out_specs = [row, row_s]
    out_shape = [jax.ShapeDtypeStruct((m, d), F32), jax.ShapeDtypeStruct((ms, d), F32)]
    if nxt is not None:
        gn, scn, shn, scn_s, shn_s = nxt
        in_specs += [pl.BlockSpec((1, 1, d), lambda i, k: (l + 1, 0, 0)), sub.mod_spec(), mod_s, sub.mod_spec(), mod_s]
        args += [gn, scn, scn_s, shn, shn_s]
        out_specs += [row, row_s]
        out_shape += [jax.ShapeDtypeStruct((m, d), BF16), jax.ShapeDtypeStruct((ms, d), BF16)]
    return pl.pallas_call(
        functools.partial(_down_kernel, nk=nk, with_next=nxt is not None),
        grid=(m // tm, nk),
        in_specs=in_specs,
        out_specs=out_specs,
        out_shape=out_shape,
        scratch_shapes=[pltpu.VMEM((tm, d), F32), pltpu.VMEM((ms, d), F32)],
        compiler_params=_params("arbitrary", "arbitrary", vmem=VMEM_LIMIT_MAX),
        name="mlp_down",
    )(*args)


def kernel(x_prompt, x_sample, c_prompt, c_sample, state_conv, state_retention, cache_win128, cache_win512, cache_win2048, norm_mix_g, norm_mix_post_g, norm_ffn_g, norm_ffn_post_g, w_mod, b_mod, w_in, conv_w, conv_b, conv_ln_g, conv_ln_b, w_branch_a, w_branch_b, w_branch_c, w_out, w_up, w_down):
    depth = w_in.shape[0]
    bp, tp, d = x_prompt.shape
    bs, ts, _ = x_sample.shape
    mp, ms = bp * tp, bs * ts
    ng = len(DIL_GROUPS)

    w_a_b, w_b_b, w_c_b, w_out_b, w_down_b = [
        w.astype(BF16) for w in (w_branch_a, w_branch_b, w_branch_c, w_out, w_down)]
    conv_w8 = jnp.broadcast_to(conv_w[:, :, None, :], (depth, CONV_WIDTH, SUBLANES, CONV_DIM))
    vec3 = lambda a: a.reshape(depth, 1, a.shape[-1])
    g_mix, g_mix_post, g_ffn, g_ffn_post = map(vec3, (norm_mix_g, norm_mix_post_g, norm_ffn_g, norm_ffn_post_g))
    conv_b3, ln_g3, ln_b3 = map(vec3, (conv_b, conv_ln_g, conv_ln_b))
    caches = [c.reshape(c.shape[0], c.shape[1], c.shape[2] * KV_ROWS, HEAD_DIM)
              for c in (cache_win128, cache_win512, cache_win2048)]

    lp_s = 16
    rope_p = _rope_tables(np.arange(tp), _rope_inv_freq(), tp)
    rope_s = _rope_tables(PAST_LEN + np.arange(ts), _rope_inv_freq(), lp_s)
    retn_p = _rope_tables(np.arange(tp), _retnet_inv_freq(), tp)
    retn_s = _rope_tables(PAST_LEN + np.arange(ts), _retnet_inv_freq(), lp_s)

    n_c = bp + bs
    c_rows = -(-n_c // 8) * 8
    c_all = jnp.concatenate([c_prompt, c_sample, jnp.zeros((c_rows - n_c, d), F32)], axis=0)
    mod = _modulation(c_all, w_mod, b_mod)

    rows_p = _Rows(mp, 1024, tp)
    rows_s = _Rows(ms, ms, None)

    def mods(l):
        parts_p = [mod[l, :bp, i * d:(i + 1) * d].reshape(bp, 1, d) for i in range(6)]
        parts_s = [jnp.repeat(mod[l, bp:n_c, i * d:(i + 1) * d], ts, axis=0).reshape(1, ms, d) for i in range(6)]
        return parts_p, parts_s

    xp = x_prompt.reshape(mp, d)
    xs = x_sample.reshape(ms, d)
    mod_p, mod_s = mods(0)
    hp = _prenorm(xp, g_mix, 0, mod_p[1], mod_p[0], rows_p)
    hs = _prenorm(xs, g_mix, 0, mod_s[1], mod_s[0], rows_s)

    conv_p, conv_s, ret_p, ret_s = [], [], [], []
    win_p = [[] for _ in range(ng)]
    new_kv_s = [[] for _ in range(ng)]

    for l in range(depth):
        sh1p, sc1p, g1p, sh2p, sc2p, g2p = mod_p
        sh1s, sc1s, g1s, sh2s, sc2s, g2s = mod_s

        proj_ab, proj_ab_s = _matmul(hp, hs, w_in, l, F32, 2048, 512, ncols=OFF_DIL, name="matmul_in_ab")
        qkv_h, qkv_s = _matmul(hp, hs, w_in, l, F32, 2048, HW, col0=OFF_DIL, ncols=OFF_GATE - OFF_DIL,
                               heads=True, name="matmul_in_qkv")

        proj3 = proj_ab.reshape(bp, tp, OFF_DIL)
        ya, st = _conv_prompt(proj3, l, conv_w8, conv_b3, ln_g3, ln_b3)
        conv_p.append(st[:, CONV_HALO - (CONV_WIDTH - 1):])
        yb, s_new = _retention(proj3, retn_p[0], retn_p[1], None, l)
        ret_p.append(s_new)
        yc, k_rot = _attn_prompt(qkv_h, rope_p[0], rope_p[1], bp, tp)
        for g, (window, dil) in enumerate(DIL_GROUPS):
            keep = min(window, tp)
            win_p[g].append(_win_pack(k_rot[g], qkv_h, g, bp, tp, keep))
        merged = _merge(hp, ya.reshape(mp, CONV_DIM), yb.reshape(mp, RV_W), yc, w_in, w_a_b, w_b_b, w_c_b, l, 1024, 256)
        xp, h2 = _outproj(merged, w_out_b, l, xp, g_mix_post, g1p, g_ffn, sc2p, sh2p, rows_p)

        proj_s3 = proj_ab_s.reshape(bs, ts, OFF_DIL)
        ya_s, st_s = _conv_sample(proj_s3, l, state_conv, conv_w, conv_b3, ln_g3, ln_b3)
        conv_s.append(st_s)
        yb_s, s_new_s = _retention(proj_s3, retn_s[0], retn_s[1], state_retention, l)
        ret_s.append(s_new_s)
        yc_s, kv_new = _attn_sample(qkv_s.reshape(bs, ts, OFF_GATE - OFF_DIL), caches, l, rope_s[0], rope_s[1])
        for g in range(ng):
            new_kv_s[g].append(kv_new[g])
        merged_s = _merge(hs, ya_s.reshape(ms, CONV_DIM), yb_s.reshape(ms, RV_W), yc_s.reshape(ms, HW),
                          w_in, w_a_b, w_b_b, w_c_b, l, ms, 256)
        xs, h2_s = _outproj(merged_s, w_out_b, l, xs, g_mix_post, g1s, g_ffn, sc2s, sh2s, rows_s)

        a, a_s = _matmul(h2, h2_s, w_up, l, BF16, 2048, 512, act="relu2", name="mlp_up")
        if l + 1 < depth:
            mod_p_n, mod_s_n = mods(l + 1)
            nxt = (g_mix, mod_p_n[1], mod_p_n[0], mod_s_n[1], mod_s_n[0])
            xp, xs, hp, hs = _down(a, a_s, w_down_b, l, xp, xs, g_ffn_post, g2p, g2s, nxt, rows_p)
            mod_p, mod_s = mod_p_n, mod_s_n
        else:
            xp, xs = _down(a, a_s, w_down_b, l, xp, xs, g_ffn_post, g2p, g2s, None, rows_p)

    win_s = [_cache_shift(caches[g], jnp.stack(new_kv_s[g])) for g in range(ng)]
    kv5 = lambda a: a.reshape(a.shape[0], a.shape[1], a.shape[2] // KV_ROWS, 2, N_HEADS, HEAD_DIM)
    win_p = [kv5(jnp.stack(w)) for w in win_p]
    win_s = [kv5(w) for w in win_s]

    return (xp.reshape(bp, tp, d), xs.reshape(bs, ts, d),
            jnp.stack(conv_p), jnp.stack(conv_s), jnp.stack(ret_p), jnp.stack(ret_s),
            win_p[0], win_s[0], win_p[1], win_s[1], win_p[2], win_s[2])
```

```python
import functools
import math

import numpy as np
import jax
import jax.numpy as jnp
from jax import lax
from jax.experimental import pallas as pl
from jax.experimental.pallas import tpu as pltpu

F32 = jnp.float32
BF16 = jnp.bfloat16

D_MODEL = 2048
PAST_LEN = 16384
HEAD_DIM = 128
CONV_DIM = D_MODEL // 4
CONV_WIDTH = 31
CONV_HALO = 32
N_HEADS = D_MODEL // 512
RET_QK = 128
RET_V = 256
RET_CHUNK = 128
DIL_GROUPS = ((128, 1), (512, 4), (2048, 16))
ATTN_BLOCK = 128
D_FF = 4 * D_MODEL
ROPE_THETA = 10000.0
EPS = 1e-6
HW = N_HEADS * HEAD_DIM
RV_W = N_HEADS * RET_V
KV_ROWS = 2 * N_HEADS
SUBLANES = 8

OFF_CONV_A = 0
OFF_CONV_B = CONV_DIM
OFF_RQ = 2 * CONV_DIM
OFF_RK = OFF_RQ + HW
OFF_RV = OFF_RK + HW
OFF_RG = OFF_RV + RV_W
OFF_DIL = OFF_RG + RV_W
OFF_GATE = OFF_DIL + 3 * len(DIL_GROUPS) * HW
N_IN = OFF_GATE + 3 * D_MODEL

NEG = -1e30
VMEM_LIMIT = 52 * 1024 * 1024
VMEM_LIMIT_MAX = 58 * 1024 * 1024


def _params(*sem, vmem=VMEM_LIMIT):
    return pltpu.CompilerParams(dimension_semantics=sem, vmem_limit_bytes=vmem)


def _silu(x):
    return x * jax.nn.sigmoid(x)


def _rot(x, cos2, sin2):
    return x * cos2 + pltpu.roll(x, HEAD_DIM // 2, axis=1) * sin2


def _dot(a, b):
    return jnp.dot(a, b, preferred_element_type=F32)


def _dot_nt(a, b):
    return lax.dot_general(a, b, (((1,), (1,)), ((), ())), preferred_element_type=F32)


def _dot_tn(a, b):
    return lax.dot_general(a, b, (((0,), (0,)), ((), ())), preferred_element_type=F32)


def _rope_tables(pos, inv_freq, rows):
    ang = np.asarray(pos, np.float64)[:, None] * inv_freq[None, :]
    cos = np.concatenate([np.cos(ang), np.cos(ang)], axis=-1)
    sin = np.concatenate([-np.sin(ang), np.sin(ang)], axis=-1)
    pad = rows - cos.shape[0]
    if pad:
        cos = np.pad(cos, ((0, pad), (0, 0)))
        sin = np.pad(sin, ((0, pad), (0, 0)))
    return jnp.asarray(cos, F32), jnp.asarray(sin, F32)


def _qkv_rope_tables(pos):
    cos, sin = _rope_tables(pos, _rope_inv_freq(), len(pos))
    scale = HEAD_DIM ** -0.5
    return (jnp.stack([cos * scale, cos, jnp.ones_like(cos)]),
            jnp.stack([sin * scale, sin, jnp.zeros_like(sin)]))


def _rope_inv_freq():
    return 1.0 / (ROPE_THETA ** (np.arange(0, HEAD_DIM, 2, dtype=np.float64) / HEAD_DIM))


def _retnet_inv_freq():
    return 1.0 / (ROPE_THETA ** np.linspace(0.0, 1.0, RET_QK // 2, dtype=np.float64))


def _ret_decay_tables(rows, lp):
    lg = np.log(1.0 - 2.0 ** (-5.0 - np.arange(N_HEADS, dtype=np.float64)))
    i = np.arange(lp, dtype=np.float64)
    rel = i[:, None] - i[None, :]
    real = (i < rows)
    dmask = np.where((rel >= 0) & real[:, None] & real[None, :],
                     np.exp(np.maximum(rel, 0.0)[None] * lg[:, None, None]), 0.0)
    qdec = np.where(real[None, :], np.exp((i[None, :] + 1.0) * lg[:, None]), 0.0)
    kdec = np.where(real[None, :], np.exp((rows - 1.0 - i[None, :]) * lg[:, None]), 0.0)
    cdec = tuple(float(v) for v in np.exp(rows * lg))
    bl = lambda a: jnp.asarray(np.broadcast_to(a[:, :, None], (N_HEADS, lp, HEAD_DIM)), F32)
    return jnp.asarray(dmask, F32), bl(qdec), bl(kdec), cdec


class _Rows:
    def __init__(self, m, tm, rows_per_batch):
        self.m = m
        self.tm = tm
        self.per_row = rows_per_batch is None
        self.tiles_per_b = None if self.per_row else rows_per_batch // tm

    def mod_spec(self):
        if self.per_row:
            return pl.BlockSpec((1, self.tm, D_MODEL), lambda i, *_: (0, i, 0))
        tpb = self.tiles_per_b
        return pl.BlockSpec((1, 1, D_MODEL), lambda i, *_: (i // tpb, 0, 0))


def _mod_kernel(c_ref, w_ref, b_ref, o_ref):
    c = c_ref[...]
    s = _silu(c).astype(BF16)
    o_ref[0] = _dot(s, w_ref[0].astype(BF16)) + b_ref[0]


def _modulation(c_all, w_mod, b_mod):
    depth, d, n = w_mod.shape
    rows = c_all.shape[0]
    tn = 1024
    return pl.pallas_call(
        _mod_kernel,
        grid=(depth, n // tn),
        in_specs=[pl.BlockSpec((rows, d), lambda l, j: (0, 0)),
                  pl.BlockSpec((1, d, tn), lambda l, j: (l, 0, j)),
                  pl.BlockSpec((1, 1, tn), lambda l, j: (l, 0, j))],
        out_specs=pl.BlockSpec((1, rows, tn), lambda l, j: (l, 0, j)),
        out_shape=jax.ShapeDtypeStruct((depth, rows, n), F32),
        compiler_params=_params("parallel", "parallel"),
        name="modulation",
    )(c_all, w_mod, b_mod.reshape(depth, 1, n))


def _norm_mod(x, g, sc, sh):
    y = x * lax.rsqrt(jnp.mean(x * x, axis=-1, keepdims=True) + EPS) * g
    return y * (1.0 + sc) + sh


def _prenorm_kernel(x_ref, g_ref, sc_ref, sh_ref, h_ref):
    h_ref[...] = _norm_mod(x_ref[...], g_ref[0], sc_ref[0], sh_ref[0]).astype(BF16)


def _prenorm(x, g, l, sc, sh, rows):
    m, d = x.shape
    tm = rows.tm
    return pl.pallas_call(
        _prenorm_kernel,
        grid=(m // tm,),
        in_specs=[pl.BlockSpec((tm, d), lambda i: (i, 0)),
                  pl.BlockSpec((1, 1, d), lambda i: (l, 0, 0)),
                  rows.mod_spec(), rows.mod_spec()],
        out_specs=pl.BlockSpec((tm, d), lambda i: (i, 0)),
        out_shape=jax.ShapeDtypeStruct((m, d), BF16),
        compiler_params=_params("parallel"),
        name="prenorm",
    )(x, g, sc, sh)


def _mm_kernel(*refs, act, heads, rope):
    if rope:
        x_ref, xs_ref, w_ref, c_ref, s_ref, cs_ref, ss_ref, o_ref, os_ref = refs
    else:
        x_ref, xs_ref, w_ref, o_ref, os_ref = refs

    def product(rows_ref, cos_ref=None, sin_ref=None):
        y = _dot(rows_ref[...], w_ref[...].astype(BF16))
        if act == "relu2":
            y = jnp.square(jnp.maximum(y, 0.0))
        if rope:
            cos2, sin2 = cos_ref[...], sin_ref[...]
            y = jnp.concatenate([_rot(y[:, h * HEAD_DIM:(h + 1) * HEAD_DIM], cos2, sin2)
                                 for h in range(y.shape[1] // HEAD_DIM)], axis=1)
        return y

    y = product(x_ref, *((c_ref, s_ref) if rope else ()))
    if heads:
        for h in range(o_ref.shape[0]):
            o_ref[h] = y[:, h * HEAD_DIM:(h + 1) * HEAD_DIM]
    else:
        o_ref[...] = y.astype(o_ref.dtype)

    @pl.when(pl.program_id(0) == 0)
    def _():
        os_ref[0] = product(xs_ref, *((cs_ref, ss_ref) if rope else ())).astype(os_ref.dtype)

    @pl.when(pl.program_id(0) > 0)
    def _():
        os_ref[...] = jnp.zeros_like(os_ref)


def _matmul(x, xs, w, l, out_dtype, tm, tn, act=None, col0=0, ncols=None, heads=False, rope=None, name="matmul"):
    m, k = x.shape
    ms = xs.shape[0]
    n = w.shape[-1] if ncols is None else ncols
    c0 = col0 // tn
    assert c0 * tn == col0 and n % tn == 0 and tn % HEAD_DIM == 0
    if heads:
        out_spec = pl.BlockSpec((tn // HEAD_DIM, tm, HEAD_DIM), lambda i, j: (j, i, 0))
        out_shape = jax.ShapeDtypeStruct((n // HEAD_DIM, m, HEAD_DIM), out_dtype)
    else:
        out_spec = pl.BlockSpec((tm, tn), lambda i, j: (i, j))
        out_shape = jax.ShapeDtypeStruct((m, n), out_dtype)
    in_specs = [pl.BlockSpec((tm, k), lambda i, j: (i, 0)),
                pl.BlockSpec((ms, k), lambda i, j: (0, 0)),
                pl.BlockSpec((None, k, tn), lambda i, j: (l, 0, c0 + j))]
    args = [x, xs, w]
    if rope is not None:
        period, t = rope[0].shape[0], rope[0].shape[1]
        per_seq = t // tm
        assert per_seq * tm == t
        tab = pl.BlockSpec((None, tm, HEAD_DIM), lambda i, j: (j % period, i % per_seq, 0))
        tab_s = pl.BlockSpec((None, ms, HEAD_DIM), lambda i, j: (j % period, 0, 0))
        in_specs += [tab, tab, tab_s, tab_s]
        args += list(rope)
    res = pl.pallas_call(
        functools.partial(_mm_kernel, act=act, heads=heads, rope=rope is not None),
        grid=(m // tm, n // tn),
        in_specs=in_specs,
        out_specs=[out_spec, pl.BlockSpec((1, ms, tn), lambda i, j: (i, 0, j))],
        out_shape=[out_shape, jax.ShapeDtypeStruct((m // tm, ms, n), out_dtype)],
        compiler_params=_params("parallel", "parallel"),
        name=name,
    )(*args)
    return res[0], res[1][0]


def _conv_ln_silu(full_ref, first, rows, w_ref, cb, lg, lb):
    acc = jnp.zeros((rows, CONV_DIM), F32)
    for j in range(CONV_WIDTH):
        acc = acc + full_ref[first + j:first + j + rows, :] * w_ref[j:j + 1, :]
    y = acc + cb
    yc = y - jnp.mean(y, axis=-1, keepdims=True)
    z = yc * lax.rsqrt(jnp.mean(yc * yc, axis=-1, keepdims=True) + EPS) * lg + lb
    return _silu(z)


def _conv_p_kernel(a_ref, b_ref, ha_ref, hb_ref, w_ref, cb_ref, lg_ref, lb_ref, ya_ref, st_ref, sh_ref,
                   *, tt, rc):
    t = pl.program_id(1)
    u = a_ref[...] * jax.nn.sigmoid(b_ref[...])
    uh = ha_ref[...] * jax.nn.sigmoid(hb_ref[...])
    sh_ref[0, 0:CONV_HALO, :] = jnp.where(t > 0, uh, 0.0)
    sh_ref[0, CONV_HALO:, :] = u
    st_ref[...] = u[tt - CONV_HALO:, :]
    n_sh = tt + CONV_HALO - SUBLANES
    for s in range(1, SUBLANES):
        sh_ref[s, 0:n_sh, :] = sh_ref[0, s:s + n_sh, :]
    skip = CONV_HALO - (CONV_WIDTH - 1)
    cb, lg, lb = cb_ref[...], lg_ref[...], lb_ref[...]
    for r0 in range(0, tt, rc):
        acc = jnp.zeros((rc, CONV_DIM), F32)
        for j in range(CONV_WIDTH):
            first = r0 + skip + j
            s = first % SUBLANES
            wj = jnp.tile(w_ref[j], (rc // SUBLANES, 1))
            acc = acc + sh_ref[s, first - s:first - s + rc, :] * wj
        y = acc + cb
        yc = y - jnp.mean(y, axis=-1, keepdims=True)
        z = yc * lax.rsqrt(jnp.mean(yc * yc, axis=-1, keepdims=True) + EPS) * lg + lb
        ya_ref[r0:r0 + rc, :] = _silu(z).astype(BF16)


def _conv_prompt(proj3, l, conv_w8, conv_b, ln_g, ln_b):
    b, t, _ = proj3.shape
    tt, rc = 256, 32
    hpt = tt // CONV_HALO
    vec = pl.BlockSpec((None, 1, CONV_DIM), lambda i, j: (l, 0, 0))
    return pl.pallas_call(
        functools.partial(_conv_p_kernel, tt=tt, rc=rc),
        grid=(b, t // tt),
        in_specs=[pl.BlockSpec((None, tt, CONV_DIM), lambda i, j: (i, j, 0)),
                  pl.BlockSpec((None, tt, CONV_DIM), lambda i, j: (i, j, 1)),
                  pl.BlockSpec((None, CONV_HALO, CONV_DIM), lambda i, j: (i, jnp.maximum(j * hpt - 1, 0), 0)),
                  pl.BlockSpec((None, CONV_HALO, CONV_DIM), lambda i, j: (i, jnp.maximum(j * hpt - 1, 0), 1)),
                  pl.BlockSpec((None, CONV_WIDTH, SUBLANES, CONV_DIM), lambda i, j: (l, 0, 0, 0)),
                  vec, vec, vec],
        out_specs=[pl.BlockSpec((None, tt, CONV_DIM), lambda i, j: (i, j, 0)),
                   pl.BlockSpec((None, CONV_HALO, CONV_DIM), lambda i, j: (i, 0, 0))],
        out_shape=[jax.ShapeDtypeStruct((b, t, CONV_DIM), BF16),
                   jax.ShapeDtypeStruct((b, CONV_HALO, CONV_DIM), F32)],
        scratch_shapes=[pltpu.VMEM((SUBLANES, tt + CONV_HALO, CONV_DIM), F32)],
        compiler_params=_params("parallel", "arbitrary"),
        name="conv_prompt",
    )(proj3, proj3, proj3, proj3, conv_w8, conv_b, ln_g, ln_b)


def _conv_s_kernel(a_ref, b_ref, st_in_ref, w_ref, cb_ref, lg_ref, lb_ref, ya_ref, st_ref, full_ref, *, t):
    keep = CONV_WIDTH - 1
    u = a_ref[...] * jax.nn.sigmoid(b_ref[...])
    full_ref[0:keep, :] = st_in_ref[...]
    full_ref[keep:keep + t, :] = u
    st_ref[...] = full_ref[t:t + keep, :]
    y = _conv_ln_silu(full_ref, 0, t, w_ref, cb_ref[...], lg_ref[...], lb_ref[...])
    ya_ref[...] = y.astype(BF16)


def _conv_sample(proj3, l, state_conv, conv_w, conv_b, ln_g, ln_b):
    b, t, _ = proj3.shape
    keep = CONV_WIDTH - 1
    vec = pl.BlockSpec((None, 1, CONV_DIM), lambda i: (l, 0, 0))
    return pl.pallas_call(
        functools.partial(_conv_s_kernel, t=t),
        grid=(b,),
        in_specs=[pl.BlockSpec((None, t, CONV_DIM), lambda i: (i, 0, 0)),
                  pl.BlockSpec((None, t, CONV_DIM), lambda i: (i, 0, 1)),
                  pl.BlockSpec((None, None, keep, CONV_DIM), lambda i: (l, i, 0, 0)),
                  pl.BlockSpec((None, CONV_WIDTH, CONV_DIM), lambda i: (l, 0, 0)),
                  vec, vec, vec],
        out_specs=[pl.BlockSpec((None, t, CONV_DIM), lambda i: (i, 0, 0)),
                   pl.BlockSpec((None, keep, CONV_DIM), lambda i: (i, 0, 0))],
        out_shape=[jax.ShapeDtypeStruct((b, t, CONV_DIM), BF16),
                   jax.ShapeDtypeStruct((b, keep, CONV_DIM), F32)],
        scratch_shapes=[pltpu.VMEM((keep + 8 + t, CONV_DIM), F32)],
        compiler_params=_params("parallel"),
        name="conv_sample",
    )(proj3, proj3, state_conv, conv_w, conv_b, ln_g, ln_b)


def _ret_kernel(*refs, rows, lp, has_s0, cdec, n_steps, cpb):
    if has_s0:
        rq, rk, rv, rg, cos, sin, dm, qd, kd, s0, yb, sout, s_ref, qp, kp, vp, gp = refs
    else:
        rq, rk, rv, rg, cos, sin, dm, qd, kd, yb, sout, s_ref = refs
    c = pl.program_id(1)

    @pl.when(c == 0)
    def _():
        if has_s0:
            s_ref[...] = s0[...]
        else:
            s_ref[...] = jnp.zeros_like(s_ref)

    if rows < lp:
        for pad, src in ((qp, rq), (kp, rk), (vp, rv), (gp, rg)):
            pad[...] = jnp.zeros_like(pad)
            pad[0:rows, :] = src[...]
        rq, rk, rv, rg = qp, kp, vp, gp

    scale = RET_QK ** -0.5
    for ci in range(cpb):
        rs = slice(ci * lp, (ci + 1) * lp)
        cos2 = cos[rs, :]
        sin2 = sin[rs, :]
        for h in range(N_HEADS):
            qs = slice(h * RET_QK, (h + 1) * RET_QK)
            vs = slice(h * RET_V, (h + 1) * RET_V)
            q = _rot(rq[rs, qs], cos2, sin2) * scale
            k = _rot(rk[rs, qs], cos2, sin2)
            v = rv[rs, vs].astype(BF16)
            sc = _dot_nt(q.astype(BF16), k.astype(BF16)) * dm[h]
            s_h = s_ref[h]
            o = _dot(sc.astype(BF16), v) + _dot((q * qd[h]).astype(BF16), s_h.astype(BF16))
            s_ref[h] = s_h * cdec[h] + _dot_tn((k * kd[h]).astype(BF16), v)
            o = o * lax.rsqrt(jnp.mean(o * o, axis=-1, keepdims=True) + EPS)
            y = o * _silu(rg[rs, vs])
            yb[ci * rows:(ci + 1) * rows, vs] = y[0:rows, :].astype(BF16)

    @pl.when(c == n_steps - 1)
    def _():
        sout[...] = s_ref[...]


def _retention(proj3, cos, sin, s0, l):
    b, t, _ = proj3.shape
    rows = math.gcd(t, RET_CHUNK)
    lp = max(rows, 16)
    n_chunks = t // rows
    dmask, qdec, kdec, cdec = _ret_decay_tables(rows, lp)
    has_s0 = s0 is not None
    assert has_s0 == (rows < lp)
    cpb = 4 if (rows == lp and n_chunks % 4 == 0) else 1
    n_steps = n_chunks // cpb
    br = rows * cpb
    in_specs = [pl.BlockSpec((None, br, HW), lambda i, c: (i, c, OFF_RQ // HW)),
                pl.BlockSpec((None, br, HW), lambda i, c: (i, c, OFF_RK // HW)),
                pl.BlockSpec((None, br, RV_W), lambda i, c: (i, c, OFF_RV // RV_W)),
                pl.BlockSpec((None, br, RV_W), lambda i, c: (i, c, OFF_RG // RV_W)),
                pl.BlockSpec((lp * cpb, HEAD_DIM), lambda i, c: (c, 0)),
                pl.BlockSpec((lp * cpb, HEAD_DIM), lambda i, c: (c, 0)),
                pl.BlockSpec((N_HEADS, lp, lp), lambda i, c: (0, 0, 0)),
                pl.BlockSpec((N_HEADS, lp, HEAD_DIM), lambda i, c: (0, 0, 0)),
                pl.BlockSpec((N_HEADS, lp, HEAD_DIM), lambda i, c: (0, 0, 0))]
    args = [proj3, proj3, proj3, proj3, cos, sin, dmask, qdec, kdec]
    scratch = [pltpu.VMEM((N_HEADS, RET_QK, RET_V), F32)]
    if has_s0:
        in_specs.append(pl.BlockSpec((None, None, N_HEADS, RET_QK, RET_V), lambda i, c: (l, i, 0, 0, 0)))
        args.append(s0)
        scratch += [pltpu.VMEM((lp, HW), F32), pltpu.VMEM((lp, HW), F32),
                    pltpu.VMEM((lp, RV_W), F32), pltpu.VMEM((lp, RV_W), F32)]
    return pl.pallas_call(
        functools.partial(_ret_kernel, rows=rows, lp=lp, has_s0=has_s0, cdec=cdec, n_steps=n_steps, cpb=cpb),
        grid=(b, n_steps),
        in_specs=in_specs,
        out_specs=[pl.BlockSpec((None, br, RV_W), lambda i, c: (i, c, 0)),
                   pl.BlockSpec((None, N_HEADS, RET_QK, RET_V), lambda i, c: (i, 0, 0, 0))],
        out_shape=[jax.ShapeDtypeStruct((b, t, RV_W), BF16),
                   jax.ShapeDtypeStruct((b, N_HEADS, RET_QK, RET_V), F32)],
        scratch_shapes=scratch,
        compiler_params=_params("parallel", "arbitrary"),
        name="retention",
    )(*args)


ATTN_ROWS = 2048
ROPE_CHUNK = 256


def _rows_at(start, size, stride):
    return pl.ds(start, size, stride=stride) if stride > 1 else pl.ds(start, size)


def _softmax_pv(q, keys, vals, valid):
    s = jnp.where(valid, _dot_nt(q, keys), NEG)
    m = jnp.max(s, axis=-1, keepdims=True)
    p = jnp.exp(s - m)
    den = jnp.sum(p, axis=-1, keepdims=True)
    return _dot((p * (1.0 / den)).astype(BF16), vals), m + jnp.log(den)


def _attn_p_kernel(*refs):
    ng = len(DIL_GROUPS)
    qkv = refs[0:3 * ng]
    yc_ref = refs[3 * ng]
    scr = refs[3 * ng + 1:]
    o_sc, lse_sc, kps, vps = [scr[n * ng:(n + 1) * ng] for n in range(4)]
    qrot, krs, vs = qkv[0::3], qkv[1::3], qkv[2::3]
    i = pl.program_id(2)
    rows, blk = ATTN_ROWS, ATTN_BLOCK

    @pl.when(i == 0)
    def _():
        for g in range(ng):
            kps[g][...] = jnp.zeros_like(kps[g])
            vps[g][...] = jnp.zeros_like(vps[g])

    qi = lax.broadcasted_iota(jnp.int32, (blk, 2 * blk), 0)
    ki = lax.broadcasted_iota(jnp.int32, (blk, 2 * blk), 1)
    back = blk + qi - ki
    band = (back >= 0) & (back <= blk)
    band_first = band & (ki >= jnp.where(i > 0, 0, blk))

    for g, (window, dil) in enumerate(DIL_GROUPS):
        assert window // dil == blk
        n_sub = rows // (dil * blk)

        for r in range(dil):
            for jb in range(n_sub):
                at = lambda j, n: _rows_at(r + j * blk * dil, n, dil)
                q = qrot[g][at(jb, blk), :].astype(BF16)
                if jb == 0:
                    keys = jnp.concatenate([kps[g][at(0, blk), :], krs[g][at(0, blk), :]], axis=0)
                    vals = jnp.concatenate([vps[g][at(0, blk), :], vs[g][at(0, blk), :]], axis=0)
                    valid = band_first
                else:
                    keys = krs[g][at(jb - 1, 2 * blk), :]
                    vals = vs[g][at(jb - 1, 2 * blk), :]
                    valid = band
                o, lse = _softmax_pv(q, keys.astype(BF16), vals.astype(BF16), valid)
                o_sc[g][at(jb, blk), :] = o
                lse_sc[g][at(jb, blk), :] = jnp.broadcast_to(lse, (blk, HEAD_DIM))

    for g, (window, dil) in enumerate(DIL_GROUPS):
        keep = dil * blk
        kps[g][...] = krs[g][rows - keep:, :]
        vps[g][...] = vs[g][rows - keep:, :]

    for c in range(0, rows, ROPE_CHUNK):
        rs = slice(c, c + ROPE_CHUNK)
        ls = [lse_sc[g][rs, :] for g in range(ng)]
        m = functools.reduce(jnp.maximum, ls)
        es = [jnp.exp(a - m) for a in ls]
        den = functools.reduce(lambda a, e: a + e, es)
        y = functools.reduce(lambda a, e: a + e, [(es[g] / den) * o_sc[g][rs, :] for g in range(ng)])
        yc_ref[rs, :] = y.astype(BF16)


def _attn_prompt(qkv_h, b, t):
    ng = len(DIL_GROUPS)
    m = b * t
    rows = ATTN_ROWS
    nblk = t // rows
    assert nblk * rows == t

    def head(n):
        return pl.BlockSpec((None, rows, HEAD_DIM), lambda i, h, j: (n * N_HEADS + h, i * nblk + j, 0))

    keeps = [dil * ATTN_BLOCK for _, dil in DIL_GROUPS]
    scratch = ([pltpu.VMEM((rows, HEAD_DIM), F32)] * (2 * ng)
               + [pltpu.VMEM((k, HEAD_DIM), F32) for k in keeps] * 2)
    return pl.pallas_call(
        _attn_p_kernel,
        grid=(b, N_HEADS, nblk),
        in_specs=[head(n) for n in range(3 * ng)],
        out_specs=pl.BlockSpec((rows, HEAD_DIM), lambda i, h, j: (i * nblk + j, h)),
        out_shape=jax.ShapeDtypeStruct((m, HW), BF16),
        scratch_shapes=scratch,
        compiler_params=_params("parallel", "parallel", "arbitrary"),
        name="attn_prompt",
    )(*([qkv_h] * (3 * ng)))


def _attn_s_kernel(*refs, t, lp, cache_rows):
    ng = len(DIL_GROUPS)
    qkv = refs[0:3 * ng]
    caches = refs[3 * ng:4 * ng]
    yc_ref = refs[4 * ng]
    knew = refs[4 * ng + 1:5 * ng + 1]
    qp, kp, vp = refs[5 * ng + 1:5 * ng + 4]
    qi = lax.broadcasted_iota(jnp.int32, (lp, lp), 0)
    kn_i = lax.broadcasted_iota(jnp.int32, (lp, lp), 1)
    outs = [[None] * ng for _ in range(N_HEADS)]
    lses = [[None] * ng for _ in range(N_HEADS)]
    for g, (window, dil) in enumerate(DIL_GROUPS):
        lb = cache_rows[g]
        for pad, src in ((qp, qkv[3 * g]), (kp, qkv[3 * g + 1]), (vp, qkv[3 * g + 2])):
            pad[...] = jnp.zeros_like(pad)
            pad[0:t, :] = src[...]
        cache = caches[g]
        qc = lax.broadcasted_iota(jnp.int32, (lp, lb), 0)
        kc_i = lax.broadcasted_iota(jnp.int32, (lp, lb), 1)
        back_c = lb + qc - kc_i
        valid_c = ((back_c & (dil - 1)) == 0) & (back_c <= window)
        back_n = qi - kn_i
        valid_n = (back_n >= 0) & ((back_n & (dil - 1)) == 0) & (back_n <= window) & (kn_i < t)
        for h in range(N_HEADS):
            hs = slice(h * HEAD_DIM, (h + 1) * HEAD_DIM)
            q = qp[:, hs].astype(BF16)
            k_new = kp[:, hs]
            v_new = vp[:, hs]
            for tt in range(t):
                knew[g][tt * KV_ROWS + h:tt * KV_ROWS + h + 1, :] = k_new[tt:tt + 1, :]
                knew[g][tt * KV_ROWS + N_HEADS + h:tt * KV_ROWS + N_HEADS + h + 1, :] = v_new[tt:tt + 1, :]
            k_old = cache[pl.ds(h, lb, stride=KV_ROWS), :].astype(BF16)
            v_old = cache[pl.ds(N_HEADS + h, lb, stride=KV_ROWS), :].astype(BF16)
            s_c = jnp.where(valid_c, _dot_nt(q, k_old), NEG)
            s_n = jnp.where(valid_n, _dot_nt(q, k_new.astype(BF16)), NEG)
            m = jnp.maximum(jnp.max(s_c, axis=-1, keepdims=True), jnp.max(s_n, axis=-1, keepdims=True))
            p_c = jnp.exp(s_c - m)
            p_n = jnp.exp(s_n - m)
            den = jnp.sum(p_c, axis=-1, keepdims=True) + jnp.sum(p_n, axis=-1, keepdims=True)
            o = (_dot((p_c / den).astype(BF16), v_old)
                 + _dot((p_n / den).astype(BF16), v_new.astype(BF16)))
            outs[h][g] = o
            lses[h][g] = m + jnp.log(den)
    for h in range(N_HEADS):
        hs = slice(h * HEAD_DIM, (h + 1) * HEAD_DIM)
        m = functools.reduce(jnp.maximum, lses[h])
        es = [jnp.exp(a - m) for a in lses[h]]
        den = functools.reduce(lambda a, c: a + c, es)
        y = functools.reduce(lambda a, c: a + c, [(e / den) * o for e, o in zip(es, outs[h])])
        yc_ref[:, hs] = y[0:t, :].astype(BF16)


def _attn_sample(proj3, caches, l, lp):
    b, t, _ = proj3.shape
    ng = len(DIL_GROUPS)
    cache_rows = tuple(c.shape[2] // KV_ROWS for c in caches)
    tok = lambda c0: pl.BlockSpec((None, t, HW), lambda i: (i, 0, c0))
    in_specs = [tok(n) for n in range(3 * ng)]
    in_specs += [pl.BlockSpec((None, None, lb * KV_ROWS, HEAD_DIM), lambda i: (l, i, 0, 0)) for lb in cache_rows]
    res = pl.pallas_call(
        functools.partial(_attn_s_kernel, t=t, lp=lp, cache_rows=cache_rows),
        grid=(b,),
        in_specs=in_specs,
        out_specs=[pl.BlockSpec((None, t, HW), lambda i: (i, 0, 0))]
                  + [pl.BlockSpec((None, t * KV_ROWS, HEAD_DIM), lambda i: (i, 0, 0))] * ng,
        out_shape=[jax.ShapeDtypeStruct((b, t, HW), BF16)]
                  + [jax.ShapeDtypeStruct((b, t * KV_ROWS, HEAD_DIM), F32)] * ng,
        scratch_shapes=[pltpu.VMEM((lp, HW), F32)] * 3,
        compiler_params=_params("parallel"),
        name="attn_sample",
    )(*([proj3] * (3 * ng)), *caches)
    return res[0], res[1:]


def _cache_shift_kernel(cur_ref, nxt_ref, new_ref, o_ref, *, ch, shift, n_chunks):
    c = pl.program_id(2)
    o_ref[0:ch - shift, :] = cur_ref[shift:ch, :]
    o_ref[ch - shift:ch, :] = jnp.where(c == n_chunks - 1, new_ref[...], nxt_ref[...])


def _cache_shift(cache, new):
    depth, b, rows, d = cache.shape
    shift = new.shape[2]
    ch = min(rows, 8192)
    n_chunks = rows // ch
    assert n_chunks * ch == rows and ch % shift == 0 and shift % 8 == 0
    per = ch // shift
    last = rows // shift - 1
    return pl.pallas_call(
        functools.partial(_cache_shift_kernel, ch=ch, shift=shift, n_chunks=n_chunks),
        grid=(depth, b, n_chunks),
        in_specs=[pl.BlockSpec((None, None, ch, d), lambda l, i, c: (l, i, c, 0)),
                  pl.BlockSpec((None, None, shift, d), lambda l, i, c: (l, i, jnp.minimum((c + 1) * per, last), 0)),
                  pl.BlockSpec((None, None, shift, d), lambda l, i, c: (l, i, 0, 0))],
        out_specs=pl.BlockSpec((None, None, ch, d), lambda l, i, c: (l, i, c, 0)),
        out_shape=jax.ShapeDtypeStruct(cache.shape, cache.dtype),
        compiler_params=_params("parallel", "parallel", "arbitrary"),
        name="cache_shift",
    )(cache, cache, new)


def _win_pack_kernel(k_ref, v_ref, o_ref, *, rows):
    for h in range(N_HEADS):
        o_ref[pl.ds(h, rows, stride=KV_ROWS), :] = k_ref[h]
        o_ref[pl.ds(N_HEADS + h, rows, stride=KV_ROWS), :] = v_ref[h]


def _win_pack(qkv_h, g, b, t, keep):
    ch = min(keep, 512)
    first = (t - keep) // ch
    per_b = t // ch
    k_blk, v_blk = 3 * g + 1, 3 * g + 2
    return pl.pallas_call(
        functools.partial(_win_pack_kernel, rows=ch),
        grid=(b, keep // ch),
        in_specs=[pl.BlockSpec((N_HEADS, ch, HEAD_DIM), lambda i, c: (k_blk, i * per_b + first + c, 0)),
                  pl.BlockSpec((N_HEADS, ch, HEAD_DIM), lambda i, c: (v_blk, i * per_b + first + c, 0))],
        out_specs=pl.BlockSpec((None, ch * KV_ROWS, HEAD_DIM), lambda i, c: (i, c, 0)),
        out_shape=jax.ShapeDtypeStruct((b, keep * KV_ROWS, HEAD_DIM), F32),
        compiler_params=_params("parallel", "parallel"),
        name="win_pack",
    )(qkv_h, qkv_h)


def _merge_kernel(h_ref, ya, yb, yc, wga, wgb, wgc, wa, wb, wc, o_ref, wg_ref):
    tn = o_ref.shape[1]
    for n, src in enumerate((wga, wgb, wgc)):
        wg_ref[:, n * tn:(n + 1) * tn] = src[...].astype(BF16)
    gates = jax.nn.sigmoid(_dot(h_ref[...], wg_ref[...]))
    acc = gates[:, 0:tn] * _dot(ya[...], wa[...])
    acc = acc + gates[:, tn:2 * tn] * _dot(yb[...], wb[...])
    acc = acc + gates[:, 2 * tn:3 * tn] * _dot(yc[...], wc[...])
    o_ref[...] = acc.astype(BF16)


def _merge(h, ya, yb, yc, w_in, wa, wb, wc, l, tm, tn):
    m = h.shape[0]
    act = lambda w: pl.BlockSpec((tm, w), lambda i, j: (i, 0))
    wgt = lambda k: pl.BlockSpec((None, k, tn), lambda i, j: (l, 0, j))
    wgate = lambda n: pl.BlockSpec((None, D_MODEL, tn), lambda i, j: (l, 0, (OFF_GATE + n * D_MODEL) // tn + j))
    return pl.pallas_call(
        _merge_kernel,
        grid=(m // tm, D_MODEL // tn),
        in_specs=[act(D_MODEL), act(CONV_DIM), act(RV_W), act(HW), wgate(0), wgate(1), wgate(2),
                  wgt(CONV_DIM), wgt(RV_W), wgt(HW)],
        out_specs=pl.BlockSpec((tm, tn), lambda i, j: (i, j)),
        out_shape=jax.ShapeDtypeStruct((m, D_MODEL), BF16),
        scratch_shapes=[pltpu.VMEM((D_MODEL, 3 * tn), BF16)],
        compiler_params=_params("parallel", "parallel"),
        name="merge",
    )(h, ya, yb, yc, w_in, w_in, w_in, wa, wb, wc)


def _post_norm_residual(x, y, gpost, gate):
    r = y * lax.rsqrt(jnp.mean(y * y, axis=-1, keepdims=True) + EPS) * gpost
    return x + gate * r


def _mod_rows(ref, rs):
    v = ref[0]
    return v if v.shape[0] == 1 else v[rs, :]


def _outproj_kernel(m_ref, w_ref, x_ref, gpost_ref, g1_ref, gffn_ref, sc2_ref, sh2_ref, xo_ref, h2_ref, *, sub):
    w = w_ref[...]
    tm = m_ref.shape[0]
    for r in range(0, tm, sub):
        rs = slice(r, r + sub)
        y = _dot(m_ref[rs, :], w)
        xn = _post_norm_residual(x_ref[rs, :], y, gpost_ref[0], _mod_rows(g1_ref, rs))
        xo_ref[rs, :] = xn
        h2_ref[rs, :] = _norm_mod(xn, gffn_ref[0], _mod_rows(sc2_ref, rs), _mod_rows(sh2_ref, rs)).astype(BF16)


def _outproj(merged, w_out, l, x, gpost, g1, gffn, sc2, sh2, rows):
    m, d = x.shape
    tm = min(rows.tm, 512)
    sub = _Rows(m, tm, None if rows.per_row else rows.tiles_per_b * rows.tm)
    row = pl.BlockSpec((tm, d), lambda i: (i, 0))
    vec = pl.BlockSpec((1, 1, d), lambda i: (l, 0, 0))
    return pl.pallas_call(
        functools.partial(_outproj_kernel, sub=min(tm, 128)),
        grid=(m // tm,),
        in_specs=[row, pl.BlockSpec((None, d, d), lambda i: (l, 0, 0), pipeline_mode=pl.Buffered(1)),
                  row, vec, sub.mod_spec(), vec, sub.mod_spec(), sub.mod_spec()],
        out_specs=[row, row],
        out_shape=[jax.ShapeDtypeStruct((m, d), F32), jax.ShapeDtypeStruct((m, d), BF16)],
        compiler_params=_params("parallel"),
        name="outproj",
    )(merged, w_out, x, gpost, g1, gffn, sc2, sh2)


def _down_kernel(*refs, nk, with_next):
    n_in = 13 if with_next else 8
    n_out = 4 if with_next else 2
    ins, outs, (acc_ref, accs_ref) = refs[:n_in], refs[n_in:n_in + n_out], refs[n_in + n_out:]
    a_ref, as_ref, w_ref, x_ref, xs_ref, gpost_ref, g2_ref, g2s_ref = ins[:8]
    i = pl.program_id(0)
    k = pl.program_id(1)

    def finish(x_r, acc_r, g2_r, pos):
        xn = _post_norm_residual(x_r[...], acc_r[...], gpost_ref[0], g2_r[0])
        outs[pos][...] = xn
        if with_next:
            gn_ref, scn_ref, shn_ref = ins[8], ins[9 + pos], ins[11 + pos]
            outs[2 + pos][...] = _norm_mod(xn, gn_ref[0], scn_ref[0], shn_ref[0]).astype(BF16)

    @pl.when(k == 0)
    def _():
        acc_ref[...] = jnp.zeros_like(acc_ref)

    acc_ref[...] += _dot(a_ref[...], w_ref[...])

    @pl.when(k == nk - 1)
    def _():
        finish(x_ref, acc_ref, g2_ref, 0)

    @pl.when(i == 0)
    def _():
        @pl.when(k == 0)
        def _():
            accs_ref[...] = jnp.zeros_like(accs_ref)

        accs_ref[...] += _dot(as_ref[...], w_ref[...])

        @pl.when(k == nk - 1)
        def _():
            finish(xs_ref, accs_ref, g2s_ref, 1)


def _down(a, a_s, w_down, l, x, x_s, gpost, g2, g2_s, nxt, rows):
    m, d = x.shape
    ms = x_s.shape[0]
    kdim = a.shape[1]
    tm = min(rows.tm, 1024)
    tk = 1024
    nk = kdim // tk
    sub = _Rows(m, tm, None if rows.per_row else rows.tiles_per_b * rows.tm)
    one = lambda shape, imap: pl.BlockSpec(shape, imap, pipeline_mode=pl.Buffered(1))
    row = one((tm, d), lambda i, k: (i, 0))
    row_s = pl.BlockSpec((ms, d), lambda i, k: (0, 0))
    mod_s = pl.BlockSpec((1, ms, d), lambda i, k: (0, 0, 0))
    vec = pl.BlockSpec((1, 1, d), lambda i, k: (l, 0, 0))
    in_specs = [pl.BlockSpec((tm, tk), lambda i, k: (i, k)),
                pl.BlockSpec((ms, tk), lambda i, k: (0, k)),
                pl.BlockSpec((None, tk, d), lambda i, k: (l, k, 0)),
                row, row_s, vec, sub.mod_spec(), mod_s]
    args = [a, a_s, w_down, x, x_s, gpost, g2, g2_s]
    out_specs = [row, row_s]
    out_shape = [jax.ShapeDtypeStruct((m, d), F32), jax.ShapeDtypeStruct((ms, d), F32)]
    if nxt is not None:
        gn, scn, shn, scn_s, shn_s = nxt
        in_specs += [pl.BlockSpec((1, 1, d), lambda i, k: (l + 1, 0, 0)), sub.mod_spec(), mod_s, sub.mod_spec(), mod_s]
        args += [gn, scn, scn_s, shn, shn_s]
        out_specs += [row, row_s]
        out_shape += [jax.ShapeDtypeStruct((m, d), BF16), jax.ShapeDtypeStruct((ms, d), BF16)]
    return pl.pallas_call(
        functools.partial(_down_kernel, nk=nk, with_next=nxt is not None),
        grid=(m // tm, nk),
        in_specs=in_specs,
        out_specs=out_specs,
        out_shape=out_shape,
        scratch_shapes=[pltpu.VMEM((tm, d), F32), pltpu.VMEM((ms, d), F32)],
        compiler_params=_params("arbitrary", "arbitrary", vmem=VMEM_LIMIT_MAX),
        name="mlp_down",
    )(*args)


def kernel(x_prompt, x_sample, c_prompt, c_sample, state_conv, state_retention, cache_win128, cache_win512, cache_win2048, norm_mix_g, norm_mix_post_g, norm_ffn_g, norm_ffn_post_g, w_mod, b_mod, w_in, conv_w, conv_b, conv_ln_g, conv_ln_b, w_branch_a, w_branch_b, w_branch_c, w_out, w_up, w_down):
    depth = w_in.shape[0]
    bp, tp, d = x_prompt.shape
    bs, ts, _ = x_sample.shape
    mp, ms = bp * tp, bs * ts
    ng = len(DIL_GROUPS)

    w_a_b, w_b_b, w_c_b, w_out_b, w_down_b = [
        w.astype(BF16) for w in (w_branch_a, w_branch_b, w_branch_c, w_out, w_down)]
    conv_w8 = jnp.broadcast_to(conv_w[:, :, None, :], (depth, CONV_WIDTH, SUBLANES, CONV_DIM))
    vec3 = lambda a: a.reshape(depth, 1, a.shape[-1])
    g_mix, g_mix_post, g_ffn, g_ffn_post = map(vec3, (norm_mix_g, norm_mix_post_g, norm_ffn_g, norm_ffn_post_g))
    conv_b3, ln_g3, ln_b3 = map(vec3, (conv_b, conv_ln_g, conv_ln_b))
    caches = [c.reshape(c.shape[0], c.shape[1], c.shape[2] * KV_ROWS, HEAD_DIM)
              for c in (cache_win128, cache_win512, cache_win2048)]

    lp_s = 16
    qkv_rope = _qkv_rope_tables(np.arange(tp)) + _qkv_rope_tables(np.tile(PAST_LEN + np.arange(ts), bs))
    retn_p = _rope_tables(np.arange(tp), _retnet_inv_freq(), tp)
    retn_s = _rope_tables(PAST_LEN + np.arange(ts), _retnet_inv_freq(), lp_s)

    n_c = bp + bs
    c_rows = -(-n_c // 8) * 8
    c_all = jnp.concatenate([c_prompt, c_sample, jnp.zeros((c_rows - n_c, d), F32)], axis=0)
    mod = _modulation(c_all, w_mod, b_mod)

    rows_p = _Rows(mp, 1024, tp)
    rows_s = _Rows(ms, ms, None)

    def mods(l):
        parts_p = [mod[l, :bp, i * d:(i + 1) * d].reshape(bp, 1, d) for i in range(6)]
        parts_s = [jnp.repeat(mod[l, bp:n_c, i * d:(i + 1) * d], ts, axis=0).reshape(1, ms, d) for i in range(6)]
        return parts_p, parts_s

    xp = x_prompt.reshape(mp, d)
    xs = x_sample.reshape(ms, d)
    mod_p, mod_s = mods(0)
    hp = _prenorm(xp, g_mix, 0, mod_p[1], mod_p[0], rows_p)
    hs = _prenorm(xs, g_mix, 0, mod_s[1], mod_s[0], rows_s)

    conv_p, conv_s, ret_p, ret_s = [], [], [], []
    win_p = [[] for _ in range(ng)]
    new_kv_s = [[] for _ in range(ng)]

    for l in range(depth):
        sh1p, sc1p, g1p, sh2p, sc2p, g2p = mod_p
        sh1s, sc1s, g1s, sh2s, sc2s, g2s = mod_s

        proj_ab, proj_ab_s = _matmul(hp, hs, w_in, l, F32, 2048, 512, ncols=OFF_DIL, name="matmul_in_ab")
        qkv_h, qkv_s = _matmul(hp, hs, w_in, l, F32, 2048, HW, col0=OFF_DIL, ncols=OFF_GATE - OFF_DIL,
                               heads=True, rope=qkv_rope, name="matmul_in_qkv")

        proj3 = proj_ab.reshape(bp, tp, OFF_DIL)
        ya, st = _conv_prompt(proj3, l, conv_w8, conv_b3, ln_g3, ln_b3)
        conv_p.append(st[:, CONV_HALO - (CONV_WIDTH - 1):])
        yb, s_new = _retention(proj3, retn_p[0], retn_p[1], None, l)
        ret_p.append(s_new)
        yc = _attn_prompt(qkv_h, bp, tp)
        for g, (window, dil) in enumerate(DIL_GROUPS):
            keep = min(window, tp)
            win_p[g].append(_win_pack(qkv_h, g, bp, tp, keep))
        merged = _merge(hp, ya.reshape(mp, CONV_DIM), yb.reshape(mp, RV_W), yc, w_in, w_a_b, w_b_b, w_c_b, l, 1024, 256)
        xp, h2 = _outproj(merged, w_out_b, l, xp, g_mix_post, g1p, g_ffn, sc2p, sh2p, rows_p)

        proj_s3 = proj_ab_s.reshape(bs, ts, OFF_DIL)
        ya_s, st_s = _conv_sample(proj_s3, l, state_conv, conv_w, conv_b3, ln_g3, ln_b3)
        conv_s.append(st_s)
        yb_s, s_new_s = _retention(proj_s3, retn_s[0], retn_s[1], state_retention, l)
        ret_s.append(s_new_s)
        yc_s, kv_new = _attn_sample(qkv_s.reshape(bs, ts, OFF_GATE - OFF_DIL), caches, l, lp_s)
        for g in range(ng):
            new_kv_s[g].append(kv_new[g])
        merged_s = _merge(hs, ya_s.reshape(ms, CONV_DIM), yb_s.reshape(ms, RV_W), yc_s.reshape(ms, HW),
                          w_in, w_a_b, w_b_b, w_c_b, l, ms, 256)
        xs, h2_s = _outproj(merged_s, w_out_b, l, xs, g_mix_post, g1s, g_ffn, sc2s, sh2s, rows_s)

        a, a_s = _matmul(h2, h2_s, w_up, l, BF16, 2048, 1024, act="relu2", name="mlp_up")
        if l + 1 < depth:
            mod_p_n, mod_s_n = mods(l + 1)
            nxt = (g_mix, mod_p_n[1], mod_p_n[0], mod_s_n[1], mod_s_n[0])
            xp, xs, hp, hs = _down(a, a_s, w_down_b, l, xp, xs, g_ffn_post, g2p, g2s, nxt, rows_p)
            mod_p, mod_s = mod_p_n, mod_s_n
        else:
            xp, xs = _down(a, a_s, w_down_b, l, xp, xs, g_ffn_post, g2p, g2s, None, rows_p)

    win_s = [_cache_shift(caches[g], jnp.stack(new_kv_s[g])) for g in range(ng)]
    kv5 = lambda a: a.reshape(a.shape[0], a.shape[1], a.shape[2] // KV_ROWS, 2, N_HEADS, HEAD_DIM)
    win_p = [kv5(jnp.stack(w)) for w in win_p]
    win_s = [kv5(w) for w in win_s]

    return (xp.reshape(bp, tp, d), xs.reshape(bs, ts, d),
            jnp.stack(conv_p), jnp.stack(conv_s), jnp.stack(ret_p), jnp.stack(ret_s),
            win_p[0], win_s[0], win_p[1], win_s[1], win_p[2], win_s[2])
```

```python
import functools
import math

import numpy as np
import jax
import jax.numpy as jnp
from jax import lax
from jax.experimental import pallas as pl
from jax.experimental.pallas import tpu as pltpu

F32 = jnp.float32
BF16 = jnp.bfloat16

D_MODEL = 2048
PAST_LEN = 16384
HEAD_DIM = 128
CONV_DIM = D_MODEL // 4
CONV_WIDTH = 31
CONV_HALO = 32
N_HEADS = D_MODEL // 512
RET_QK = 128
RET_V = 256
RET_CHUNK = 128
DIL_GROUPS = ((128, 1), (512, 4), (2048, 16))
ATTN_BLOCK = 128
D_FF = 4 * D_MODEL
ROPE_THETA = 10000.0
EPS = 1e-6
HW = N_HEADS * HEAD_DIM
RV_W = N_HEADS * RET_V
KV_ROWS = 2 * N_HEADS
SUBLANES = 8

OFF_CONV_A = 0
OFF_CONV_B = CONV_DIM
OFF_RQ = 2 * CONV_DIM
OFF_RK = OFF_RQ + HW
OFF_RV = OFF_RK + HW
OFF_RG = OFF_RV + RV_W
OFF_DIL = OFF_RG + RV_W
OFF_GATE = OFF_DIL + 3 * len(DIL_GROUPS) * HW
N_IN = OFF_GATE + 3 * D_MODEL

NEG = -1e30
VMEM_LIMIT = 52 * 1024 * 1024
VMEM_LIMIT_MAX = 58 * 1024 * 1024


def _params(*sem, vmem=VMEM_LIMIT):
    return pltpu.CompilerParams(dimension_semantics=sem, vmem_limit_bytes=vmem)


def _silu(x):
    return x * jax.nn.sigmoid(x)


def _rot(x, cos2, sin2):
    return x * cos2 + pltpu.roll(x, HEAD_DIM // 2, axis=1) * sin2


def _dot(a, b):
    return jnp.dot(a, b, preferred_element_type=F32)


def _dot_nt(a, b):
    return lax.dot_general(a, b, (((1,), (1,)), ((), ())), preferred_element_type=F32)


def _dot_tn(a, b):
    return lax.dot_general(a, b, (((0,), (0,)), ((), ())), preferred_element_type=F32)


def _rope_tables(pos, inv_freq, rows):
    ang = np.asarray(pos, np.float64)[:, None] * inv_freq[None, :]
    cos = np.concatenate([np.cos(ang), np.cos(ang)], axis=-1)
    sin = np.concatenate([-np.sin(ang), np.sin(ang)], axis=-1)
    pad = rows - cos.shape[0]
    if pad:
        cos = np.pad(cos, ((0, pad), (0, 0)))
        sin = np.pad(sin, ((0, pad), (0, 0)))
    return jnp.asarray(cos, F32), jnp.asarray(sin, F32)


def _qkv_rope_tables(pos):
    cos, sin = _rope_tables(pos, _rope_inv_freq(), len(pos))
    scale = HEAD_DIM ** -0.5
    return (jnp.stack([cos * scale, cos, jnp.ones_like(cos)]),
            jnp.stack([sin * scale, sin, jnp.zeros_like(sin)]))


def _rope_inv_freq():
    return 1.0 / (ROPE_THETA ** (np.arange(0, HEAD_DIM, 2, dtype=np.float64) / HEAD_DIM))


def _retnet_inv_freq():
    return 1.0 / (ROPE_THETA ** np.linspace(0.0, 1.0, RET_QK // 2, dtype=np.float64))


def _ret_decay_tables(rows, lp):
    lg = np.log(1.0 - 2.0 ** (-5.0 - np.arange(N_HEADS, dtype=np.float64)))
    i = np.arange(lp, dtype=np.float64)
    rel = i[:, None] - i[None, :]
    real = (i < rows)
    dmask = np.where((rel >= 0) & real[:, None] & real[None, :],
                     np.exp(np.maximum(rel, 0.0)[None] * lg[:, None, None]), 0.0)
    qdec = np.where(real[None, :], np.exp((i[None, :] + 1.0) * lg[:, None]), 0.0)
    kdec = np.where(real[None, :], np.exp((rows - 1.0 - i[None, :]) * lg[:, None]), 0.0)
    cdec = tuple(float(v) for v in np.exp(rows * lg))
    bl = lambda a: jnp.asarray(np.broadcast_to(a[:, :, None], (N_HEADS, lp, HEAD_DIM)), F32)
    return jnp.asarray(dmask, F32), bl(qdec), bl(kdec), cdec


class _Rows:
    def __init__(self, m, tm, rows_per_batch):
        self.m = m
        self.tm = tm
        self.per_row = rows_per_batch is None
        self.tiles_per_b = None if self.per_row else rows_per_batch // tm

    def mod_spec(self):
        if self.per_row:
            return pl.BlockSpec((1, self.tm, D_MODEL), lambda i, *_: (0, i, 0))
        tpb = self.tiles_per_b
        return pl.BlockSpec((1, 1, D_MODEL), lambda i, *_: (i // tpb, 0, 0))


def _mod_kernel(c_ref, w_ref, b_ref, o_ref):
    c = c_ref[...]
    s = _silu(c).astype(BF16)
    o_ref[0] = _dot(s, w_ref[0].astype(BF16)) + b_ref[0]


def _modulation(c_all, w_mod, b_mod):
    depth, d, n = w_mod.shape
    rows = c_all.shape[0]
    tn = 1024
    return pl.pallas_call(
        _mod_kernel,
        grid=(depth, n // tn),
        in_specs=[pl.BlockSpec((rows, d), lambda l, j: (0, 0)),
                  pl.BlockSpec((1, d, tn), lambda l, j: (l, 0, j)),
                  pl.BlockSpec((1, 1, tn), lambda l, j: (l, 0, j))],
        out_specs=pl.BlockSpec((1, rows, tn), lambda l, j: (l, 0, j)),
        out_shape=jax.ShapeDtypeStruct((depth, rows, n), F32),
        compiler_params=_params("parallel", "parallel"),
        name="modulation",
    )(c_all, w_mod, b_mod.reshape(depth, 1, n))


def _norm_mod(x, g, sc, sh):
    y = x * lax.rsqrt(jnp.mean(x * x, axis=-1, keepdims=True) + EPS) * g
    return y * (1.0 + sc) + sh


def _prenorm_kernel(x_ref, g_ref, sc_ref, sh_ref, h_ref):
    h_ref[...] = _norm_mod(x_ref[...], g_ref[0], sc_ref[0], sh_ref[0]).astype(BF16)


def _prenorm(x, g, l, sc, sh, rows):
    m, d = x.shape
    tm = rows.tm
    return pl.pallas_call(
        _prenorm_kernel,
        grid=(m // tm,),
        in_specs=[pl.BlockSpec((tm, d), lambda i: (i, 0)),
                  pl.BlockSpec((1, 1, d), lambda i: (l, 0, 0)),
                  rows.mod_spec(), rows.mod_spec()],
        out_specs=pl.BlockSpec((tm, d), lambda i: (i, 0)),
        out_shape=jax.ShapeDtypeStruct((m, d), BF16),
        compiler_params=_params("parallel"),
        name="prenorm",
    )(x, g, sc, sh)


def _mm_kernel(*refs, act, heads, rope):
    if rope:
        x_ref, xs_ref, w_ref, c_ref, s_ref, cs_ref, ss_ref, o_ref, os_ref = refs
    else:
        x_ref, xs_ref, w_ref, o_ref, os_ref = refs

    def product(rows_ref, cos_ref=None, sin_ref=None):
        y = _dot(rows_ref[...], w_ref[...].astype(BF16))
        if act == "relu2":
            y = jnp.square(jnp.maximum(y, 0.0))
        if rope:
            cos2, sin2 = cos_ref[...], sin_ref[...]
            y = jnp.concatenate([_rot(y[:, h * HEAD_DIM:(h + 1) * HEAD_DIM], cos2, sin2)
                                 for h in range(y.shape[1] // HEAD_DIM)], axis=1)
        return y

    y = product(x_ref, *((c_ref, s_ref) if rope else ()))
    if heads:
        for h in range(o_ref.shape[0]):
            o_ref[h] = y[:, h * HEAD_DIM:(h + 1) * HEAD_DIM]
    else:
        o_ref[...] = y.astype(o_ref.dtype)

    @pl.when(pl.program_id(0) == 0)
    def _():
        os_ref[0] = product(xs_ref, *((cs_ref, ss_ref) if rope else ())).astype(os_ref.dtype)

    @pl.when(pl.program_id(0) > 0)
    def _():
        os_ref[...] = jnp.zeros_like(os_ref)


def _matmul(x, xs, w, l, out_dtype, tm, tn, act=None, col0=0, ncols=None, heads=False, rope=None, name="matmul"):
    m, k = x.shape
    ms = xs.shape[0]
    n = w.shape[-1] if ncols is None else ncols
    c0 = col0 // tn
    assert c0 * tn == col0 and n % tn == 0 and tn % HEAD_DIM == 0
    if heads:
        out_spec = pl.BlockSpec((tn // HEAD_DIM, tm, HEAD_DIM), lambda i, j: (j, i, 0))
        out_shape = jax.ShapeDtypeStruct((n // HEAD_DIM, m, HEAD_DIM), out_dtype)
    else:
        out_spec = pl.BlockSpec((tm, tn), lambda i, j: (i, j))
        out_shape = jax.ShapeDtypeStruct((m, n), out_dtype)
    in_specs = [pl.BlockSpec((tm, k), lambda i, j: (i, 0)),
                pl.BlockSpec((ms, k), lambda i, j: (0, 0)),
                pl.BlockSpec((None, k, tn), lambda i, j: (l, 0, c0 + j))]
    args = [x, xs, w]
    if rope is not None:
        period, t = rope[0].shape[0], rope[0].shape[1]
        per_seq = t // tm
        assert per_seq * tm == t
        tab = pl.BlockSpec((None, tm, HEAD_DIM), lambda i, j: (j % period, i % per_seq, 0))
        tab_s = pl.BlockSpec((None, ms, HEAD_DIM), lambda i, j: (j % period, 0, 0))
        in_specs += [tab, tab, tab_s, tab_s]
        args += list(rope)
    res = pl.pallas_call(
        functools.partial(_mm_kernel, act=act, heads=heads, rope=rope is not None),
        grid=(m // tm, n // tn),
        in_specs=in_specs,
        out_specs=[out_spec, pl.BlockSpec((1, ms, tn), lambda i, j: (i, 0, j))],
        out_shape=[out_shape, jax.ShapeDtypeStruct((m // tm, ms, n), out_dtype)],
        compiler_params=_params("parallel", "parallel"),
        name=name,
    )(*args)
    return res[0], res[1][0]


def _conv_ln_silu(full_ref, first, rows, w_ref, cb, lg, lb):
    acc = jnp.zeros((rows, CONV_DIM), F32)
    for j in range(CONV_WIDTH):
        acc = acc + full_ref[first + j:first + j + rows, :] * w_ref[j:j + 1, :]
    y = acc + cb
    yc = y - jnp.mean(y, axis=-1, keepdims=True)
    z = yc * lax.rsqrt(jnp.mean(yc * yc, axis=-1, keepdims=True) + EPS) * lg + lb
    return _silu(z)


def _conv_p_kernel(a_ref, b_ref, ha_ref, hb_ref, w_ref, cb_ref, lg_ref, lb_ref, ya_ref, st_ref, sh_ref,
                   *, tt, rc):
    t = pl.program_id(1)
    u = a_ref[...] * jax.nn.sigmoid(b_ref[...])
    uh = ha_ref[...] * jax.nn.sigmoid(hb_ref[...])
    sh_ref[0, 0:CONV_HALO, :] = jnp.where(t > 0, uh, 0.0)
    sh_ref[0, CONV_HALO:, :] = u
    st_ref[...] = u[tt - CONV_HALO:, :]
    n_sh = tt + CONV_HALO - SUBLANES
    for s in range(1, SUBLANES):
        sh_ref[s, 0:n_sh, :] = sh_ref[0, s:s + n_sh, :]
    skip = CONV_HALO - (CONV_WIDTH - 1)
    cb, lg, lb = cb_ref[...], lg_ref[...], lb_ref[...]
    for r0 in range(0, tt, rc):
        acc = jnp.zeros((rc, CONV_DIM), F32)
        for j in range(CONV_WIDTH):
            first = r0 + skip + j
            s = first % SUBLANES
            wj = jnp.tile(w_ref[j], (rc // SUBLANES, 1))
            acc = acc + sh_ref[s, first - s:first - s + rc, :] * wj
        y = acc + cb
        yc = y - jnp.mean(y, axis=-1, keepdims=True)
        z = yc * lax.rsqrt(jnp.mean(yc * yc, axis=-1, keepdims=True) + EPS) * lg + lb
        ya_ref[r0:r0 + rc, :] = _silu(z).astype(BF16)


def _conv_prompt(proj3, l, conv_w8, conv_b, ln_g, ln_b):
    b, t, _ = proj3.shape
    tt, rc = 256, 32
    hpt = tt // CONV_HALO
    vec = pl.BlockSpec((None, 1, CONV_DIM), lambda i, j: (l, 0, 0))
    return pl.pallas_call(
        functools.partial(_conv_p_kernel, tt=tt, rc=rc),
        grid=(b, t // tt),
        in_specs=[pl.BlockSpec((None, tt, CONV_DIM), lambda i, j: (i, j, 0)),
                  pl.BlockSpec((None, tt, CONV_DIM), lambda i, j: (i, j, 1)),
                  pl.BlockSpec((None, CONV_HALO, CONV_DIM), lambda i, j: (i, jnp.maximum(j * hpt - 1, 0), 0)),
                  pl.BlockSpec((None, CONV_HALO, CONV_DIM), lambda i, j: (i, jnp.maximum(j * hpt - 1, 0), 1)),
                  pl.BlockSpec((None, CONV_WIDTH, SUBLANES, CONV_DIM), lambda i, j: (l, 0, 0, 0)),
                  vec, vec, vec],
        out_specs=[pl.BlockSpec((None, tt, CONV_DIM), lambda i, j: (i, j, 0)),
                   pl.BlockSpec((None, CONV_HALO, CONV_DIM), lambda i, j: (i, 0, 0))],
        out_shape=[jax.ShapeDtypeStruct((b, t, CONV_DIM), BF16),
                   jax.ShapeDtypeStruct((b, CONV_HALO, CONV_DIM), F32)],
        scratch_shapes=[pltpu.VMEM((SUBLANES, tt + CONV_HALO, CONV_DIM), F32)],
        compiler_params=_params("parallel", "arbitrary"),
        name="conv_prompt",
    )(proj3, proj3, proj3, proj3, conv_w8, conv_b, ln_g, ln_b)


def _conv_s_kernel(a_ref, b_ref, st_in_ref, w_ref, cb_ref, lg_ref, lb_ref, ya_ref, st_ref, full_ref, *, t):
    keep = CONV_WIDTH - 1
    u = a_ref[...] * jax.nn.sigmoid(b_ref[...])
    full_ref[0:keep, :] = st_in_ref[...]
    full_ref[keep:keep + t, :] = u
    st_ref[...] = full_ref[t:t + keep, :]
    y = _conv_ln_silu(full_ref, 0, t, w_ref, cb_ref[...], lg_ref[...], lb_ref[...])
    ya_ref[...] = y.astype(BF16)


def _conv_sample(proj3, l, state_conv, conv_w, conv_b, ln_g, ln_b):
    b, t, _ = proj3.shape
    keep = CONV_WIDTH - 1
    vec = pl.BlockSpec((None, 1, CONV_DIM), lambda i: (l, 0, 0))
    return pl.pallas_call(
        functools.partial(_conv_s_kernel, t=t),
        grid=(b,),
        in_specs=[pl.BlockSpec((None, t, CONV_DIM), lambda i: (i, 0, 0)),
                  pl.BlockSpec((None, t, CONV_DIM), lambda i: (i, 0, 1)),
                  pl.BlockSpec((None, None, keep, CONV_DIM), lambda i: (l, i, 0, 0)),
                  pl.BlockSpec((None, CONV_WIDTH, CONV_DIM), lambda i: (l, 0, 0)),
                  vec, vec, vec],
        out_specs=[pl.BlockSpec((None, t, CONV_DIM), lambda i: (i, 0, 0)),
                   pl.BlockSpec((None, keep, CONV_DIM), lambda i: (i, 0, 0))],
        out_shape=[jax.ShapeDtypeStruct((b, t, CONV_DIM), BF16),
                   jax.ShapeDtypeStruct((b, keep, CONV_DIM), F32)],
        scratch_shapes=[pltpu.VMEM((keep + 8 + t, CONV_DIM), F32)],
        compiler_params=_params("parallel"),
        name="conv_sample",
    )(proj3, proj3, state_conv, conv_w, conv_b, ln_g, ln_b)


def _ret_kernel(*refs, rows, lp, has_s0, cdec, n_steps, cpb):
    if has_s0:
        rq, rk, rv, rg, cos, sin, dm, qd, kd, s0, yb, sout, s_ref, qp, kp, vp, gp = refs
    else:
        rq, rk, rv, rg, cos, sin, dm, qd, kd, yb, sout, s_ref = refs
    c = pl.program_id(1)

    @pl.when(c == 0)
    def _():
        if has_s0:
            s_ref[...] = s0[...]
        else:
            s_ref[...] = jnp.zeros_like(s_ref)

    if rows < lp:
        for pad, src in ((qp, rq), (kp, rk), (vp, rv), (gp, rg)):
            pad[...] = jnp.zeros_like(pad)
            pad[0:rows, :] = src[...]
        rq, rk, rv, rg = qp, kp, vp, gp

    scale = RET_QK ** -0.5
    for ci in range(cpb):
        rs = slice(ci * lp, (ci + 1) * lp)
        cos2 = cos[rs, :]
        sin2 = sin[rs, :]
        for h in range(N_HEADS):
            qs = slice(h * RET_QK, (h + 1) * RET_QK)
            vs = slice(h * RET_V, (h + 1) * RET_V)
            q = _rot(rq[rs, qs], cos2, sin2) * scale
            k = _rot(rk[rs, qs], cos2, sin2)
            v = rv[rs, vs].astype(BF16)
            sc = _dot_nt(q.astype(BF16), k.astype(BF16)) * dm[h]
            s_h = s_ref[h]
            o = _dot(sc.astype(BF16), v) + _dot((q * qd[h]).astype(BF16), s_h.astype(BF16))
            s_ref[h] = s_h * cdec[h] + _dot_tn((k * kd[h]).astype(BF16), v)
            o = o * lax.rsqrt(jnp.mean(o * o, axis=-1, keepdims=True) + EPS)
            y = o * _silu(rg[rs, vs])
            yb[ci * rows:(ci + 1) * rows, vs] = y[0:rows, :].astype(BF16)

    @pl.when(c == n_steps - 1)
    def _():
        sout[...] = s_ref[...]


def _retention(proj3, cos, sin, s0, l):
    b, t, _ = proj3.shape
    rows = math.gcd(t, RET_CHUNK)
    lp = max(rows, 16)
    n_chunks = t // rows
    dmask, qdec, kdec, cdec = _ret_decay_tables(rows, lp)
    has_s0 = s0 is not None
    assert has_s0 == (rows < lp)
    cpb = 4 if (rows == lp and n_chunks % 4 == 0) else 1
    n_steps = n_chunks // cpb
    br = rows * cpb
    in_specs = [pl.BlockSpec((None, br, HW), lambda i, c: (i, c, OFF_RQ // HW)),
                pl.BlockSpec((None, br, HW), lambda i, c: (i, c, OFF_RK // HW)),
                pl.BlockSpec((None, br, RV_W), lambda i, c: (i, c, OFF_RV // RV_W)),
                pl.BlockSpec((None, br, RV_W), lambda i, c: (i, c, OFF_RG // RV_W)),
                pl.BlockSpec((lp * cpb, HEAD_DIM), lambda i, c: (c, 0)),
                pl.BlockSpec((lp * cpb, HEAD_DIM), lambda i, c: (c, 0)),
                pl.BlockSpec((N_HEADS, lp, lp), lambda i, c: (0, 0, 0)),
                pl.BlockSpec((N_HEADS, lp, HEAD_DIM), lambda i, c: (0, 0, 0)),
                pl.BlockSpec((N_HEADS, lp, HEAD_DIM), lambda i, c: (0, 0, 0))]
    args = [proj3, proj3, proj3, proj3, cos, sin, dmask, qdec, kdec]
    scratch = [pltpu.VMEM((N_HEADS, RET_QK, RET_V), F32)]
    if has_s0:
        in_specs.append(pl.BlockSpec((None, None, N_HEADS, RET_QK, RET_V), lambda i, c: (l, i, 0, 0, 0)))
        args.append(s0)
        scratch += [pltpu.VMEM((lp, HW), F32), pltpu.VMEM((lp, HW), F32),
                    pltpu.VMEM((lp, RV_W), F32), pltpu.VMEM((lp, RV_W), F32)]
    return pl.pallas_call(
        functools.partial(_ret_kernel, rows=rows, lp=lp, has_s0=has_s0, cdec=cdec, n_steps=n_steps, cpb=cpb),
        grid=(b, n_steps),
        in_specs=in_specs,
        out_specs=[pl.BlockSpec((None, br, RV_W), lambda i, c: (i, c, 0)),
                   pl.BlockSpec((None, N_HEADS, RET_QK, RET_V), lambda i, c: (i, 0, 0, 0))],
        out_shape=[jax.ShapeDtypeStruct((b, t, RV_W), BF16),
                   jax.ShapeDtypeStruct((b, N_HEADS, RET_QK, RET_V), F32)],
        scratch_shapes=scratch,
        compiler_params=_params("parallel", "arbitrary"),
        name="retention",
    )(*args)


ATTN_ROWS = 2048
ROPE_CHUNK = 256


def _rows_at(start, size, stride):
    return pl.ds(start, size, stride=stride) if stride > 1 else pl.ds(start, size)


def _softmax_pv(q, keys, vals, valid):
    s = jnp.where(valid, _dot_nt(q, keys), NEG)
    m = jnp.max(s, axis=-1, keepdims=True)
    p = jnp.exp(s - m)
    den = jnp.sum(p, axis=-1, keepdims=True)
    return _dot((p * (1.0 / den)).astype(BF16), vals), m + jnp.log(den)


def _attn_p_kernel(*refs):
    ng = len(DIL_GROUPS)
    qkv = refs[0:3 * ng]
    yc_ref = refs[3 * ng]
    scr = refs[3 * ng + 1:]
    o_sc, lse_sc, kps, vps = [scr[n * ng:(n + 1) * ng] for n in range(4)]
    qrot, krs, vs = qkv[0::3], qkv[1::3], qkv[2::3]
    i = pl.program_id(2)
    rows, blk = ATTN_ROWS, ATTN_BLOCK

    @pl.when(i == 0)
    def _():
        for g in range(ng):
            kps[g][...] = jnp.zeros_like(kps[g])
            vps[g][...] = jnp.zeros_like(vps[g])

    qi = lax.broadcasted_iota(jnp.int32, (blk, 2 * blk), 0)
    ki = lax.broadcasted_iota(jnp.int32, (blk, 2 * blk), 1)
    back = blk + qi - ki
    band = (back >= 0) & (back <= blk)
    band_first = band & (ki >= jnp.where(i > 0, 0, blk))

    for g, (window, dil) in enumerate(DIL_GROUPS):
        assert window // dil == blk
        n_sub = rows // (dil * blk)

        for r in range(dil):
            for jb in range(n_sub):
                at = lambda j, n: _rows_at(r + j * blk * dil, n, dil)
                q = qrot[g][at(jb, blk), :].astype(BF16)
                if jb == 0:
                    keys = jnp.concatenate([kps[g][at(0, blk), :], krs[g][at(0, blk), :]], axis=0)
                    vals = jnp.concatenate([vps[g][at(0, blk), :], vs[g][at(0, blk), :]], axis=0)
                    valid = band_first
                else:
                    keys = krs[g][at(jb - 1, 2 * blk), :]
                    vals = vs[g][at(jb - 1, 2 * blk), :]
                    valid = band
                o, lse = _softmax_pv(q, keys.astype(BF16), vals.astype(BF16), valid)
                o_sc[g][at(jb, blk), :] = o
                lse_sc[g][at(jb, blk), :] = jnp.broadcast_to(lse, (blk, HEAD_DIM))

    for g, (window, dil) in enumerate(DIL_GROUPS):
        keep = dil * blk
        kps[g][...] = krs[g][rows - keep:, :]
        vps[g][...] = vs[g][rows - keep:, :]

    for c in range(0, rows, ROPE_CHUNK):
        rs = slice(c, c + ROPE_CHUNK)
        ls = [lse_sc[g][rs, :] for g in range(ng)]
        m = functools.reduce(jnp.maximum, ls)
        es = [jnp.exp(a - m) for a in ls]
        den = functools.reduce(lambda a, e: a + e, es)
        y = functools.reduce(lambda a, e: a + e, [(es[g] / den) * o_sc[g][rs, :] for g in range(ng)])
        yc_ref[rs, :] = y.astype(BF16)


def _attn_prompt(qkv_h, b, t):
    ng = len(DIL_GROUPS)
    m = b * t
    rows = ATTN_ROWS
    nblk = t // rows
    assert nblk * rows == t

    def head(n):
        return pl.BlockSpec((None, rows, HEAD_DIM), lambda i, h, j: (n * N_HEADS + h, i * nblk + j, 0))

    keeps = [dil * ATTN_BLOCK for _, dil in DIL_GROUPS]
    scratch = ([pltpu.VMEM((rows, HEAD_DIM), F32)] * (2 * ng)
               + [pltpu.VMEM((k, HEAD_DIM), F32) for k in keeps] * 2)
    return pl.pallas_call(
        _attn_p_kernel,
        grid=(b, N_HEADS, nblk),
        in_specs=[head(n) for n in range(3 * ng)],
        out_specs=pl.BlockSpec((rows, HEAD_DIM), lambda i, h, j: (i * nblk + j, h)),
        out_shape=jax.ShapeDtypeStruct((m, HW), BF16),
        scratch_shapes=scratch,
        compiler_params=_params("parallel", "parallel", "arbitrary"),
        name="attn_prompt",
    )(*([qkv_h] * (3 * ng)))


def _attn_s_kernel(*refs, t, lp, cache_rows):
    ng = len(DIL_GROUPS)
    qkv = refs[0:3 * ng]
    caches = refs[3 * ng:4 * ng]
    yc_ref = refs[4 * ng]
    knew = refs[4 * ng + 1:5 * ng + 1]
    qp, kp, vp = refs[5 * ng + 1:5 * ng + 4]
    qi = lax.broadcasted_iota(jnp.int32, (lp, lp), 0)
    kn_i = lax.broadcasted_iota(jnp.int32, (lp, lp), 1)
    outs = [[None] * ng for _ in range(N_HEADS)]
    lses = [[None] * ng for _ in range(N_HEADS)]
    for g, (window, dil) in enumerate(DIL_GROUPS):
        lb = cache_rows[g]
        for pad, src in ((qp, qkv[3 * g]), (kp, qkv[3 * g + 1]), (vp, qkv[3 * g + 2])):
            pad[...] = jnp.zeros_like(pad)
            pad[0:t, :] = src[...]
        cache = caches[g]
        qc = lax.broadcasted_iota(jnp.int32, (lp, lb), 0)
        kc_i = lax.broadcasted_iota(jnp.int32, (lp, lb), 1)
        back_c = lb + qc - kc_i
        valid_c = ((back_c & (dil - 1)) == 0) & (back_c <= window)
        back_n = qi - kn_i
        valid_n = (back_n >= 0) & ((back_n & (dil - 1)) == 0) & (back_n <= window) & (kn_i < t)
        for h in range(N_HEADS):
            hs = slice(h * HEAD_DIM, (h + 1) * HEAD_DIM)
            q = qp[:, hs].astype(BF16)
            k_new = kp[:, hs]
            v_new = vp[:, hs]
            for tt in range(t):
                knew[g][tt * KV_ROWS + h:tt * KV_ROWS + h + 1, :] = k_new[tt:tt + 1, :]
                knew[g][tt * KV_ROWS + N_HEADS + h:tt * KV_ROWS + N_HEADS + h + 1, :] = v_new[tt:tt + 1, :]
            k_old = cache[pl.ds(h, lb, stride=KV_ROWS), :].astype(BF16)
            v_old = cache[pl.ds(N_HEADS + h, lb, stride=KV_ROWS), :].astype(BF16)
            s_c = jnp.where(valid_c, _dot_nt(q, k_old), NEG)
            s_n = jnp.where(valid_n, _dot_nt(q, k_new.astype(BF16)), NEG)
            m = jnp.maximum(jnp.max(s_c, axis=-1, keepdims=True), jnp.max(s_n, axis=-1, keepdims=True))
            p_c = jnp.exp(s_c - m)
            p_n = jnp.exp(s_n - m)
            den = jnp.sum(p_c, axis=-1, keepdims=True) + jnp.sum(p_n, axis=-1, keepdims=True)
            o = (_dot((p_c / den).astype(BF16), v_old)
                 + _dot((p_n / den).astype(BF16), v_new.astype(BF16)))
            outs[h][g] = o
            lses[h][g] = m + jnp.log(den)
    for h in range(N_HEADS):
        hs = slice(h * HEAD_DIM, (h + 1) * HEAD_DIM)
        m = functools.reduce(jnp.maximum, lses[h])
        es = [jnp.exp(a - m) for a in lses[h]]
        den = functools.reduce(lambda a, c: a + c, es)
        y = functools.reduce(lambda a, c: a + c, [(e / den) * o for e, o in zip(es, outs[h])])
        yc_ref[:, hs] = y[0:t, :].astype(BF16)


def _attn_sample(proj3, caches, l, lp):
    b, t, _ = proj3.shape
    ng = len(DIL_GROUPS)
    cache_rows = tuple(c.shape[2] // KV_ROWS for c in caches)
    tok = lambda c0: pl.BlockSpec((None, t, HW), lambda i: (i, 0, c0))
    in_specs = [tok(n) for n in range(3 * ng)]
    in_specs += [pl.BlockSpec((None, None, lb * KV_ROWS, HEAD_DIM), lambda i: (l, i, 0, 0)) for lb in cache_rows]
    res = pl.pallas_call(
        functools.partial(_attn_s_kernel, t=t, lp=lp, cache_rows=cache_rows),
        grid=(b,),
        in_specs=in_specs,
        out_specs=[pl.BlockSpec((None, t, HW), lambda i: (i, 0, 0))]
                  + [pl.BlockSpec((None, t * KV_ROWS, HEAD_DIM), lambda i: (i, 0, 0))] * ng,
        out_shape=[jax.ShapeDtypeStruct((b, t, HW), BF16)]
                  + [jax.ShapeDtypeStruct((b, t * KV_ROWS, HEAD_DIM), F32)] * ng,
        scratch_shapes=[pltpu.VMEM((lp, HW), F32)] * 3,
        compiler_params=_params("parallel"),
        name="attn_sample",
    )(*([proj3] * (3 * ng)), *caches)
    return res[0], res[1:]


def _cache_shift_kernel(cur_ref, nxt_ref, new_ref, o_ref, *, ch, shift, n_chunks):
    c = pl.program_id(2)
    o_ref[0:ch - shift, :] = cur_ref[shift:ch, :]
    o_ref[ch - shift:ch, :] = jnp.where(c == n_chunks - 1, new_ref[...], nxt_ref[...])


def _cache_shift(cache, new):
    depth, b, rows, d = cache.shape
    shift = new.shape[2]
    ch = min(rows, 8192)
    n_chunks = rows // ch
    assert n_chunks * ch == rows and ch % shift == 0 and shift % 8 == 0
    per = ch // shift
    last = rows // shift - 1
    return pl.pallas_call(
        functools.partial(_cache_shift_kernel, ch=ch, shift=shift, n_chunks=n_chunks),
        grid=(depth, b, n_chunks),
        in_specs=[pl.BlockSpec((None, None, ch, d), lambda l, i, c: (l, i, c, 0)),
                  pl.BlockSpec((None, None, shift, d), lambda l, i, c: (l, i, jnp.minimum((c + 1) * per, last), 0)),
                  pl.BlockSpec((None, None, shift, d), lambda l, i, c: (l, i, 0, 0))],
        out_specs=pl.BlockSpec((None, None, ch, d), lambda l, i, c: (l, i, c, 0)),
        out_shape=jax.ShapeDtypeStruct(cache.shape, cache.dtype),
        compiler_params=_params("parallel", "parallel", "arbitrary"),
        name="cache_shift",
    )(cache, cache, new)


def _win_pack_kernel(k_ref, v_ref, o_ref, *, rows):
    for h in range(N_HEADS):
        o_ref[pl.ds(h, rows, stride=KV_ROWS), :] = k_ref[h]
        o_ref[pl.ds(N_HEADS + h, rows, stride=KV_ROWS), :] = v_ref[h]


def _win_pack(qkv_h, g, b, t, keep):
    ch = min(keep, 512)
    first = (t - keep) // ch
    per_b = t // ch
    k_blk, v_blk = 3 * g + 1, 3 * g + 2
    return pl.pallas_call(
        functools.partial(_win_pack_kernel, rows=ch),
        grid=(b, keep // ch),
        in_specs=[pl.BlockSpec((N_HEADS, ch, HEAD_DIM), lambda i, c: (k_blk, i * per_b + first + c, 0)),
                  pl.BlockSpec((N_HEADS, ch, HEAD_DIM), lambda i, c: (v_blk, i * per_b + first + c, 0))],
        out_specs=pl.BlockSpec((None, ch * KV_ROWS, HEAD_DIM), lambda i, c: (i, c, 0)),
        out_shape=jax.ShapeDtypeStruct((b, keep * KV_ROWS, HEAD_DIM), F32),
        compiler_params=_params("parallel", "parallel"),
        name="win_pack",
    )(qkv_h, qkv_h)


def _merge_kernel(h_ref, ya, yb, yc, hs_ref, ya_s, yb_s, yc_s, wga, wgb, wgc, wa, wb, wc, o_ref, os_ref, wg_ref):
    tn = o_ref.shape[1]
    for n, src in enumerate((wga, wgb, wgc)):
        wg_ref[:, n * tn:(n + 1) * tn] = src[...].astype(BF16)

    def merged(h, a, b, c):
        gates = jax.nn.sigmoid(_dot(h[...], wg_ref[...]))
        acc = gates[:, 0:tn] * _dot(a[...], wa[...])
        acc = acc + gates[:, tn:2 * tn] * _dot(b[...], wb[...])
        acc = acc + gates[:, 2 * tn:3 * tn] * _dot(c[...], wc[...])
        return acc.astype(BF16)

    o_ref[...] = merged(h_ref, ya, yb, yc)

    @pl.when(pl.program_id(0) == 0)
    def _():
        os_ref[0] = merged(hs_ref, ya_s, yb_s, yc_s)

    @pl.when(pl.program_id(0) > 0)
    def _():
        os_ref[...] = jnp.zeros_like(os_ref)


def _merge(h, ya, yb, yc, hs, ya_s, yb_s, yc_s, w_in, wa, wb, wc, l, tm, tn):
    m = h.shape[0]
    ms = hs.shape[0]
    act = lambda w: pl.BlockSpec((tm, w), lambda i, j: (i, 0))
    act_s = lambda w: pl.BlockSpec((ms, w), lambda i, j: (0, 0))
    wgt = lambda k: pl.BlockSpec((None, k, tn), lambda i, j: (l, 0, j))
    wgate = lambda n: pl.BlockSpec((None, D_MODEL, tn), lambda i, j: (l, 0, (OFF_GATE + n * D_MODEL) // tn + j))
    res = pl.pallas_call(
        _merge_kernel,
        grid=(m // tm, D_MODEL // tn),
        in_specs=[act(D_MODEL), act(CONV_DIM), act(RV_W), act(HW),
                  act_s(D_MODEL), act_s(CONV_DIM), act_s(RV_W), act_s(HW),
                  wgate(0), wgate(1), wgate(2), wgt(CONV_DIM), wgt(RV_W), wgt(HW)],
        out_specs=[pl.BlockSpec((tm, tn), lambda i, j: (i, j)),
                   pl.BlockSpec((1, ms, tn), lambda i, j: (i, 0, j))],
        out_shape=[jax.ShapeDtypeStruct((m, D_MODEL), BF16),
                   jax.ShapeDtypeStruct((m // tm, ms, D_MODEL), BF16)],
        scratch_shapes=[pltpu.VMEM((D_MODEL, 3 * tn), BF16)],
        compiler_params=_params("parallel", "parallel"),
        name="merge",
    )(h, ya, yb, yc, hs, ya_s, yb_s, yc_s, w_in, w_in, w_in, wa, wb, wc)
    return res[0], res[1][0]


def _post_norm_residual(x, y, gpost, gate):
    r = y * lax.rsqrt(jnp.mean(y * y, axis=-1, keepdims=True) + EPS) * gpost
    return x + gate * r


def _mod_rows(ref, rs):
    v = ref[0]
    return v if v.shape[0] == 1 else v[rs, :]


def _outproj_kernel(m_ref, ms_ref, w_ref, x_ref, xs_ref, gpost_ref, gffn_ref, g1_ref, sc2_ref, sh2_ref,
                    g1s_ref, sc2s_ref, sh2s_ref, xo_ref, h2_ref, xso_ref, h2s_ref, *, sub):
    def finish(rows_m, rows_x, g1, sc2, sh2):
        y = _dot(rows_m, w_ref[...])
        xn = _post_norm_residual(rows_x, y, gpost_ref[0], g1)
        return xn, _norm_mod(xn, gffn_ref[0], sc2, sh2).astype(BF16)

    tm = m_ref.shape[0]
    for r in range(0, tm, sub):
        rs = slice(r, r + sub)
        xo_ref[rs, :], h2_ref[rs, :] = finish(m_ref[rs, :], x_ref[rs, :], _mod_rows(g1_ref, rs),
                                              _mod_rows(sc2_ref, rs), _mod_rows(sh2_ref, rs))

    @pl.when(pl.program_id(0) == 0)
    def _():
        xso_ref[...], h2s_ref[...] = finish(ms_ref[...], xs_ref[...], g1s_ref[0], sc2s_ref[0], sh2s_ref[0])


def _outproj(merged, merged_s, w_out, l, x, x_s, gpost, gffn, mods, mods_s, rows):
    m, d = x.shape
    ms = x_s.shape[0]
    tm = min(rows.tm, 512)
    sub = _Rows(m, tm, rows.tiles_per_b * rows.tm)
    row = pl.BlockSpec((tm, d), lambda i: (i, 0))
    row_s = pl.BlockSpec((ms, d), lambda i: (0, 0))
    mod_s = pl.BlockSpec((1, ms, d), lambda i: (0, 0, 0))
    vec = pl.BlockSpec((1, 1, d), lambda i: (l, 0, 0))
    return pl.pallas_call(
        functools.partial(_outproj_kernel, sub=min(tm, 128)),
        grid=(m // tm,),
        in_specs=[row, row_s, pl.BlockSpec((None, d, d), lambda i: (l, 0, 0), pipeline_mode=pl.Buffered(1)),
                  row, row_s, vec, vec, sub.mod_spec(), sub.mod_spec(), sub.mod_spec(), mod_s, mod_s, mod_s],
        out_specs=[row, row, row_s, row_s],
        out_shape=[jax.ShapeDtypeStruct((m, d), F32), jax.ShapeDtypeStruct((m, d), BF16),
                   jax.ShapeDtypeStruct((ms, d), F32), jax.ShapeDtypeStruct((ms, d), BF16)],
        compiler_params=_params("arbitrary"),
        name="outproj",
    )(merged, merged_s, w_out, x, x_s, gpost, gffn, *mods, *mods_s)


def _down_kernel(*refs, nk, with_next):
    n_in = 13 if with_next else 8
    n_out = 4 if with_next else 2
    ins, outs, (acc_ref, accs_ref) = refs[:n_in], refs[n_in:n_in + n_out], refs[n_in + n_out:]
    a_ref, as_ref, w_ref, x_ref, xs_ref, gpost_ref, g2_ref, g2s_ref = ins[:8]
    i = pl.program_id(0)
    k = pl.program_id(1)

    def finish(x_r, acc_r, g2_r, pos):
        xn = _post_norm_residual(x_r[...], acc_r[...], gpost_ref[0], g2_r[0])
        outs[pos][...] = xn
        if with_next:
            gn_ref, scn_ref, shn_ref = ins[8], ins[9 + pos], ins[11 + pos]
            outs[2 + pos][...] = _norm_mod(xn, gn_ref[0], scn_ref[0], shn_ref[0]).astype(BF16)

    @pl.when(k == 0)
    def _():
        acc_ref[...] = jnp.zeros_like(acc_ref)

    acc_ref[...] += _dot(a_ref[...], w_ref[...])

    @pl.when(k == nk - 1)
    def _():
        finish(x_ref, acc_ref, g2_ref, 0)

    @pl.when(i == 0)
    def _():
        @pl.when(k == 0)
        def _():
            accs_ref[...] = jnp.zeros_like(accs_ref)

        accs_ref[...] += _dot(as_ref[...], w_ref[...])

        @pl.when(k == nk - 1)
        def _():
            finish(xs_ref, accs_ref, g2s_ref, 1)


def _down(a, a_s, w_down, l, x, x_s, gpost, g2, g2_s, nxt, rows):
    m, d = x.shape
    ms = x_s.shape[0]
    kdim = a.shape[1]
    tm = min(rows.tm, 1024)
    tk = 1024
    nk = kdim // tk
    sub = _Rows(m, tm, None if rows.per_row else rows.tiles_per_b * rows.tm)
    one = lambda shape, imap: pl.BlockSpec(shape, imap, pipeline_mode=pl.Buffered(1))
    row = one((tm, d), lambda i, k: (i, 0))
    row_s = pl.BlockSpec((ms, d), lambda i, k: (0, 0))
    mod_s = pl.BlockSpec((1, ms, d), lambda i, k: (0, 0, 0))
    vec = pl.BlockSpec((1, 1, d), lambda i, k: (l, 0, 0))
    in_specs = [pl.BlockSpec((tm, tk), lambda i, k: (i, k)),
                pl.BlockSpec((ms, tk), lambda i, k: (0, k)),
                pl.BlockSpec((None, tk, d), lambda i, k: (l, k, 0)),
                row, row_s, vec, sub.mod_spec(), mod_s]
    args = [a, a_s, w_down, x, x_s, gpost, g2, g2_s]
    out_specs = [row, row_s]
    out_shape = [jax.ShapeDtypeStruct((m, d), F32), jax.ShapeDtypeStruct((ms, d), F32)]
    if nxt is not None:
        gn, scn, shn, scn_s, shn_s = nxt
        in_specs += [pl.BlockSpec((1, 1, d), lambda i, k: (l + 1, 0, 0)), sub.mod_spec(), mod_s, sub.mod_spec(), mod_s]
        args += [gn, scn, scn_s, shn, shn_s]
        out_specs += [row, row_s]
        out_shape += [jax.ShapeDtypeStruct((m, d), BF16), jax.ShapeDtypeStruct((ms, d), BF16)]
    return pl.pallas_call(
        functools.partial(_down_kernel, nk=nk, with_next=nxt is not None),
        grid=(m // tm, nk),
        in_specs=in_specs,
        out_specs=out_specs,
        out_shape=out_shape,
        scratch_shapes=[pltpu.VMEM((tm, d), F32), pltpu.VMEM((ms, d), F32)],
        compiler_params=_params("arbitrary", "arbitrary", vmem=VMEM_LIMIT_MAX),
        name="mlp_down",
    )(*args)


def kernel(x_prompt, x_sample, c_prompt, c_sample, state_conv, state_retention, cache_win128, cache_win512, cache_win2048, norm_mix_g, norm_mix_post_g, norm_ffn_g, norm_ffn_post_g, w_mod, b_mod, w_in, conv_w, conv_b, conv_ln_g, conv_ln_b, w_branch_a, w_branch_b, w_branch_c, w_out, w_up, w_down):
    depth = w_in.shape[0]
    bp, tp, d = x_prompt.shape
    bs, ts, _ = x_sample.shape
    mp, ms = bp * tp, bs * ts
    ng = len(DIL_GROUPS)

    w_a_b, w_b_b, w_c_b, w_out_b, w_down_b = [
        w.astype(BF16) for w in (w_branch_a, w_branch_b, w_branch_c, w_out, w_down)]
    conv_w8 = jnp.broadcast_to(conv_w[:, :, None, :], (depth, CONV_WIDTH, SUBLANES, CONV_DIM))
    vec3 = lambda a: a.reshape(depth, 1, a.shape[-1])
    g_mix, g_mix_post, g_ffn, g_ffn_post = map(vec3, (norm_mix_g, norm_mix_post_g, norm_ffn_g, norm_ffn_post_g))
    conv_b3, ln_g3, ln_b3 = map(vec3, (conv_b, conv_ln_g, conv_ln_b))
    caches = [c.reshape(c.shape[0], c.shape[1], c.shape[2] * KV_ROWS, HEAD_DIM)
              for c in (cache_win128, cache_win512, cache_win2048)]

    lp_s = 16
    qkv_rope = _qkv_rope_tables(np.arange(tp)) + _qkv_rope_tables(np.tile(PAST_LEN + np.arange(ts), bs))
    retn_p = _rope_tables(np.arange(tp), _retnet_inv_freq(), tp)
    retn_s = _rope_tables(PAST_LEN + np.arange(ts), _retnet_inv_freq(), lp_s)

    n_c = bp + bs
    c_rows = -(-n_c // 8) * 8
    c_all = jnp.concatenate([c_prompt, c_sample, jnp.zeros((c_rows - n_c, d), F32)], axis=0)
    mod = _modulation(c_all, w_mod, b_mod)

    rows_p = _Rows(mp, 1024, tp)
    rows_s = _Rows(ms, ms, None)

    def mods(l):
        parts_p = [mod[l, :bp, i * d:(i + 1) * d].reshape(bp, 1, d) for i in range(6)]
        parts_s = [jnp.repeat(mod[l, bp:n_c, i * d:(i + 1) * d], ts, axis=0).reshape(1, ms, d) for i in range(6)]
        return parts_p, parts_s

    xp = x_prompt.reshape(mp, d)
    xs = x_sample.reshape(ms, d)
    mod_p, mod_s = mods(0)
    hp = _prenorm(xp, g_mix, 0, mod_p[1], mod_p[0], rows_p)
    hs = _prenorm(xs, g_mix, 0, mod_s[1], mod_s[0], rows_s)

    conv_p, conv_s, ret_p, ret_s = [], [], [], []
    win_p = [[] for _ in range(ng)]
    new_kv_s = [[] for _ in range(ng)]

    for l in range(depth):
        sh1p, sc1p, g1p, sh2p, sc2p, g2p = mod_p
        sh1s, sc1s, g1s, sh2s, sc2s, g2s = mod_s

        proj_ab, proj_ab_s = _matmul(hp, hs, w_in, l, F32, 2048, 512, ncols=OFF_DIL, name="matmul_in_ab")
        qkv_h, qkv_s = _matmul(hp, hs, w_in, l, F32, 2048, HW, col0=OFF_DIL, ncols=OFF_GATE - OFF_DIL,
                               heads=True, rope=qkv_rope, name="matmul_in_qkv")

        proj3 = proj_ab.reshape(bp, tp, OFF_DIL)
        ya, st = _conv_prompt(proj3, l, conv_w8, conv_b3, ln_g3, ln_b3)
        conv_p.append(st[:, CONV_HALO - (CONV_WIDTH - 1):])
        yb, s_new = _retention(proj3, retn_p[0], retn_p[1], None, l)
        ret_p.append(s_new)
        yc = _attn_prompt(qkv_h, bp, tp)
        for g, (window, dil) in enumerate(DIL_GROUPS):
            keep = min(window, tp)
            win_p[g].append(_win_pack(qkv_h, g, bp, tp, keep))
        proj_s3 = proj_ab_s.reshape(bs, ts, OFF_DIL)
        ya_s, st_s = _conv_sample(proj_s3, l, state_conv, conv_w, conv_b3, ln_g3, ln_b3)
        conv_s.append(st_s)
        yb_s, s_new_s = _retention(proj_s3, retn_s[0], retn_s[1], state_retention, l)
        ret_s.append(s_new_s)
        yc_s, kv_new = _attn_sample(qkv_s.reshape(bs, ts, OFF_GATE - OFF_DIL), caches, l, lp_s)
        for g in range(ng):
            new_kv_s[g].append(kv_new[g])
        merged, merged_s = _merge(hp, ya.reshape(mp, CONV_DIM), yb.reshape(mp, RV_W), yc,
                                  hs, ya_s.reshape(ms, CONV_DIM), yb_s.reshape(ms, RV_W), yc_s.reshape(ms, HW),
                                  w_in, w_a_b, w_b_b, w_c_b, l, 1024, 256)
        xp, h2, xs, h2_s = _outproj(merged, merged_s, w_out_b, l, xp, xs, g_mix_post, g_ffn,
                                    (g1p, sc2p, sh2p), (g1s, sc2s, sh2s), rows_p)
        a, a_s = _matmul(h2, h2_s, w_up, l, BF16, 2048, 1024, act="relu2", name="mlp_up")
        if l + 1 < depth:
            mod_p_n, mod_s_n = mods(l + 1)
            nxt = (g_mix, mod_p_n[1], mod_p_n[0], mod_s_n[1], mod_s_n[0])
            xp, xs, hp, hs = _down(a, a_s, w_down_b, l, xp, xs, g_ffn_post, g2p, g2s, nxt, rows_p)
            mod_p, mod_s = mod_p_n, mod_s_n
        else:
            xp, xs = _down(a, a_s, w_down_b, l, xp, xs, g_ffn_post, g2p, g2s, None, rows_p)

    win_s = [_cache_shift(caches[g], jnp.stack(new_kv_s[g])) for g in range(ng)]
    kv5 = lambda a: a.reshape(a.shape[0], a.shape[1], a.shape[2] // KV_ROWS, 2, N_HEADS, HEAD_DIM)
    win_p = [kv5(jnp.stack(w)) for w in win_p]
    win_s = [kv5(w) for w in win_s]

    return (xp.reshape(bp, tp, d), xs.reshape(bs, ts, d),
            jnp.stack(conv_p), jnp.stack(conv_s), jnp.stack(ret_p), jnp.stack(ret_s),
            win_p[0], win_s[0], win_p[1], win_s[1], win_p[2], win_s[2])
```

```python
import functools
import math

import numpy as np
import jax
import jax.numpy as jnp
from jax import lax
from jax.experimental import pallas as pl
from jax.experimental.pallas import tpu as pltpu

F32 = jnp.float32
BF16 = jnp.bfloat16

D_MODEL = 2048
PAST_LEN = 16384
HEAD_DIM = 128
CONV_DIM = D_MODEL // 4
CONV_WIDTH = 31
CONV_HALO = 32
N_HEADS = D_MODEL // 512
RET_QK = 128
RET_V = 256
RET_CHUNK = 128
DIL_GROUPS = ((128, 1), (512, 4), (2048, 16))
ATTN_BLOCK = 128
D_FF = 4 * D_MODEL
ROPE_THETA = 10000.0
EPS = 1e-6
HW = N_HEADS * HEAD_DIM
RV_W = N_HEADS * RET_V
KV_ROWS = 2 * N_HEADS
SUBLANES = 8

OFF_CONV_A = 0
OFF_CONV_B = CONV_DIM
OFF_RQ = 2 * CONV_DIM
OFF_RK = OFF_RQ + HW
OFF_RV = OFF_RK + HW
OFF_RG = OFF_RV + RV_W
OFF_DIL = OFF_RG + RV_W
OFF_GATE = OFF_DIL + 3 * len(DIL_GROUPS) * HW
N_IN = OFF_GATE + 3 * D_MODEL

NEG = -1e30
VMEM_LIMIT = 52 * 1024 * 1024
VMEM_LIMIT_MAX = 58 * 1024 * 1024


def _params(*sem, vmem=VMEM_LIMIT):
    return pltpu.CompilerParams(dimension_semantics=sem, vmem_limit_bytes=vmem)


def _silu(x):
    return x * jax.nn.sigmoid(x)


def _rot(x, cos2, sin2):
    return x * cos2 + pltpu.roll(x, HEAD_DIM // 2, axis=1) * sin2


def _dot(a, b):
    return jnp.dot(a, b, preferred_element_type=F32)


def _dot_nt(a, b):
    return lax.dot_general(a, b, (((1,), (1,)), ((), ())), preferred_element_type=F32)


def _dot_tn(a, b):
    return lax.dot_general(a, b, (((0,), (0,)), ((), ())), preferred_element_type=F32)


def _rope_tables(pos, inv_freq, rows):
    ang = np.asarray(pos, np.float64)[:, None] * inv_freq[None, :]
    cos = np.concatenate([np.cos(ang), np.cos(ang)], axis=-1)
    sin = np.concatenate([-np.sin(ang), np.sin(ang)], axis=-1)
    pad = rows - cos.shape[0]
    if pad:
        cos = np.pad(cos, ((0, pad), (0, 0)))
        sin = np.pad(sin, ((0, pad), (0, 0)))
    return jnp.asarray(cos, F32), jnp.asarray(sin, F32)


def _qkv_rope_tables(pos):
    cos, sin = _rope_tables(pos, _rope_inv_freq(), len(pos))
    scale = HEAD_DIM ** -0.5
    return (jnp.stack([cos * scale, cos, jnp.ones_like(cos)]),
            jnp.stack([sin * scale, sin, jnp.zeros_like(sin)]))


def _rope_inv_freq():
    return 1.0 / (ROPE_THETA ** (np.arange(0, HEAD_DIM, 2, dtype=np.float64) / HEAD_DIM))


def _retnet_inv_freq():
    return 1.0 / (ROPE_THETA ** np.linspace(0.0, 1.0, RET_QK // 2, dtype=np.float64))


def _ret_decay_tables(rows, lp):
    lg = np.log(1.0 - 2.0 ** (-5.0 - np.arange(N_HEADS, dtype=np.float64)))
    i = np.arange(lp, dtype=np.float64)
    rel = i[:, None] - i[None, :]
    real = (i < rows)
    dmask = np.where((rel >= 0) & real[:, None] & real[None, :],
                     np.exp(np.maximum(rel, 0.0)[None] * lg[:, None, None]), 0.0)
    qdec = np.where(real[None, :], np.exp((i[None, :] + 1.0) * lg[:, None]), 0.0)
    kdec = np.where(real[None, :], np.exp((rows - 1.0 - i[None, :]) * lg[:, None]), 0.0)
    cdec = tuple(float(v) for v in np.exp(rows * lg))
    bl = lambda a: jnp.asarray(np.broadcast_to(a[:, :, None], (N_HEADS, lp, HEAD_DIM)), F32)
    return jnp.asarray(dmask, F32), bl(qdec), bl(kdec), cdec


class _Rows:
    def __init__(self, m, tm, rows_per_batch):
        self.m = m
        self.tm = tm
        self.per_row = rows_per_batch is None
        self.tiles_per_b = None if self.per_row else rows_per_batch // tm

    def mod_spec(self):
        if self.per_row:
            return pl.BlockSpec((1, self.tm, D_MODEL), lambda i, *_: (0, i, 0))
        tpb = self.tiles_per_b
        return pl.BlockSpec((1, 1, D_MODEL), lambda i, *_: (i // tpb, 0, 0))


def _mod_kernel(c_ref, w_ref, b_ref, o_ref):
    c = c_ref[...]
    s = _silu(c).astype(BF16)
    o_ref[0] = _dot(s, w_ref[0].astype(BF16)) + b_ref[0]


def _modulation(c_all, w_mod, b_mod):
    depth, d, n = w_mod.shape
    rows = c_all.shape[0]
    tn = 1024
    return pl.pallas_call(
        _mod_kernel,
        grid=(depth, n // tn),
        in_specs=[pl.BlockSpec((rows, d), lambda l, j: (0, 0)),
                  pl.BlockSpec((1, d, tn), lambda l, j: (l, 0, j)),
                  pl.BlockSpec((1, 1, tn), lambda l, j: (l, 0, j))],
        out_specs=pl.BlockSpec((1, rows, tn), lambda l, j: (l, 0, j)),
        out_shape=jax.ShapeDtypeStruct((depth, rows, n), F32),
        compiler_params=_params("parallel", "parallel"),
        name="modulation",
    )(c_all, w_mod, b_mod.reshape(depth, 1, n))


def _norm_mod(x, g, sc, sh):
    y = x * lax.rsqrt(jnp.mean(x * x, axis=-1, keepdims=True) + EPS) * g
    return y * (1.0 + sc) + sh


def _prenorm_kernel(x_ref, g_ref, sc_ref, sh_ref, h_ref):
    h_ref[...] = _norm_mod(x_ref[...], g_ref[0], sc_ref[0], sh_ref[0]).astype(BF16)


def _prenorm(x, g, l, sc, sh, rows):
    m, d = x.shape
    tm = rows.tm
    return pl.pallas_call(
        _prenorm_kernel,
        grid=(m // tm,),
        in_specs=[pl.BlockSpec((tm, d), lambda i: (i, 0)),
                  pl.BlockSpec((1, 1, d), lambda i: (l, 0, 0)),
                  rows.mod_spec(), rows.mod_spec()],
        out_specs=pl.BlockSpec((tm, d), lambda i: (i, 0)),
        out_shape=jax.ShapeDtypeStruct((m, d), BF16),
        compiler_params=_params("parallel"),
        name="prenorm",
    )(x, g, sc, sh)


def _mm_kernel(*refs, act, heads, rope):
    if rope:
        x_ref, xs_ref, w_ref, c_ref, s_ref, cs_ref, ss_ref, o_ref, os_ref = refs
    else:
        x_ref, xs_ref, w_ref, o_ref, os_ref = refs

    def product(rows_ref, cos_ref=None, sin_ref=None):
        y = _dot(rows_ref[...], w_ref[...].astype(BF16))
        if act == "relu2":
            y = jnp.square(jnp.maximum(y, 0.0))
        if rope:
            cos2, sin2 = cos_ref[...], sin_ref[...]
            y = jnp.concatenate([_rot(y[:, h * HEAD_DIM:(h + 1) * HEAD_DIM], cos2, sin2)
                                 for h in range(y.shape[1] // HEAD_DIM)], axis=1)
        return y

    y = product(x_ref, *((c_ref, s_ref) if rope else ()))
    if heads:
        for h in range(o_ref.shape[0]):
            o_ref[h] = y[:, h * HEAD_DIM:(h + 1) * HEAD_DIM]
    else:
        o_ref[...] = y.astype(o_ref.dtype)

    @pl.when(pl.program_id(0) == 0)
    def _():
        os_ref[0] = product(xs_ref, *((cs_ref, ss_ref) if rope else ())).astype(os_ref.dtype)

    @pl.when(pl.program_id(0) > 0)
    def _():
        os_ref[...] = jnp.zeros_like(os_ref)


def _matmul(x, xs, w, l, out_dtype, tm, tn, act=None, col0=0, ncols=None, heads=False, rope=None, name="matmul"):
    m, k = x.shape
    ms = xs.shape[0]
    n = w.shape[-1] if ncols is None else ncols
    c0 = col0 // tn
    assert c0 * tn == col0 and n % tn == 0 and tn % HEAD_DIM == 0
    if heads:
        out_spec = pl.BlockSpec((tn // HEAD_DIM, tm, HEAD_DIM), lambda i, j: (j, i, 0))
        out_shape = jax.ShapeDtypeStruct((n // HEAD_DIM, m, HEAD_DIM), out_dtype)
    else:
        out_spec = pl.BlockSpec((tm, tn), lambda i, j: (i, j))
        out_shape = jax.ShapeDtypeStruct((m, n), out_dtype)
    in_specs = [pl.BlockSpec((tm, k), lambda i, j: (i, 0)),
                pl.BlockSpec((ms, k), lambda i, j: (0, 0)),
                pl.BlockSpec((None, k, tn), lambda i, j: (l, 0, c0 + j))]
    args = [x, xs, w]
    if rope is not None:
        period, t = rope[0].shape[0], rope[0].shape[1]
        per_seq = t // tm
        assert per_seq * tm == t
        tab = pl.BlockSpec((None, tm, HEAD_DIM), lambda i, j: (j % period, i % per_seq, 0))
        tab_s = pl.BlockSpec((None, ms, HEAD_DIM), lambda i, j: (j % period, 0, 0))
        in_specs += [tab, tab, tab_s, tab_s]
        args += list(rope)
    res = pl.pallas_call(
        functools.partial(_mm_kernel, act=act, heads=heads, rope=rope is not None),
        grid=(m // tm, n // tn),
        in_specs=in_specs,
        out_specs=[out_spec, pl.BlockSpec((1, ms, tn), lambda i, j: (i, 0, j))],
        out_shape=[out_shape, jax.ShapeDtypeStruct((m // tm, ms, n), out_dtype)],
        compiler_params=_params("parallel", "parallel"),
        name=name,
    )(*args)
    return res[0], res[1][0]


def _conv_ln_silu(full_ref, first, rows, w_ref, cb, lg, lb):
    acc = jnp.zeros((rows, CONV_DIM), F32)
    for j in range(CONV_WIDTH):
        acc = acc + full_ref[first + j:first + j + rows, :] * w_ref[j:j + 1, :]
    y = acc + cb
    yc = y - jnp.mean(y, axis=-1, keepdims=True)
    z = yc * lax.rsqrt(jnp.mean(yc * yc, axis=-1, keepdims=True) + EPS) * lg + lb
    return _silu(z)


def _conv_p_kernel(a_ref, b_ref, ha_ref, hb_ref, w_ref, cb_ref, lg_ref, lb_ref, ya_ref, st_ref, sh_ref,
                   *, tt, rc):
    t = pl.program_id(1)
    u = a_ref[...] * jax.nn.sigmoid(b_ref[...])
    uh = ha_ref[...] * jax.nn.sigmoid(hb_ref[...])
    sh_ref[0, 0:CONV_HALO, :] = jnp.where(t > 0, uh, 0.0)
    sh_ref[0, CONV_HALO:, :] = u
    st_ref[...] = u[tt - CONV_HALO:, :]
    n_sh = tt + CONV_HALO - SUBLANES
    for s in range(1, SUBLANES):
        sh_ref[s, 0:n_sh, :] = sh_ref[0, s:s + n_sh, :]
    skip = CONV_HALO - (CONV_WIDTH - 1)
    cb, lg, lb = cb_ref[...], lg_ref[...], lb_ref[...]
    for r0 in range(0, tt, rc):
        acc = jnp.zeros((rc, CONV_DIM), F32)
        for j in range(CONV_WIDTH):
            first = r0 + skip + j
            s = first % SUBLANES
            wj = jnp.tile(w_ref[j], (rc // SUBLANES, 1))
            acc = acc + sh_ref[s, first - s:first - s + rc, :] * wj
        y = acc + cb
        yc = y - jnp.mean(y, axis=-1, keepdims=True)
        z = yc * lax.rsqrt(jnp.mean(yc * yc, axis=-1, keepdims=True) + EPS) * lg + lb
        ya_ref[r0:r0 + rc, :] = _silu(z).astype(BF16)


def _conv_prompt(proj3, l, conv_w8, conv_b, ln_g, ln_b):
    b, t, _ = proj3.shape
    tt, rc = 256, 32
    hpt = tt // CONV_HALO
    vec = pl.BlockSpec((None, 1, CONV_DIM), lambda i, j: (l, 0, 0))
    return pl.pallas_call(
        functools.partial(_conv_p_kernel, tt=tt, rc=rc),
        grid=(b, t // tt),
        in_specs=[pl.BlockSpec((None, tt, CONV_DIM), lambda i, j: (i, j, 0)),
                  pl.BlockSpec((None, tt, CONV_DIM), lambda i, j: (i, j, 1)),
                  pl.BlockSpec((None, CONV_HALO, CONV_DIM), lambda i, j: (i, jnp.maximum(j * hpt - 1, 0), 0)),
                  pl.BlockSpec((None, CONV_HALO, CONV_DIM), lambda i, j: (i, jnp.maximum(j * hpt - 1, 0), 1)),
                  pl.BlockSpec((None, CONV_WIDTH, SUBLANES, CONV_DIM), lambda i, j: (l, 0, 0, 0)),
                  vec, vec, vec],
        out_specs=[pl.BlockSpec((None, tt, CONV_DIM), lambda i, j: (i, j, 0)),
                   pl.BlockSpec((None, CONV_HALO, CONV_DIM), lambda i, j: (i, 0, 0))],
        out_shape=[jax.ShapeDtypeStruct((b, t, CONV_DIM), BF16),
                   jax.ShapeDtypeStruct((b, CONV_HALO, CONV_DIM), F32)],
        scratch_shapes=[pltpu.VMEM((SUBLANES, tt + CONV_HALO, CONV_DIM), F32)],
        compiler_params=_params("parallel", "arbitrary"),
        name="conv_prompt",
    )(proj3, proj3, proj3, proj3, conv_w8, conv_b, ln_g, ln_b)


def _conv_s_kernel(a_ref, b_ref, st_in_ref, w_ref, cb_ref, lg_ref, lb_ref, ya_ref, st_ref, full_ref, *, t):
    keep = CONV_WIDTH - 1
    u = a_ref[...] * jax.nn.sigmoid(b_ref[...])
    full_ref[0:keep, :] = st_in_ref[...]
    full_ref[keep:keep + t, :] = u
    st_ref[...] = full_ref[t:t + keep, :]
    y = _conv_ln_silu(full_ref, 0, t, w_ref, cb_ref[...], lg_ref[...], lb_ref[...])
    ya_ref[...] = y.astype(BF16)


def _conv_sample(proj3, l, state_conv, conv_w, conv_b, ln_g, ln_b):
    b, t, _ = proj3.shape
    keep = CONV_WIDTH - 1
    vec = pl.BlockSpec((None, 1, CONV_DIM), lambda i: (l, 0, 0))
    return pl.pallas_call(
        functools.partial(_conv_s_kernel, t=t),
        grid=(b,),
        in_specs=[pl.BlockSpec((None, t, CONV_DIM), lambda i: (i, 0, 0)),
                  pl.BlockSpec((None, t, CONV_DIM), lambda i: (i, 0, 1)),
                  pl.BlockSpec((None, None, keep, CONV_DIM), lambda i: (l, i, 0, 0)),
                  pl.BlockSpec((None, CONV_WIDTH, CONV_DIM), lambda i: (l, 0, 0)),
                  vec, vec, vec],
        out_specs=[pl.BlockSpec((None, t, CONV_DIM), lambda i: (i, 0, 0)),
                   pl.BlockSpec((None, keep, CONV_DIM), lambda i: (i, 0, 0))],
        out_shape=[jax.ShapeDtypeStruct((b, t, CONV_DIM), BF16),
                   jax.ShapeDtypeStruct((b, keep, CONV_DIM), F32)],
        scratch_shapes=[pltpu.VMEM((keep + 8 + t, CONV_DIM), F32)],
        compiler_params=_params("parallel"),
        name="conv_sample",
    )(proj3, proj3, state_conv, conv_w, conv_b, ln_g, ln_b)


def _ret_kernel(*refs, rows, lp, has_s0, cdec, n_steps, cpb):
    if has_s0:
        rq, rk, rv, rg, cos, sin, dm, qd, kd, s0, yb, sout, s_ref, qp, kp, vp, gp = refs
    else:
        rq, rk, rv, rg, cos, sin, dm, qd, kd, yb, sout, s_ref = refs
    c = pl.program_id(1)

    @pl.when(c == 0)
    def _():
        if has_s0:
            s_ref[...] = s0[...]
        else:
            s_ref[...] = jnp.zeros_like(s_ref)

    if rows < lp:
        for pad, src in ((qp, rq), (kp, rk), (vp, rv), (gp, rg)):
            pad[...] = jnp.zeros_like(pad)
            pad[0:rows, :] = src[...]
        rq, rk, rv, rg = qp, kp, vp, gp

    scale = RET_QK ** -0.5
    for ci in range(cpb):
        rs = slice(ci * lp, (ci + 1) * lp)
        cos2 = cos[rs, :]
        sin2 = sin[rs, :]
        for h in range(N_HEADS):
            qs = slice(h * RET_QK, (h + 1) * RET_QK)
            vs = slice(h * RET_V, (h + 1) * RET_V)
            q = _rot(rq[rs, qs], cos2, sin2) * scale
            k = _rot(rk[rs, qs], cos2, sin2)
            v = rv[rs, vs].astype(BF16)
            sc = _dot_nt(q.astype(BF16), k.astype(BF16)) * dm[h]
            s_h = s_ref[h]
            o = _dot(sc.astype(BF16), v) + _dot((q * qd[h]).astype(BF16), s_h.astype(BF16))
            s_ref[h] = s_h * cdec[h] + _dot_tn((k * kd[h]).astype(BF16), v)
            o = o * lax.rsqrt(jnp.mean(o * o, axis=-1, keepdims=True) + EPS)
            y = o * _silu(rg[rs, vs])
            yb[ci * rows:(ci + 1) * rows, vs] = y[0:rows, :].astype(BF16)

    @pl.when(c == n_steps - 1)
    def _():
        sout[...] = s_ref[...]


def _retention(proj3, cos, sin, s0, l):
    b, t, _ = proj3.shape
    rows = math.gcd(t, RET_CHUNK)
    lp = max(rows, 16)
    n_chunks = t // rows
    dmask, qdec, kdec, cdec = _ret_decay_tables(rows, lp)
    has_s0 = s0 is not None
    assert has_s0 == (rows < lp)
    cpb = 4 if (rows == lp and n_chunks % 4 == 0) else 1
    n_steps = n_chunks // cpb
    br = rows * cpb
    in_specs = [pl.BlockSpec((None, br, HW), lambda i, c: (i, c, OFF_RQ // HW)),
                pl.BlockSpec((None, br, HW), lambda i, c: (i, c, OFF_RK // HW)),
                pl.BlockSpec((None, br, RV_W), lambda i, c: (i, c, OFF_RV // RV_W)),
                pl.BlockSpec((None, br, RV_W), lambda i, c: (i, c, OFF_RG // RV_W)),
                pl.BlockSpec((lp * cpb, HEAD_DIM), lambda i, c: (c, 0)),
                pl.BlockSpec((lp * cpb, HEAD_DIM), lambda i, c: (c, 0)),
                pl.BlockSpec((N_HEADS, lp, lp), lambda i, c: (0, 0, 0)),
                pl.BlockSpec((N_HEADS, lp, HEAD_DIM), lambda i, c: (0, 0, 0)),
                pl.BlockSpec((N_HEADS, lp, HEAD_DIM), lambda i, c: (0, 0, 0))]
    args = [proj3, proj3, proj3, proj3, cos, sin, dmask, qdec, kdec]
    scratch = [pltpu.VMEM((N_HEADS, RET_QK, RET_V), F32)]
    if has_s0:
        in_specs.append(pl.BlockSpec((None, None, N_HEADS, RET_QK, RET_V), lambda i, c: (l, i, 0, 0, 0)))
        args.append(s0)
        scratch += [pltpu.VMEM((lp, HW), F32), pltpu.VMEM((lp, HW), F32),
                    pltpu.VMEM((lp, RV_W), F32), pltpu.VMEM((lp, RV_W), F32)]
    return pl.pallas_call(
        functools.partial(_ret_kernel, rows=rows, lp=lp, has_s0=has_s0, cdec=cdec, n_steps=n_steps, cpb=cpb),
        grid=(b, n_steps),
        in_specs=in_specs,
        out_specs=[pl.BlockSpec((None, br, RV_W), lambda i, c: (i, c, 0)),
                   pl.BlockSpec((None, N_HEADS, RET_QK, RET_V), lambda i, c: (i, 0, 0, 0))],
        out_shape=[jax.ShapeDtypeStruct((b, t, RV_W), BF16),
                   jax.ShapeDtypeStruct((b, N_HEADS, RET_QK, RET_V), F32)],
        scratch_shapes=scratch,
        compiler_params=_params("parallel", "arbitrary"),
        name="retention",
    )(*args)


ATTN_ROWS = 2048
ROPE_CHUNK = 256


def _rows_at(start, size, stride):
    return pl.ds(start, size, stride=stride) if stride > 1 else pl.ds(start, size)


def _softmax_pv(q, keys, vals, valid):
    s = jnp.where(valid, _dot_nt(q, keys), NEG)
    m = jnp.max(s, axis=-1, keepdims=True)
    p = jnp.exp(s - m)
    den = jnp.sum(p, axis=-1, keepdims=True)
    return _dot((p * (1.0 / den)).astype(BF16), vals), m + jnp.log(den)


def _attn_p_kernel(*refs):
    ng = len(DIL_GROUPS)
    qkv = refs[0:3 * ng]
    yc_ref = refs[3 * ng]
    scr = refs[3 * ng + 1:]
    o_sc, lse_sc, kps, vps = [scr[n * ng:(n + 1) * ng] for n in range(4)]
    qrot, krs, vs = qkv[0::3], qkv[1::3], qkv[2::3]
    i = pl.program_id(2)
    rows, blk = ATTN_ROWS, ATTN_BLOCK

    @pl.when(i == 0)
    def _():
        for g in range(ng):
            kps[g][...] = jnp.zeros_like(kps[g])
            vps[g][...] = jnp.zeros_like(vps[g])

    qi = lax.broadcasted_iota(jnp.int32, (blk, 2 * blk), 0)
    ki = lax.broadcasted_iota(jnp.int32, (blk, 2 * blk), 1)
    back = blk + qi - ki
    band = (back >= 0) & (back <= blk)
    band_first = band & (ki >= jnp.where(i > 0, 0, blk))

    for g, (window, dil) in enumerate(DIL_GROUPS):
        assert window // dil == blk
        n_sub = rows // (dil * blk)

        for r in range(dil):
            for jb in range(n_sub):
                at = lambda j, n: _rows_at(r + j * blk * dil, n, dil)
                q = qrot[g][at(jb, blk), :].astype(BF16)
                if jb == 0:
                    keys = jnp.concatenate([kps[g][at(0, blk), :], krs[g][at(0, blk), :]], axis=0)
                    vals = jnp.concatenate([vps[g][at(0, blk), :], vs[g][at(0, blk), :]], axis=0)
                    valid = band_first
                else:
                    keys = krs[g][at(jb - 1, 2 * blk), :]
                    vals = vs[g][at(jb - 1, 2 * blk), :]
                    valid = band
                o, lse = _softmax_pv(q, keys.astype(BF16), vals.astype(BF16), valid)
                o_sc[g][at(jb, blk), :] = o
                lse_sc[g][at(jb, blk), :] = jnp.broadcast_to(lse, (blk, HEAD_DIM))

    for g, (window, dil) in enumerate(DIL_GROUPS):
        keep = dil * blk
        kps[g][...] = krs[g][rows - keep:, :]
        vps[g][...] = vs[g][rows - keep:, :]

    for c in range(0, rows, ROPE_CHUNK):
        rs = slice(c, c + ROPE_CHUNK)
        ls = [lse_sc[g][rs, :] for g in range(ng)]
        m = functools.reduce(jnp.maximum, ls)
        es = [jnp.exp(a - m) for a in ls]
        den = functools.reduce(lambda a, e: a + e, es)
        y = functools.reduce(lambda a, e: a + e, [(es[g] / den) * o_sc[g][rs, :] for g in range(ng)])
        yc_ref[rs, :] = y.astype(BF16)


def _attn_prompt(qkv_h, b, t):
    ng = len(DIL_GROUPS)
    m = b * t
    rows = ATTN_ROWS
    nblk = t // rows
    assert nblk * rows == t

    def head(n):
        return pl.BlockSpec((None, rows, HEAD_DIM), lambda i, h, j: (n * N_HEADS + h, i * nblk + j, 0))

    keeps = [dil * ATTN_BLOCK for _, dil in DIL_GROUPS]
    scratch = ([pltpu.VMEM((rows, HEAD_DIM), F32)] * (2 * ng)
               + [pltpu.VMEM((k, HEAD_DIM), F32) for k in keeps] * 2)
    return pl.pallas_call(
        _attn_p_kernel,
        grid=(b, N_HEADS, nblk),
        in_specs=[head(n) for n in range(3 * ng)],
        out_specs=pl.BlockSpec((rows, HEAD_DIM), lambda i, h, j: (i * nblk + j, h)),
        out_shape=jax.ShapeDtypeStruct((m, HW), BF16),
        scratch_shapes=scratch,
        compiler_params=_params("parallel", "parallel", "arbitrary"),
        name="attn_prompt",
    )(*([qkv_h] * (3 * ng)))


def _attn_s_kernel(*refs, t, lp, cache_rows):
    ng = len(DIL_GROUPS)
    qkv = refs[0:3 * ng]
    caches = refs[3 * ng:4 * ng]
    yc_ref = refs[4 * ng]
    knew = refs[4 * ng + 1:5 * ng + 1]
    qp, kp, vp = refs[5 * ng + 1:5 * ng + 4]
    qi = lax.broadcasted_iota(jnp.int32, (lp, lp), 0)
    kn_i = lax.broadcasted_iota(jnp.int32, (lp, lp), 1)
    outs = [[None] * ng for _ in range(N_HEADS)]
    lses = [[None] * ng for _ in range(N_HEADS)]
    for g, (window, dil) in enumerate(DIL_GROUPS):
        lb = cache_rows[g]
        for pad, src in ((qp, qkv[3 * g]), (kp, qkv[3 * g + 1]), (vp, qkv[3 * g + 2])):
            pad[...] = jnp.zeros_like(pad)
            pad[0:t, :] = src[...]
        cache = caches[g]
        qc = lax.broadcasted_iota(jnp.int32, (lp, lb), 0)
        kc_i = lax.broadcasted_iota(jnp.int32, (lp, lb), 1)
        back_c = lb + qc - kc_i
        valid_c = ((back_c & (dil - 1)) == 0) & (back_c <= window)
        back_n = qi - kn_i
        valid_n = (back_n >= 0) & ((back_n & (dil - 1)) == 0) & (back_n <= window) & (kn_i < t)
        for h in range(N_HEADS):
            hs = slice(h * HEAD_DIM, (h + 1) * HEAD_DIM)
            q = qp[:, hs].astype(BF16)
            k_new = kp[:, hs]
            v_new = vp[:, hs]
            for tt in range(t):
                knew[g][tt * KV_ROWS + h:tt * KV_ROWS + h + 1, :] = k_new[tt:tt + 1, :]
                knew[g][tt * KV_ROWS + N_HEADS + h:tt * KV_ROWS + N_HEADS + h + 1, :] = v_new[tt:tt + 1, :]
            k_old = cache[pl.ds(h, lb, stride=KV_ROWS), :].astype(BF16)
            v_old = cache[pl.ds(N_HEADS + h, lb, stride=KV_ROWS), :].astype(BF16)
            s_c = jnp.where(valid_c, _dot_nt(q, k_old), NEG)
            s_n = jnp.where(valid_n, _dot_nt(q, k_new.astype(BF16)), NEG)
            m = jnp.maximum(jnp.max(s_c, axis=-1, keepdims=True), jnp.max(s_n, axis=-1, keepdims=True))
            p_c = jnp.exp(s_c - m)
            p_n = jnp.exp(s_n - m)
            den = jnp.sum(p_c, axis=-1, keepdims=True) + jnp.sum(p_n, axis=-1, keepdims=True)
            o = (_dot((p_c / den).astype(BF16), v_old)
                 + _dot((p_n / den).astype(BF16), v_new.astype(BF16)))
            outs[h][g] = o
            lses[h][g] = m + jnp.log(den)
    for h in range(N_HEADS):
        hs = slice(h * HEAD_DIM, (h + 1) * HEAD_DIM)
        m = functools.reduce(jnp.maximum, lses[h])
        es = [jnp.exp(a - m) for a in lses[h]]
        den = functools.reduce(lambda a, c: a + c, es)
        y = functools.reduce(lambda a, c: a + c, [(e / den) * o for e, o in zip(es, outs[h])])
        yc_ref[:, hs] = y[0:t, :].astype(BF16)


def _attn_sample(proj3, caches, l, lp):
    b, t, _ = proj3.shape
    ng = len(DIL_GROUPS)
    cache_rows = tuple(c.shape[2] // KV_ROWS for c in caches)
    tok = lambda c0: pl.BlockSpec((None, t, HW), lambda i: (i, 0, c0))
    in_specs = [tok(n) for n in range(3 * ng)]
    in_specs += [pl.BlockSpec((None, None, lb * KV_ROWS, HEAD_DIM), lambda i: (l, i, 0, 0)) for lb in cache_rows]
    res = pl.pallas_call(
        functools.partial(_attn_s_kernel, t=t, lp=lp, cache_rows=cache_rows),
        grid=(b,),
        in_specs=in_specs,
        out_specs=[pl.BlockSpec((None, t, HW), lambda i: (i, 0, 0))]
                  + [pl.BlockSpec((None, t * KV_ROWS, HEAD_DIM), lambda i: (i, 0, 0))] * ng,
        out_shape=[jax.ShapeDtypeStruct((b, t, HW), BF16)]
                  + [jax.ShapeDtypeStruct((b, t * KV_ROWS, HEAD_DIM), F32)] * ng,
        scratch_shapes=[pltpu.VMEM((lp, HW), F32)] * 3,
        compiler_params=_params("parallel"),
        name="attn_sample",
    )(*([proj3] * (3 * ng)), *caches)
    return res[0], res[1:]


def _cache_shift_kernel(cur_ref, nxt_ref, new_ref, o_ref, *, ch, shift, n_chunks):
    c = pl.program_id(2)
    o_ref[0:ch - shift, :] = cur_ref[shift:ch, :]
    o_ref[ch - shift:ch, :] = jnp.where(c == n_chunks - 1, new_ref[...], nxt_ref[...])


def _cache_shift(cache, new):
    depth, b, rows, d = cache.shape
    shift = new.shape[2]
    ch = min(rows, 8192)
    n_chunks = rows // ch
    assert n_chunks * ch == rows and ch % shift == 0 and shift % 8 == 0
    per = ch // shift
    last = rows // shift - 1
    return pl.pallas_call(
        functools.partial(_cache_shift_kernel, ch=ch, shift=shift, n_chunks=n_chunks),
        grid=(depth, b, n_chunks),
        in_specs=[pl.BlockSpec((None, None, ch, d), lambda l, i, c: (l, i, c, 0)),
                  pl.BlockSpec((None, None, shift, d), lambda l, i, c: (l, i, jnp.minimum((c + 1) * per, last), 0)),
                  pl.BlockSpec((None, None, shift, d), lambda l, i, c: (l, i, 0, 0))],
        out_specs=pl.BlockSpec((None, None, ch, d), lambda l, i, c: (l, i, c, 0)),
        out_shape=jax.ShapeDtypeStruct(cache.shape, cache.dtype),
        compiler_params=_params("parallel", "parallel", "arbitrary"),
        name="cache_shift",
    )(cache, cache, new)


def _win_pack_kernel(k_ref, v_ref, o_ref, *, rows):
    for h in range(N_HEADS):
        o_ref[pl.ds(h, rows, stride=KV_ROWS), :] = k_ref[h]
        o_ref[pl.ds(N_HEADS + h, rows, stride=KV_ROWS), :] = v_ref[h]


def _win_pack(qkv_h, g, b, t, keep):
    ch = min(keep, 512)
    first = (t - keep) // ch
    per_b = t // ch
    k_blk, v_blk = 3 * g + 1, 3 * g + 2
    return pl.pallas_call(
        functools.partial(_win_pack_kernel, rows=ch),
        grid=(b, keep // ch),
        in_specs=[pl.BlockSpec((N_HEADS, ch, HEAD_DIM), lambda i, c: (k_blk, i * per_b + first + c, 0)),
                  pl.BlockSpec((N_HEADS, ch, HEAD_DIM), lambda i, c: (v_blk, i * per_b + first + c, 0))],
        out_specs=pl.BlockSpec((None, ch * KV_ROWS, HEAD_DIM), lambda i, c: (i, c, 0)),
        out_shape=jax.ShapeDtypeStruct((b, keep * KV_ROWS, HEAD_DIM), F32),
        compiler_params=_params("parallel", "parallel"),
        name="win_pack",
    )(qkv_h, qkv_h)


def _merge_kernel(h_ref, ya, yb, yc, hs_ref, ya_s, yb_s, yc_s, wga, wgb, wgc, wa, wb, wc, o_ref, os_ref, wg_ref):
    tn = o_ref.shape[1]
    for n, src in enumerate((wga, wgb, wgc)):
        wg_ref[:, n * tn:(n + 1) * tn] = src[...].astype(BF16)

    def merged(h, a, b, c):
        gates = jax.nn.sigmoid(_dot(h[...], wg_ref[...]))
        acc = gates[:, 0:tn] * _dot(a[...], wa[...])
        acc = acc + gates[:, tn:2 * tn] * _dot(b[...], wb[...])
        acc = acc + gates[:, 2 * tn:3 * tn] * _dot(c[...], wc[...])
        return acc.astype(BF16)

    o_ref[...] = merged(h_ref, ya, yb, yc)

    @pl.when(pl.program_id(0) == 0)
    def _():
        os_ref[0] = merged(hs_ref, ya_s, yb_s, yc_s)

    @pl.when(pl.program_id(0) > 0)
    def _():
        os_ref[...] = jnp.zeros_like(os_ref)


def _merge(h, ya, yb, yc, hs, ya_s, yb_s, yc_s, w_in, wa, wb, wc, l, tm, tn):
    m = h.shape[0]
    ms = hs.shape[0]
    act = lambda w: pl.BlockSpec((tm, w), lambda i, j: (i, 0))
    act_s = lambda w: pl.BlockSpec((ms, w), lambda i, j: (0, 0))
    wgt = lambda k: pl.BlockSpec((None, k, tn), lambda i, j: (l, 0, j))
    wgate = lambda n: pl.BlockSpec((None, D_MODEL, tn), lambda i, j: (l, 0, (OFF_GATE + n * D_MODEL) // tn + j))
    res = pl.pallas_call(
        _merge_kernel,
        grid=(m // tm, D_MODEL // tn),
        in_specs=[act(D_MODEL), act(CONV_DIM), act(RV_W), act(HW),
                  act_s(D_MODEL), act_s(CONV_DIM), act_s(RV_W), act_s(HW),
                  wgate(0), wgate(1), wgate(2), wgt(CONV_DIM), wgt(RV_W), wgt(HW)],
        out_specs=[pl.BlockSpec((tm, tn), lambda i, j: (i, j)),
                   pl.BlockSpec((1, ms, tn), lambda i, j: (i, 0, j))],
        out_shape=[jax.ShapeDtypeStruct((m, D_MODEL), BF16),
                   jax.ShapeDtypeStruct((m // tm, ms, D_MODEL), BF16)],
        scratch_shapes=[pltpu.VMEM((D_MODEL, 3 * tn), BF16)],
        compiler_params=_params("parallel", "parallel"),
        name="merge",
    )(h, ya, yb, yc, hs, ya_s, yb_s, yc_s, w_in, w_in, w_in, wa, wb, wc)
    return res[0], res[1][0]


def _post_norm_residual(x, y, gpost, gate):
    r = y * lax.rsqrt(jnp.mean(y * y, axis=-1, keepdims=True) + EPS) * gpost
    return x + gate * r


def _mod_rows(ref, rs):
    v = ref[0]
    return v if v.shape[0] == 1 else v[rs, :]


def _outproj_kernel(m_ref, ms_ref, w_ref, x_ref, xs_ref, gpost_ref, gffn_ref, g1_ref, sc2_ref, sh2_ref,
                    g1s_ref, sc2s_ref, sh2s_ref, xo_ref, h2_ref, xso_ref, h2s_ref, *, sub):
    def finish(rows_m, rows_x, g1, sc2, sh2):
        y = _dot(rows_m, w_ref[...])
        xn = _post_norm_residual(rows_x, y, gpost_ref[0], g1)
        return xn, _norm_mod(xn, gffn_ref[0], sc2, sh2).astype(BF16)

    tm = m_ref.shape[0]
    for r in range(0, tm, sub):
        rs = slice(r, r + sub)
        xo_ref[rs, :], h2_ref[rs, :] = finish(m_ref[rs, :], x_ref[rs, :], _mod_rows(g1_ref, rs),
                                              _mod_rows(sc2_ref, rs), _mod_rows(sh2_ref, rs))

    @pl.when(pl.program_id(0) == 0)
    def _():
        xso_ref[...], h2s_ref[...] = finish(ms_ref[...], xs_ref[...], g1s_ref[0], sc2s_ref[0], sh2s_ref[0])


def _outproj(merged, merged_s, w_out, l, x, x_s, gpost, gffn, mods, mods_s, rows):
    m, d = x.shape
    ms = x_s.shape[0]
    tm = min(rows.tm, 512)
    sub = _Rows(m, tm, rows.tiles_per_b * rows.tm)
    row = pl.BlockSpec((tm, d), lambda i: (i, 0))
    row_s = pl.BlockSpec((ms, d), lambda i: (0, 0))
    mod_s = pl.BlockSpec((1, ms, d), lambda i: (0, 0, 0))
    vec = pl.BlockSpec((1, 1, d), lambda i: (l, 0, 0))
    return pl.pallas_call(
        functools.partial(_outproj_kernel, sub=min(tm, 128)),
        grid=(m // tm,),
        in_specs=[row, row_s, pl.BlockSpec((None, d, d), lambda i: (l, 0, 0), pipeline_mode=pl.Buffered(1)),
                  row, row_s, vec, vec, sub.mod_spec(), sub.mod_spec(), sub.mod_spec(), mod_s, mod_s, mod_s],
        out_specs=[row, row, row_s, row_s],
        out_shape=[jax.ShapeDtypeStruct((m, d), F32), jax.ShapeDtypeStruct((m, d), BF16),
                   jax.ShapeDtypeStruct((ms, d), F32), jax.ShapeDtypeStruct((ms, d), BF16)],
        compiler_params=_params("arbitrary"),
        name="outproj",
    )(merged, merged_s, w_out, x, x_s, gpost, gffn, *mods, *mods_s)


def _down_kernel(*refs, nk, with_next, sub):
    n_in = 13 if with_next else 8
    n_out = 4 if with_next else 2
    ins, outs, (acc_ref, accs_ref) = refs[:n_in], refs[n_in:n_in + n_out], refs[n_in + n_out:]
    a_ref, as_ref, w_ref, x_ref, xs_ref, gpost_ref, g2_ref, g2s_ref = ins[:8]
    i = pl.program_id(0)
    k = pl.program_id(1)

    def finish(x_rows, y, g2_rows, pos, rs):
        xn = _post_norm_residual(x_rows, y, gpost_ref[0], g2_rows)
        outs[pos][rs, :] = xn
        if with_next:
            gn_ref, scn_ref, shn_ref = ins[8], ins[9 + pos], ins[11 + pos]
            outs[2 + pos][rs, :] = _norm_mod(xn, gn_ref[0], _mod_rows(scn_ref, rs), _mod_rows(shn_ref, rs)).astype(BF16)

    @pl.when(k == 0)
    def _():
        acc_ref[...] = jnp.zeros_like(acc_ref)

    @pl.when(k < nk - 1)
    def _():
        acc_ref[...] += _dot(a_ref[...], w_ref[...])

    @pl.when(k == nk - 1)
    def _():
        for r in range(0, a_ref.shape[0], sub):
            rs = slice(r, r + sub)
            y = acc_ref[rs, :] + _dot(a_ref[rs, :], w_ref[...])
            finish(x_ref[rs, :], y, _mod_rows(g2_ref, rs), 0, rs)

    @pl.when(i == 0)
    def _():
        @pl.when(k == 0)
        def _():
            accs_ref[...] = jnp.zeros_like(accs_ref)

        accs_ref[...] += _dot(as_ref[...], w_ref[...])

        @pl.when(k == nk - 1)
        def _():
            finish(xs_ref[...], accs_ref[...], g2s_ref[0], 1, slice(None))


def _down(a, a_s, w_down, l, x, x_s, gpost, g2, g2_s, nxt, rows):
    m, d = x.shape
    ms = x_s.shape[0]
    kdim = a.shape[1]
    tm = min(rows.tm, 1024)
    tk = 1024
    nk = kdim // tk
    sub = _Rows(m, tm, None if rows.per_row else rows.tiles_per_b * rows.tm)
    one = lambda shape, imap: pl.BlockSpec(shape, imap, pipeline_mode=pl.Buffered(1))
    row = one((tm, d), lambda i, k: (i, 0))
    row_s = pl.BlockSpec((ms, d), lambda i, k: (0, 0))
    mod_s = pl.BlockSpec((1, ms, d), lambda i, k: (0, 0, 0))
    vec = pl.BlockSpec((1, 1, d), lambda i, k: (l, 0, 0))
    in_specs = [pl.BlockSpec((tm, tk), lambda i, k: (i, k)),
                pl.BlockSpec((ms, tk), lambda i, k: (0, k)),
                pl.BlockSpec((None, tk, d), lambda i, k: (l, k, 0)),
                row, row_s, vec, sub.mod_spec(), mod_s]
    args = [a, a_s, w_down, x, x_s, gpost, g2, g2_s]
    out_specs = [row, row_s]
    out_shape = [jax.ShapeDtypeStruct((m, d), F32), jax.ShapeDtypeStruct((ms, d), F32)]
    if nxt is not None:
        gn, scn, shn, scn_s, shn_s = nxt
        in_specs += [pl.BlockSpec((1, 1, d), lambda i, k: (l + 1, 0, 0)), sub.mod_spec(), mod_s, sub.mod_spec(), mod_s]
        args += [gn, scn, scn_s, shn, shn_s]
        out_specs += [row, row_s]
        out_shape += [jax.ShapeDtypeStruct((m, d), BF16), jax.ShapeDtypeStruct((ms, d), BF16)]
    return pl.pallas_call(
        functools.partial(_down_kernel, nk=nk, with_next=nxt is not None, sub=min(tm, 256)),
        grid=(m // tm, nk),
        in_specs=in_specs,
        out_specs=out_specs,
        out_shape=out_shape,
        scratch_shapes=[pltpu.VMEM((tm, d), F32), pltpu.VMEM((ms, d), F32)],
        compiler_params=_params("arbitrary", "arbitrary", vmem=VMEM_LIMIT_MAX),
        name="mlp_down",
    )(*args)


def kernel(x_prompt, x_sample, c_prompt, c_sample, state_conv, state_retention, cache_win128, cache_win512, cache_win2048, norm_mix_g, norm_mix_post_g, norm_ffn_g, norm_ffn_post_g, w_mod, b_mod, w_in, conv_w, conv_b, conv_ln_g, conv_ln_b, w_branch_a, w_branch_b, w_branch_c, w_out, w_up, w_down):
    depth = w_in.shape[0]
    bp, tp, d = x_prompt.shape
    bs, ts, _ = x_sample.shape
    mp, ms = bp * tp, bs * ts
    ng = len(DIL_GROUPS)

    w_a_b, w_b_b, w_c_b, w_out_b, w_down_b = [
        w.astype(BF16) for w in (w_branch_a, w_branch_b, w_branch_c, w_out, w_down)]
    conv_w8 = jnp.broadcast_to(conv_w[:, :, None, :], (depth, CONV_WIDTH, SUBLANES, CONV_DIM))
    vec3 = lambda a: a.reshape(depth, 1, a.shape[-1])
    g_mix, g_mix_post, g_ffn, g_ffn_post = map(vec3, (norm_mix_g, norm_mix_post_g, norm_ffn_g, norm_ffn_post_g))
    conv_b3, ln_g3, ln_b3 = map(vec3, (conv_b, conv_ln_g, conv_ln_b))
    caches = [c.reshape(c.shape[0], c.shape[1], c.shape[2] * KV_ROWS, HEAD_DIM)
              for c in (cache_win128, cache_win512, cache_win2048)]

    lp_s = 16
    qkv_rope = _qkv_rope_tables(np.arange(tp)) + _qkv_rope_tables(np.tile(PAST_LEN + np.arange(ts), bs))
    retn_p = _rope_tables(np.arange(tp), _retnet_inv_freq(), tp)
    retn_s = _rope_tables(PAST_LEN + np.arange(ts), _retnet_inv_freq(), lp_s)

    n_c = bp + bs
    c_rows = -(-n_c // 8) * 8
    c_all = jnp.concatenate([c_prompt, c_sample, jnp.zeros((c_rows - n_c, d), F32)], axis=0)
    mod = _modulation(c_all, w_mod, b_mod)

    rows_p = _Rows(mp, 1024, tp)
    rows_s = _Rows(ms, ms, None)

    def mods(l):
        parts_p = [mod[l, :bp, i * d:(i + 1) * d].reshape(bp, 1, d) for i in range(6)]
        parts_s = [jnp.repeat(mod[l, bp:n_c, i * d:(i + 1) * d], ts, axis=0).reshape(1, ms, d) for i in range(6)]
        return parts_p, parts_s

    xp = x_prompt.reshape(mp, d)
    xs = x_sample.reshape(ms, d)
    mod_p, mod_s = mods(0)
    hp = _prenorm(xp, g_mix, 0, mod_p[1], mod_p[0], rows_p)
    hs = _prenorm(xs, g_mix, 0, mod_s[1], mod_s[0], rows_s)

    conv_p, conv_s, ret_p, ret_s = [], [], [], []
    win_p = [[] for _ in range(ng)]
    new_kv_s = [[] for _ in range(ng)]

    for l in range(depth):
        sh1p, sc1p, g1p, sh2p, sc2p, g2p = mod_p
        sh1s, sc1s, g1s, sh2s, sc2s, g2s = mod_s

        proj_ab, proj_ab_s = _matmul(hp, hs, w_in, l, F32, 2048, 512, ncols=OFF_DIL, name="matmul_in_ab")
        qkv_h, qkv_s = _matmul(hp, hs, w_in, l, F32, 2048, HW, col0=OFF_DIL, ncols=OFF_GATE - OFF_DIL,
                               heads=True, rope=qkv_rope, name="matmul_in_qkv")

        proj3 = proj_ab.reshape(bp, tp, OFF_DIL)
        ya, st = _conv_prompt(proj3, l, conv_w8, conv_b3, ln_g3, ln_b3)
        conv_p.append(st[:, CONV_HALO - (CONV_WIDTH - 1):])
        yb, s_new = _retention(proj3, retn_p[0], retn_p[1], None, l)
        ret_p.append(s_new)
        yc = _attn_prompt(qkv_h, bp, tp)
        for g, (window, dil) in enumerate(DIL_GROUPS):
            keep = min(window, tp)
            win_p[g].append(_win_pack(qkv_h, g, bp, tp, keep))
        proj_s3 = proj_ab_s.reshape(bs, ts, OFF_DIL)
        ya_s, st_s = _conv_sample(proj_s3, l, state_conv, conv_w, conv_b3, ln_g3, ln_b3)
        conv_s.append(st_s)
        yb_s, s_new_s = _retention(proj_s3, retn_s[0], retn_s[1], state_retention, l)
        ret_s.append(s_new_s)
        yc_s, kv_new = _attn_sample(qkv_s.reshape(bs, ts, OFF_GATE - OFF_DIL), caches, l, lp_s)
        for g in range(ng):
            new_kv_s[g].append(kv_new[g])
        merged, merged_s = _merge(hp, ya.reshape(mp, CONV_DIM), yb.reshape(mp, RV_W), yc,
                                  hs, ya_s.reshape(ms, CONV_DIM), yb_s.reshape(ms, RV_W), yc_s.reshape(ms, HW),
                                  w_in, w_a_b, w_b_b, w_c_b, l, 1024, 256)
        xp, h2, xs, h2_s = _outproj(merged, merged_s, w_out_b, l, xp, xs, g_mix_post, g_ffn,
                                    (g1p, sc2p, sh2p), (g1s, sc2s, sh2s), rows_p)
        a, a_s = _matmul(h2, h2_s, w_up, l, BF16, 2048, 1024, act="relu2", name="mlp_up")
        if l + 1 < depth:
            mod_p_n, mod_s_n = mods(l + 1)
            nxt = (g_mix, mod_p_n[1], mod_p_n[0], mod_s_n[1], mod_s_n[0])
            xp, xs, hp, hs = _down(a, a_s, w_down_b, l, xp, xs, g_ffn_post, g2p, g2s, nxt, rows_p)
            mod_p, mod_s = mod_p_n, mod_s_n
        else:
            xp, xs = _down(a, a_s, w_down_b, l, xp, xs, g_ffn_post, g2p, g2s, None, rows_p)

    win_s = [_cache_shift(caches[g], jnp.stack(new_kv_s[g])) for g in range(ng)]
    kv5 = lambda a: a.reshape(a.shape[0], a.shape[1], a.shape[2] // KV_ROWS, 2, N_HEADS, HEAD_DIM)
    win_p = [kv5(jnp.stack(w)) for w in win_p]
    win_s = [kv5(w) for w in win_s]

    return (xp.reshape(bp, tp, d), xs.reshape(bs, ts, d),
            jnp.stack(conv_p), jnp.stack(conv_s), jnp.stack(ret_p), jnp.stack(ret_s),
            win_p[0], win_s[0], win_p[1], win_s[1], win_p[2], win_s[2])
```

```python
import functools
import math

import numpy as np
import jax
import jax.numpy as jnp
from jax import lax
from jax.experimental import pallas as pl
from jax.experimental.pallas import tpu as pltpu

F32 = jnp.float32
BF16 = jnp.bfloat16

D_MODEL = 2048
PAST_LEN = 16384
HEAD_DIM = 128
CONV_DIM = D_MODEL // 4
CONV_WIDTH = 31
CONV_HALO = 32
N_HEADS = D_MODEL // 512
RET_QK = 128
RET_V = 256
RET_CHUNK = 128
DIL_GROUPS = ((128, 1), (512, 4), (2048, 16))
ATTN_BLOCK = 128
D_FF = 4 * D_MODEL
ROPE_THETA = 10000.0
EPS = 1e-6
HW = N_HEADS * HEAD_DIM
RV_W = N_HEADS * RET_V
KV_ROWS = 2 * N_HEADS
SUBLANES = 8

OFF_CONV_A = 0
OFF_CONV_B = CONV_DIM
OFF_RQ = 2 * CONV_DIM
OFF_RK = OFF_RQ + HW
OFF_RV = OFF_RK + HW
OFF_RG = OFF_RV + RV_W
OFF_DIL = OFF_RG + RV_W
OFF_GATE = OFF_DIL + 3 * len(DIL_GROUPS) * HW
N_IN = OFF_GATE + 3 * D_MODEL

NEG = -1e30
VMEM_LIMIT = 52 * 1024 * 1024
VMEM_LIMIT_MAX = 58 * 1024 * 1024


def _params(*sem, vmem=VMEM_LIMIT):
    return pltpu.CompilerParams(dimension_semantics=sem, vmem_limit_bytes=vmem)


def _silu(x):
    return x * jax.nn.sigmoid(x)


def _rot(x, cos2, sin2):
    return x * cos2 + pltpu.roll(x, HEAD_DIM // 2, axis=1) * sin2


def _dot(a, b):
    return jnp.dot(a, b, preferred_element_type=F32)


def _dot_nt(a, b):
    return lax.dot_general(a, b, (((1,), (1,)), ((), ())), preferred_element_type=F32)


def _dot_tn(a, b):
    return lax.dot_general(a, b, (((0,), (0,)), ((), ())), preferred_element_type=F32)


def _rope_tables(pos, inv_freq, rows):
    ang = np.asarray(pos, np.float64)[:, None] * inv_freq[None, :]
    cos = np.concatenate([np.cos(ang), np.cos(ang)], axis=-1)
    sin = np.concatenate([-np.sin(ang), np.sin(ang)], axis=-1)
    pad = rows - cos.shape[0]
    if pad:
        cos = np.pad(cos, ((0, pad), (0, 0)))
        sin = np.pad(sin, ((0, pad), (0, 0)))
    return jnp.asarray(cos, F32), jnp.asarray(sin, F32)


def _qkv_rope_tables(pos):
    cos, sin = _rope_tables(pos, _rope_inv_freq(), len(pos))
    scale = HEAD_DIM ** -0.5
    return (jnp.stack([cos * scale, cos, jnp.ones_like(cos)]),
            jnp.stack([sin * scale, sin, jnp.zeros_like(sin)]))


def _rope_inv_freq():
    return 1.0 / (ROPE_THETA ** (np.arange(0, HEAD_DIM, 2, dtype=np.float64) / HEAD_DIM))


def _retnet_inv_freq():
    return 1.0 / (ROPE_THETA ** np.linspace(0.0, 1.0, RET_QK // 2, dtype=np.float64))


def _ret_decay_tables(rows, lp):
    lg = np.log(1.0 - 2.0 ** (-5.0 - np.arange(N_HEADS, dtype=np.float64)))
    i = np.arange(lp, dtype=np.float64)
    rel = i[:, None] - i[None, :]
    real = (i < rows)
    dmask = np.where((rel >= 0) & real[:, None] & real[None, :],
                     np.exp(np.maximum(rel, 0.0)[None] * lg[:, None, None]), 0.0)
    qdec = np.where(real[None, :], np.exp((i[None, :] + 1.0) * lg[:, None]), 0.0)
    kdec = np.where(real[None, :], np.exp((rows - 1.0 - i[None, :]) * lg[:, None]), 0.0)
    cdec = tuple(float(v) for v in np.exp(rows * lg))
    bl = lambda a: jnp.asarray(np.broadcast_to(a[:, :, None], (N_HEADS, lp, HEAD_DIM)), F32)
    return jnp.asarray(dmask, F32), bl(qdec), bl(kdec), cdec


class _Rows:
    def __init__(self, m, tm, rows_per_batch):
        self.m = m
        self.tm = tm
        self.per_row = rows_per_batch is None
        self.tiles_per_b = None if self.per_row else rows_per_batch // tm

    def mod_spec(self):
        if self.per_row:
            return pl.BlockSpec((1, self.tm, D_MODEL), lambda i, *_: (0, i, 0))
        tpb = self.tiles_per_b
        return pl.BlockSpec((1, 1, D_MODEL), lambda i, *_: (i // tpb, 0, 0))


def _mod_kernel(c_ref, w_ref, b_ref, o_ref):
    c = c_ref[...]
    s = _silu(c).astype(BF16)
    o_ref[0] = _dot(s, w_ref[0].astype(BF16)) + b_ref[0]


def _modulation(c_all, w_mod, b_mod):
    depth, d, n = w_mod.shape
    rows = c_all.shape[0]
    tn = 1024
    return pl.pallas_call(
        _mod_kernel,
        grid=(depth, n // tn),
        in_specs=[pl.BlockSpec((rows, d), lambda l, j: (0, 0)),
                  pl.BlockSpec((1, d, tn), lambda l, j: (l, 0, j)),
                  pl.BlockSpec((1, 1, tn), lambda l, j: (l, 0, j))],
        out_specs=pl.BlockSpec((1, rows, tn), lambda l, j: (l, 0, j)),
        out_shape=jax.ShapeDtypeStruct((depth, rows, n), F32),
        compiler_params=_params("parallel", "parallel"),
        name="modulation",
    )(c_all, w_mod, b_mod.reshape(depth, 1, n))


def _norm_mod(x, g, sc, sh):
    y = x * lax.rsqrt(jnp.mean(x * x, axis=-1, keepdims=True) + EPS) * g
    return y * (1.0 + sc) + sh


def _prenorm_kernel(x_ref, g_ref, sc_ref, sh_ref, h_ref):
    h_ref[...] = _norm_mod(x_ref[...], g_ref[0], sc_ref[0], sh_ref[0]).astype(BF16)


def _prenorm(x, g, l, sc, sh, rows):
    m, d = x.shape
    tm = rows.tm
    return pl.pallas_call(
        _prenorm_kernel,
        grid=(m // tm,),
        in_specs=[pl.BlockSpec((tm, d), lambda i: (i, 0)),
                  pl.BlockSpec((1, 1, d), lambda i: (l, 0, 0)),
                  rows.mod_spec(), rows.mod_spec()],
        out_specs=pl.BlockSpec((tm, d), lambda i: (i, 0)),
        out_shape=jax.ShapeDtypeStruct((m, d), BF16),
        compiler_params=_params("parallel"),
        name="prenorm",
    )(x, g, sc, sh)


def _mm_kernel(*refs, act, heads, rope):
    if rope:
        x_ref, xs_ref, w_ref, c_ref, s_ref, cs_ref, ss_ref, o_ref, os_ref = refs
    else:
        x_ref, xs_ref, w_ref, o_ref, os_ref = refs

    def product(rows_ref, cos_ref=None, sin_ref=None, rs=slice(None)):
        y = _dot(rows_ref[rs, :], w_ref[...].astype(BF16))
        if act == "relu2":
            y = jnp.square(jnp.maximum(y, 0.0))
        if rope:
            cos2, sin2 = cos_ref[rs, :], sin_ref[rs, :]
            y = jnp.concatenate([_rot(y[:, h * HEAD_DIM:(h + 1) * HEAD_DIM], cos2, sin2)
                                 for h in range(y.shape[1] // HEAD_DIM)], axis=1)
        return y

    tm = x_ref.shape[0]
    step = tm // 4 if rope else tm
    for r in range(0, tm, step):
        rs = slice(r, r + step)
        y = product(x_ref, *((c_ref, s_ref) if rope else ()), rs=rs)
        if heads:
            for h in range(o_ref.shape[0]):
                o_ref[h, rs, :] = y[:, h * HEAD_DIM:(h + 1) * HEAD_DIM]
        else:
            o_ref[rs, :] = y.astype(o_ref.dtype)

    @pl.when(pl.program_id(0) == 0)
    def _():
        os_ref[0] = product(xs_ref, *((cs_ref, ss_ref) if rope else ())).astype(os_ref.dtype)

    @pl.when(pl.program_id(0) > 0)
    def _():
        os_ref[...] = jnp.zeros_like(os_ref)


def _matmul(x, xs, w, l, out_dtype, tm, tn, act=None, col0=0, ncols=None, heads=False, rope=None, name="matmul"):
    m, k = x.shape
    ms = xs.shape[0]
    n = w.shape[-1] if ncols is None else ncols
    c0 = col0 // tn
    assert c0 * tn == col0 and n % tn == 0 and tn % HEAD_DIM == 0
    if heads:
        out_spec = pl.BlockSpec((tn // HEAD_DIM, tm, HEAD_DIM), lambda i, j: (j, i, 0))
        out_shape = jax.ShapeDtypeStruct((n // HEAD_DIM, m, HEAD_DIM), out_dtype)
    else:
        out_spec = pl.BlockSpec((tm, tn), lambda i, j: (i, j))
        out_shape = jax.ShapeDtypeStruct((m, n), out_dtype)
    in_specs = [pl.BlockSpec((tm, k), lambda i, j: (i, 0)),
                pl.BlockSpec((ms, k), lambda i, j: (0, 0)),
                pl.BlockSpec((None, k, tn), lambda i, j: (l, 0, c0 + j))]
    args = [x, xs, w]
    if rope is not None:
        period, t = rope[0].shape[0], rope[0].shape[1]
        per_seq = t // tm
        assert per_seq * tm == t
        tab = pl.BlockSpec((None, tm, HEAD_DIM), lambda i, j: (j % period, i % per_seq, 0))
        tab_s = pl.BlockSpec((None, ms, HEAD_DIM), lambda i, j: (j % period, 0, 0))
        in_specs += [tab, tab, tab_s, tab_s]
        args += list(rope)
    res = pl.pallas_call(
        functools.partial(_mm_kernel, act=act, heads=heads, rope=rope is not None),
        grid=(m // tm, n // tn),
        in_specs=in_specs,
        out_specs=[out_spec, pl.BlockSpec((1, ms, tn), lambda i, j: (i, 0, j))],
        out_shape=[out_shape, jax.ShapeDtypeStruct((m // tm, ms, n), out_dtype)],
        compiler_params=_params("parallel", "parallel"),
        name=name,
    )(*args)
    return res[0], res[1][0]


def _conv_ln_silu(full_ref, first, rows, w_ref, cb, lg, lb):
    acc = jnp.zeros((rows, CONV_DIM), F32)
    for j in range(CONV_WIDTH):
        acc = acc + full_ref[first + j:first + j + rows, :] * w_ref[j:j + 1, :]
    y = acc + cb
    yc = y - jnp.mean(y, axis=-1, keepdims=True)
    z = yc * lax.rsqrt(jnp.mean(yc * yc, axis=-1, keepdims=True) + EPS) * lg + lb
    return _silu(z)


def _conv_p_kernel(a_ref, b_ref, ha_ref, hb_ref, w_ref, cb_ref, lg_ref, lb_ref, ya_ref, st_ref, sh_ref,
                   *, tt, rc):
    t = pl.program_id(1)
    u = a_ref[...] * jax.nn.sigmoid(b_ref[...])
    uh = ha_ref[...] * jax.nn.sigmoid(hb_ref[...])
    sh_ref[0, 0:CONV_HALO, :] = jnp.where(t > 0, uh, 0.0)
    sh_ref[0, CONV_HALO:, :] = u
    st_ref[...] = u[tt - CONV_HALO:, :]
    n_sh = tt + CONV_HALO - SUBLANES
    for s in range(1, SUBLANES):
        sh_ref[s, 0:n_sh, :] = sh_ref[0, s:s + n_sh, :]
    skip = CONV_HALO - (CONV_WIDTH - 1)
    cb, lg, lb = cb_ref[...], lg_ref[...], lb_ref[...]
    for r0 in range(0, tt, rc):
        acc = jnp.zeros((rc, CONV_DIM), F32)
        for j in range(CONV_WIDTH):
            first = r0 + skip + j
            s = first % SUBLANES
            wj = jnp.tile(w_ref[j], (rc // SUBLANES, 1))
            acc = acc + sh_ref[s, first - s:first - s + rc, :] * wj
        y = acc + cb
        yc = y - jnp.mean(y, axis=-1, keepdims=True)
        z = yc * lax.rsqrt(jnp.mean(yc * yc, axis=-1, keepdims=True) + EPS) * lg + lb
        ya_ref[r0:r0 + rc, :] = _silu(z).astype(BF16)


def _conv_prompt(proj3, l, conv_w8, conv_b, ln_g, ln_b):
    b, t, _ = proj3.shape
    tt, rc = 256, 32
    hpt = tt // CONV_HALO
    vec = pl.BlockSpec((None, 1, CONV_DIM), lambda i, j: (l, 0, 0))
    return pl.pallas_call(
        functools.partial(_conv_p_kernel, tt=tt, rc=rc),
        grid=(b, t // tt),
        in_specs=[pl.BlockSpec((None, tt, CONV_DIM), lambda i, j: (i, j, 0)),
                  pl.BlockSpec((None, tt, CONV_DIM), lambda i, j: (i, j, 1)),
                  pl.BlockSpec((None, CONV_HALO, CONV_DIM), lambda i, j: (i, jnp.maximum(j * hpt - 1, 0), 0)),
                  pl.BlockSpec((None, CONV_HALO, CONV_DIM), lambda i, j: (i, jnp.maximum(j * hpt - 1, 0), 1)),
                  pl.BlockSpec((None, CONV_WIDTH, SUBLANES, CONV_DIM), lambda i, j: (l, 0, 0, 0)),
                  vec, vec, vec],
        out_specs=[pl.BlockSpec((None, tt, CONV_DIM), lambda i, j: (i, j, 0)),
                   pl.BlockSpec((None, CONV_HALO, CONV_DIM), lambda i, j: (i, 0, 0))],
        out_shape=[jax.ShapeDtypeStruct((b, t, CONV_DIM), BF16),
                   jax.ShapeDtypeStruct((b, CONV_HALO, CONV_DIM), F32)],
        scratch_shapes=[pltpu.VMEM((SUBLANES, tt + CONV_HALO, CONV_DIM), F32)],
        compiler_params=_params("parallel", "arbitrary"),
        name="conv_prompt",
    )(proj3, proj3, proj3, proj3, conv_w8, conv_b, ln_g, ln_b)


def _conv_s_kernel(a_ref, b_ref, st_in_ref, w_ref, cb_ref, lg_ref, lb_ref, ya_ref, st_ref, full_ref, *, t):
    keep = CONV_WIDTH - 1
    u = a_ref[...] * jax.nn.sigmoid(b_ref[...])
    full_ref[0:keep, :] = st_in_ref[...]
    full_ref[keep:keep + t, :] = u
    st_ref[...] = full_ref[t:t + keep, :]
    y = _conv_ln_silu(full_ref, 0, t, w_ref, cb_ref[...], lg_ref[...], lb_ref[...])
    ya_ref[...] = y.astype(BF16)


def _conv_sample(proj3, l, state_conv, conv_w, conv_b, ln_g, ln_b):
    b, t, _ = proj3.shape
    keep = CONV_WIDTH - 1
    vec = pl.BlockSpec((None, 1, CONV_DIM), lambda i: (l, 0, 0))
    return pl.pallas_call(
        functools.partial(_conv_s_kernel, t=t),
        grid=(b,),
        in_specs=[pl.BlockSpec((None, t, CONV_DIM), lambda i: (i, 0, 0)),
                  pl.BlockSpec((None, t, CONV_DIM), lambda i: (i, 0, 1)),
                  pl.BlockSpec((None, None, keep, CONV_DIM), lambda i: (l, i, 0, 0)),
                  pl.BlockSpec((None, CONV_WIDTH, CONV_DIM), lambda i: (l, 0, 0)),
                  vec, vec, vec],
        out_specs=[pl.BlockSpec((None, t, CONV_DIM), lambda i: (i, 0, 0)),
                   pl.BlockSpec((None, keep, CONV_DIM), lambda i: (i, 0, 0))],
        out_shape=[jax.ShapeDtypeStruct((b, t, CONV_DIM), BF16),
                   jax.ShapeDtypeStruct((b, keep, CONV_DIM), F32)],
        scratch_shapes=[pltpu.VMEM((keep + 8 + t, CONV_DIM), F32)],
        compiler_params=_params("parallel"),
        name="conv_sample",
    )(proj3, proj3, state_conv, conv_w, conv_b, ln_g, ln_b)


def _ret_kernel(*refs, rows, lp, has_s0, cdec, n_steps, cpb):
    if has_s0:
        rq, rk, rv, rg, cos, sin, dm, qd, kd, s0, yb, sout, s_ref, qp, kp, vp, gp = refs
    else:
        rq, rk, rv, rg, cos, sin, dm, qd, kd, yb, sout, s_ref = refs
    c = pl.program_id(1)

    @pl.when(c == 0)
    def _():
        if has_s0:
            s_ref[...] = s0[...]
        else:
            s_ref[...] = jnp.zeros_like(s_ref)

    if rows < lp:
        for pad, src in ((qp, rq), (kp, rk), (vp, rv), (gp, rg)):
            pad[...] = jnp.zeros_like(pad)
            pad[0:rows, :] = src[...]
        rq, rk, rv, rg = qp, kp, vp, gp

    scale = RET_QK ** -0.5
    for ci in range(cpb):
        rs = slice(ci * lp, (ci + 1) * lp)
        cos2 = cos[rs, :]
        sin2 = sin[rs, :]
        for h in range(N_HEADS):
            qs = slice(h * RET_QK, (h + 1) * RET_QK)
            vs = slice(h * RET_V, (h + 1) * RET_V)
            q = _rot(rq[rs, qs], cos2, sin2) * scale
            k = _rot(rk[rs, qs], cos2, sin2)
            v = rv[rs, vs].astype(BF16)
            sc = _dot_nt(q.astype(BF16), k.astype(BF16)) * dm[h]
            s_h = s_ref[h]
            o = _dot(sc.astype(BF16), v) + _dot((q * qd[h]).astype(BF16), s_h.astype(BF16))
            s_ref[h] = s_h * cdec[h] + _dot_tn((k * kd[h]).astype(BF16), v)
            o = o * lax.rsqrt(jnp.mean(o * o, axis=-1, keepdims=True) + EPS)
            y = o * _silu(rg[rs, vs])
            yb[ci * rows:(ci + 1) * rows, vs] = y[0:rows, :].astype(BF16)

    @pl.when(c == n_steps - 1)
    def _():
        sout[...] = s_ref[...]


def _retention(proj3, cos, sin, s0, l):
    b, t, _ = proj3.shape
    rows = math.gcd(t, RET_CHUNK)
    lp = max(rows, 16)
    n_chunks = t // rows
    dmask, qdec, kdec, cdec = _ret_decay_tables(rows, lp)
    has_s0 = s0 is not None
    assert has_s0 == (rows < lp)
    cpb = 4 if (rows == lp and n_chunks % 4 == 0) else 1
    n_steps = n_chunks // cpb
    br = rows * cpb
    in_specs = [pl.BlockSpec((None, br, HW), lambda i, c: (i, c, OFF_RQ // HW)),
                pl.BlockSpec((None, br, HW), lambda i, c: (i, c, OFF_RK // HW)),
                pl.BlockSpec((None, br, RV_W), lambda i, c: (i, c, OFF_RV // RV_W)),
                pl.BlockSpec((None, br, RV_W), lambda i, c: (i, c, OFF_RG // RV_W)),
                pl.BlockSpec((lp * cpb, HEAD_DIM), lambda i, c: (c, 0)),
                pl.BlockSpec((lp * cpb, HEAD_DIM), lambda i, c: (c, 0)),
                pl.BlockSpec((N_HEADS, lp, lp), lambda i, c: (0, 0, 0)),
                pl.BlockSpec((N_HEADS, lp, HEAD_DIM), lambda i, c: (0, 0, 0)),
                pl.BlockSpec((N_HEADS, lp, HEAD_DIM), lambda i, c: (0, 0, 0))]
    args = [proj3, proj3, proj3, proj3, cos, sin, dmask, qdec, kdec]
    scratch = [pltpu.VMEM((N_HEADS, RET_QK, RET_V), F32)]
    if has_s0:
        in_specs.append(pl.BlockSpec((None, None, N_HEADS, RET_QK, RET_V), lambda i, c: (l, i, 0, 0, 0)))
        args.append(s0)
        scratch += [pltpu.VMEM((lp, HW), F32), pltpu.VMEM((lp, HW), F32),
                    pltpu.VMEM((lp, RV_W), F32), pltpu.VMEM((lp, RV_W), F32)]
    return pl.pallas_call(
        functools.partial(_ret_kernel, rows=rows, lp=lp, has_s0=has_s0, cdec=cdec, n_steps=n_steps, cpb=cpb),
        grid=(b, n_steps),
        in_specs=in_specs,
        out_specs=[pl.BlockSpec((None, br, RV_W), lambda i, c: (i, c, 0)),
                   pl.BlockSpec((None, N_HEADS, RET_QK, RET_V), lambda i, c: (i, 0, 0, 0))],
        out_shape=[jax.ShapeDtypeStruct((b, t, RV_W), BF16),
                   jax.ShapeDtypeStruct((b, N_HEADS, RET_QK, RET_V), F32)],
        scratch_shapes=scratch,
        compiler_params=_params("parallel", "arbitrary"),
        name="retention",
    )(*args)


ATTN_ROWS = 2048
ROPE_CHUNK = 256


def _rows_at(start, size, stride):
    return pl.ds(start, size, stride=stride) if stride > 1 else pl.ds(start, size)


def _softmax_pv(q, keys, vals, valid):
    s = jnp.where(valid, _dot_nt(q, keys), NEG)
    m = jnp.max(s, axis=-1, keepdims=True)
    p = jnp.exp(s - m)
    den = jnp.sum(p, axis=-1, keepdims=True)
    return _dot((p * (1.0 / den)).astype(BF16), vals), m + jnp.log(den)


def _attn_p_kernel(*refs):
    ng = len(DIL_GROUPS)
    qkv = refs[0:3 * ng]
    yc_ref = refs[3 * ng]
    scr = refs[3 * ng + 1:]
    o_sc, lse_sc, kps, vps = [scr[n * ng:(n + 1) * ng] for n in range(4)]
    qrot, krs, vs = qkv[0::3], qkv[1::3], qkv[2::3]
    i = pl.program_id(2)
    rows, blk = ATTN_ROWS, ATTN_BLOCK

    @pl.when(i == 0)
    def _():
        for g in range(ng):
            kps[g][...] = jnp.zeros_like(kps[g])
            vps[g][...] = jnp.zeros_like(vps[g])

    qi = lax.broadcasted_iota(jnp.int32, (blk, 2 * blk), 0)
    ki = lax.broadcasted_iota(jnp.int32, (blk, 2 * blk), 1)
    back = blk + qi - ki
    band = (back >= 0) & (back <= blk)
    band_first = band & (ki >= jnp.where(i > 0, 0, blk))

    for g, (window, dil) in enumerate(DIL_GROUPS):
        assert window // dil == blk
        n_sub = rows // (dil * blk)

        for r in range(dil):
            for jb in range(n_sub):
                at = lambda j, n: _rows_at(r + j * blk * dil, n, dil)
                q = qrot[g][at(jb, blk), :].astype(BF16)
                if jb == 0:
                    keys = jnp.concatenate([kps[g][at(0, blk), :], krs[g][at(0, blk), :]], axis=0)
                    vals = jnp.concatenate([vps[g][at(0, blk), :], vs[g][at(0, blk), :]], axis=0)
                    valid = band_first
                else:
                    keys = krs[g][at(jb - 1, 2 * blk), :]
                    vals = vs[g][at(jb - 1, 2 * blk), :]
                    valid = band
                o, lse = _softmax_pv(q, keys.astype(BF16), vals.astype(BF16), valid)
                o_sc[g][at(jb, blk), :] = o
                lse_sc[g][at(jb, blk), :] = jnp.broadcast_to(lse, (blk, HEAD_DIM))

    for g, (window, dil) in enumerate(DIL_GROUPS):
        keep = dil * blk
        kps[g][...] = krs[g][rows - keep:, :]
        vps[g][...] = vs[g][rows - keep:, :]

    for c in range(0, rows, ROPE_CHUNK):
        rs = slice(c, c + ROPE_CHUNK)
        ls = [lse_sc[g][rs, :] for g in range(ng)]
        m = functools.reduce(jnp.maximum, ls)
        es = [jnp.exp(a - m) for a in ls]
        den = functools.reduce(lambda a, e: a + e, es)
        y = functools.reduce(lambda a, e: a + e, [(es[g] / den) * o_sc[g][rs, :] for g in range(ng)])
        yc_ref[rs, :] = y.astype(BF16)


def _attn_prompt(qkv_h, b, t):
    ng = len(DIL_GROUPS)
    m = b * t
    rows = ATTN_ROWS
    nblk = t // rows
    assert nblk * rows == t

    def head(n):
        return pl.BlockSpec((None, rows, HEAD_DIM), lambda i, h, j: (n * N_HEADS + h, i * nblk + j, 0))

    keeps = [dil * ATTN_BLOCK for _, dil in DIL_GROUPS]
    scratch = ([pltpu.VMEM((rows, HEAD_DIM), F32)] * (2 * ng)
               + [pltpu.VMEM((k, HEAD_DIM), F32) for k in keeps] * 2)
    return pl.pallas_call(
        _attn_p_kernel,
        grid=(b, N_HEADS, nblk),
        in_specs=[head(n) for n in range(3 * ng)],
        out_specs=pl.BlockSpec((rows, HEAD_DIM), lambda i, h, j: (i * nblk + j, h)),
        out_shape=jax.ShapeDtypeStruct((m, HW), BF16),
        scratch_shapes=scratch,
        compiler_params=_params("parallel", "parallel", "arbitrary"),
        name="attn_prompt",
    )(*([qkv_h] * (3 * ng)))


def _attn_s_kernel(*refs, t, lp, cache_rows):
    ng = len(DIL_GROUPS)
    qkv = refs[0:3 * ng]
    caches = refs[3 * ng:4 * ng]
    yc_ref = refs[4 * ng]
    knew = refs[4 * ng + 1:5 * ng + 1]
    qp, kp, vp = refs[5 * ng + 1:5 * ng + 4]
    qi = lax.broadcasted_iota(jnp.int32, (lp, lp), 0)
    kn_i = lax.broadcasted_iota(jnp.int32, (lp, lp), 1)
    outs = [[None] * ng for _ in range(N_HEADS)]
    lses = [[None] * ng for _ in range(N_HEADS)]
    for g, (window, dil) in enumerate(DIL_GROUPS):
        lb = cache_rows[g]
        for pad, src in ((qp, qkv[3 * g]), (kp, qkv[3 * g + 1]), (vp, qkv[3 * g + 2])):
            pad[...] = jnp.zeros_like(pad)
            pad[0:t, :] = src[...]
        cache = caches[g]
        qc = lax.broadcasted_iota(jnp.int32, (lp, lb), 0)
        kc_i = lax.broadcasted_iota(jnp.int32, (lp, lb), 1)
        back_c = lb + qc - kc_i
        valid_c = ((back_c & (dil - 1)) == 0) & (back_c <= window)
        back_n = qi - kn_i
        valid_n = (back_n >= 0) & ((back_n & (dil - 1)) == 0) & (back_n <= window) & (kn_i < t)
        for h in range(N_HEADS):
            hs = slice(h * HEAD_DIM, (h + 1) * HEAD_DIM)
            q = qp[:, hs].astype(BF16)
            k_new = kp[:, hs]
            v_new = vp[:, hs]
            for tt in range(t):
                knew[g][tt * KV_ROWS + h:tt * KV_ROWS + h + 1, :] = k_new[tt:tt + 1, :]
                knew[g][tt * KV_ROWS + N_HEADS + h:tt * KV_ROWS + N_HEADS + h + 1, :] = v_new[tt:tt + 1, :]
            k_old = cache[pl.ds(h, lb, stride=KV_ROWS), :].astype(BF16)
            v_old = cache[pl.ds(N_HEADS + h, lb, stride=KV_ROWS), :].astype(BF16)
            s_c = jnp.where(valid_c, _dot_nt(q, k_old), NEG)
            s_n = jnp.where(valid_n, _dot_nt(q, k_new.astype(BF16)), NEG)
            m = jnp.maximum(jnp.max(s_c, axis=-1, keepdims=True), jnp.max(s_n, axis=-1, keepdims=True))
            p_c = jnp.exp(s_c - m)
            p_n = jnp.exp(s_n - m)
            den = jnp.sum(p_c, axis=-1, keepdims=True) + jnp.sum(p_n, axis=-1, keepdims=True)
            o = (_dot((p_c / den).astype(BF16), v_old)
                 + _dot((p_n / den).astype(BF16), v_new.astype(BF16)))
            outs[h][g] = o
            lses[h][g] = m + jnp.log(den)
    for h in range(N_HEADS):
        hs = slice(h * HEAD_DIM, (h + 1) * HEAD_DIM)
        m = functools.reduce(jnp.maximum, lses[h])
        es = [jnp.exp(a - m) for a in lses[h]]
        den = functools.reduce(lambda a, c: a + c, es)
        y = functools.reduce(lambda a, c: a + c, [(e / den) * o for e, o in zip(es, outs[h])])
        yc_ref[:, hs] = y[0:t, :].astype(BF16)


def _attn_sample(proj3, caches, l, lp):
    b, t, _ = proj3.shape
    ng = len(DIL_GROUPS)
    cache_rows = tuple(c.shape[2] // KV_ROWS for c in caches)
    tok = lambda c0: pl.BlockSpec((None, t, HW), lambda i: (i, 0, c0))
    in_specs = [tok(n) for n in range(3 * ng)]
    in_specs += [pl.BlockSpec((None, None, lb * KV_ROWS, HEAD_DIM), lambda i: (l, i, 0, 0)) for lb in cache_rows]
    res = pl.pallas_call(
        functools.partial(_attn_s_kernel, t=t, lp=lp, cache_rows=cache_rows),
        grid=(b,),
        in_specs=in_specs,
        out_specs=[pl.BlockSpec((None, t, HW), lambda i: (i, 0, 0))]
                  + [pl.BlockSpec((None, t * KV_ROWS, HEAD_DIM), lambda i: (i, 0, 0))] * ng,
        out_shape=[jax.ShapeDtypeStruct((b, t, HW), BF16)]
                  + [jax.ShapeDtypeStruct((b, t * KV_ROWS, HEAD_DIM), F32)] * ng,
        scratch_shapes=[pltpu.VMEM((lp, HW), F32)] * 3,
        compiler_params=_params("parallel"),
        name="attn_sample",
    )(*([proj3] * (3 * ng)), *caches)
    return res[0], res[1:]


def _cache_shift_kernel(cur_ref, nxt_ref, new_ref, o_ref, *, ch, shift, n_chunks):
    c = pl.program_id(2)
    o_ref[0:ch - shift, :] = cur_ref[shift:ch, :]
    o_ref[ch - shift:ch, :] = jnp.where(c == n_chunks - 1, new_ref[...], nxt_ref[...])


def _cache_shift(cache, new):
    depth, b, rows, d = cache.shape
    shift = new.shape[2]
    ch = min(rows, 8192)
    n_chunks = rows // ch
    assert n_chunks * ch == rows and ch % shift == 0 and shift % 8 == 0
    per = ch // shift
    last = rows // shift - 1
    return pl.pallas_call(
        functools.partial(_cache_shift_kernel, ch=ch, shift=shift, n_chunks=n_chunks),
        grid=(depth, b, n_chunks),
        in_specs=[pl.BlockSpec((None, None, ch, d), lambda l, i, c: (l, i, c, 0)),
                  pl.BlockSpec((None, None, shift, d), lambda l, i, c: (l, i, jnp.minimum((c + 1) * per, last), 0)),
                  pl.BlockSpec((None, None, shift, d), lambda l, i, c: (l, i, 0, 0))],
        out_specs=pl.BlockSpec((None, None, ch, d), lambda l, i, c: (l, i, c, 0)),
        out_shape=jax.ShapeDtypeStruct(cache.shape, cache.dtype),
        compiler_params=_params("parallel", "parallel", "arbitrary"),
        name="cache_shift",
    )(cache, cache, new)


def _win_pack_kernel(k_ref, v_ref, o_ref, *, rows):
    for h in range(N_HEADS):
        o_ref[pl.ds(h, rows, stride=KV_ROWS), :] = k_ref[h]
        o_ref[pl.ds(N_HEADS + h, rows, stride=KV_ROWS), :] = v_ref[h]


def _win_pack(qkv_h, g, b, t, keep):
    ch = min(keep, 512)
    first = (t - keep) // ch
    per_b = t // ch
    k_blk, v_blk = 3 * g + 1, 3 * g + 2
    return pl.pallas_call(
        functools.partial(_win_pack_kernel, rows=ch),
        grid=(b, keep // ch),
        in_specs=[pl.BlockSpec((N_HEADS, ch, HEAD_DIM), lambda i, c: (k_blk, i * per_b + first + c, 0)),
                  pl.BlockSpec((N_HEADS, ch, HEAD_DIM), lambda i, c: (v_blk, i * per_b + first + c, 0))],
        out_specs=pl.BlockSpec((None, ch * KV_ROWS, HEAD_DIM), lambda i, c: (i, c, 0)),
        out_shape=jax.ShapeDtypeStruct((b, keep * KV_ROWS, HEAD_DIM), F32),
        compiler_params=_params("parallel", "parallel"),
        name="win_pack",
    )(qkv_h, qkv_h)


def _merge_kernel(h_ref, ya, yb, yc, hs_ref, ya_s, yb_s, yc_s, wga, wgb, wgc, wa, wb, wc, o_ref, os_ref, wg_ref):
    tn = o_ref.shape[1]
    for n, src in enumerate((wga, wgb, wgc)):
        wg_ref[:, n * tn:(n + 1) * tn] = src[...].astype(BF16)

    def merged(h, a, b, c):
        gates = jax.nn.sigmoid(_dot(h[...], wg_ref[...]))
        acc = gates[:, 0:tn] * _dot(a[...], wa[...])
        acc = acc + gates[:, tn:2 * tn] * _dot(b[...], wb[...])
        acc = acc + gates[:, 2 * tn:3 * tn] * _dot(c[...], wc[...])
        return acc.astype(BF16)

    o_ref[...] = merged(h_ref, ya, yb, yc)

    @pl.when(pl.program_id(0) == 0)
    def _():
        os_ref[0] = merged(hs_ref, ya_s, yb_s, yc_s)

    @pl.when(pl.program_id(0) > 0)
    def _():
        os_ref[...] = jnp.zeros_like(os_ref)


def _merge(h, ya, yb, yc, hs, ya_s, yb_s, yc_s, w_in, wa, wb, wc, l, tm, tn):
    m = h.shape[0]
    ms = hs.shape[0]
    act = lambda w: pl.BlockSpec((tm, w), lambda i, j: (i, 0))
    act_s = lambda w: pl.BlockSpec((ms, w), lambda i, j: (0, 0))
    wgt = lambda k: pl.BlockSpec((None, k, tn), lambda i, j: (l, 0, j))
    wgate = lambda n: pl.BlockSpec((None, D_MODEL, tn), lambda i, j: (l, 0, (OFF_GATE + n * D_MODEL) // tn + j))
    res = pl.pallas_call(
        _merge_kernel,
        grid=(m // tm, D_MODEL // tn),
        in_specs=[act(D_MODEL), act(CONV_DIM), act(RV_W), act(HW),
                  act_s(D_MODEL), act_s(CONV_DIM), act_s(RV_W), act_s(HW),
                  wgate(0), wgate(1), wgate(2), wgt(CONV_DIM), wgt(RV_W), wgt(HW)],
        out_specs=[pl.BlockSpec((tm, tn), lambda i, j: (i, j)),
                   pl.BlockSpec((1, ms, tn), lambda i, j: (i, 0, j))],
        out_shape=[jax.ShapeDtypeStruct((m, D_MODEL), BF16),
                   jax.ShapeDtypeStruct((m // tm, ms, D_MODEL), BF16)],
        scratch_shapes=[pltpu.VMEM((D_MODEL, 3 * tn), BF16)],
        compiler_params=_params("parallel", "parallel"),
        name="merge",
    )(h, ya, yb, yc, hs, ya_s, yb_s, yc_s, w_in, w_in, w_in, wa, wb, wc)
    return res[0], res[1][0]


def _post_norm_residual(x, y, gpost, gate):
    r = y * lax.rsqrt(jnp.mean(y * y, axis=-1, keepdims=True) + EPS) * gpost
    return x + gate * r


def _mod_rows(ref, rs):
    v = ref[0]
    return v if v.shape[0] == 1 else v[rs, :]


def _outproj_kernel(m_ref, ms_ref, w_ref, x_ref, xs_ref, gpost_ref, gffn_ref, g1_ref, sc2_ref, sh2_ref,
                    g1s_ref, sc2s_ref, sh2s_ref, xo_ref, h2_ref, xso_ref, h2s_ref, *, sub):
    def finish(rows_m, rows_x, g1, sc2, sh2):
        y = _dot(rows_m, w_ref[...])
        xn = _post_norm_residual(rows_x, y, gpost_ref[0], g1)
        return xn, _norm_mod(xn, gffn_ref[0], sc2, sh2).astype(BF16)

    tm = m_ref.shape[0]
    for r in range(0, tm, sub):
        rs = slice(r, r + sub)
        xo_ref[rs, :], h2_ref[rs, :] = finish(m_ref[rs, :], x_ref[rs, :], _mod_rows(g1_ref, rs),
                                              _mod_rows(sc2_ref, rs), _mod_rows(sh2_ref, rs))

    @pl.when(pl.program_id(0) == 0)
    def _():
        xso_ref[...], h2s_ref[...] = finish(ms_ref[...], xs_ref[...], g1s_ref[0], sc2s_ref[0], sh2s_ref[0])


def _outproj(merged, merged_s, w_out, l, x, x_s, gpost, gffn, mods, mods_s, rows):
    m, d = x.shape
    ms = x_s.shape[0]
    tm = min(rows.tm, 512)
    sub = _Rows(m, tm, rows.tiles_per_b * rows.tm)
    row = pl.BlockSpec((tm, d), lambda i: (i, 0))
    row_s = pl.BlockSpec((ms, d), lambda i: (0, 0))
    mod_s = pl.BlockSpec((1, ms, d), lambda i: (0, 0, 0))
    vec = pl.BlockSpec((1, 1, d), lambda i: (l, 0, 0))
    return pl.pallas_call(
        functools.partial(_outproj_kernel, sub=min(tm, 128)),
        grid=(m // tm,),
        in_specs=[row, row_s, pl.BlockSpec((None, d, d), lambda i: (l, 0, 0), pipeline_mode=pl.Buffered(1)),
                  row, row_s, vec, vec, sub.mod_spec(), sub.mod_spec(), sub.mod_spec(), mod_s, mod_s, mod_s],
        out_specs=[row, row, row_s, row_s],
        out_shape=[jax.ShapeDtypeStruct((m, d), F32), jax.ShapeDtypeStruct((m, d), BF16),
                   jax.ShapeDtypeStruct((ms, d), F32), jax.ShapeDtypeStruct((ms, d), BF16)],
        compiler_params=_params("arbitrary"),
        name="outproj",
    )(merged, merged_s, w_out, x, x_s, gpost, gffn, *mods, *mods_s)


def _down_kernel(*refs, nk, with_next, sub):
    n_in = 13 if with_next else 8
    n_out = 4 if with_next else 2
    ins, outs, (acc_ref, accs_ref) = refs[:n_in], refs[n_in:n_in + n_out], refs[n_in + n_out:]
    a_ref, as_ref, w_ref, x_ref, xs_ref, gpost_ref, g2_ref, g2s_ref = ins[:8]
    i = pl.program_id(0)
    k = pl.program_id(1)

    def finish(x_rows, y, g2_rows, pos, rs):
        xn = _post_norm_residual(x_rows, y, gpost_ref[0], g2_rows)
        outs[pos][rs, :] = xn
        if with_next:
            gn_ref, scn_ref, shn_ref = ins[8], ins[9 + pos], ins[11 + pos]
            outs[2 + pos][rs, :] = _norm_mod(xn, gn_ref[0], _mod_rows(scn_ref, rs), _mod_rows(shn_ref, rs)).astype(BF16)

    @pl.when(k == 0)
    def _():
        acc_ref[...] = jnp.zeros_like(acc_ref)

    @pl.when(k < nk - 1)
    def _():
        acc_ref[...] += _dot(a_ref[...], w_ref[...])

    @pl.when(k == nk - 1)
    def _():
        for r in range(0, a_ref.shape[0], sub):
            rs = slice(r, r + sub)
            y = acc_ref[rs, :] + _dot(a_ref[rs, :], w_ref[...])
            finish(x_ref[rs, :], y, _mod_rows(g2_ref, rs), 0, rs)

    @pl.when(i == 0)
    def _():
        @pl.when(k == 0)
        def _():
            accs_ref[...] = jnp.zeros_like(accs_ref)

        accs_ref[...] += _dot(as_ref[...], w_ref[...])

        @pl.when(k == nk - 1)
        def _():
            finish(xs_ref[...], accs_ref[...], g2s_ref[0], 1, slice(None))


def _down(a, a_s, w_down, l, x, x_s, gpost, g2, g2_s, nxt, rows):
    m, d = x.shape
    ms = x_s.shape[0]
    kdim = a.shape[1]
    tm = min(rows.tm, 1024)
    tk = 1024
    nk = kdim // tk
    sub = _Rows(m, tm, None if rows.per_row else rows.tiles_per_b * rows.tm)
    one = lambda shape, imap: pl.BlockSpec(shape, imap, pipeline_mode=pl.Buffered(1))
    row = one((tm, d), lambda i, k: (i, 0))
    row_s = pl.BlockSpec((ms, d), lambda i, k: (0, 0))
    mod_s = pl.BlockSpec((1, ms, d), lambda i, k: (0, 0, 0))
    vec = pl.BlockSpec((1, 1, d), lambda i, k: (l, 0, 0))
    in_specs = [pl.BlockSpec((tm, tk), lambda i, k: (i, k)),
                pl.BlockSpec((ms, tk), lambda i, k: (0, k)),
                pl.BlockSpec((None, tk, d), lambda i, k: (l, k, 0)),
                row, row_s, vec, sub.mod_spec(), mod_s]
    args = [a, a_s, w_down, x, x_s, gpost, g2, g2_s]
    out_specs = [row, row_s]
    out_shape = [jax.ShapeDtypeStruct((m, d), F32), jax.ShapeDtypeStruct((ms, d), F32)]
    if nxt is not None:
        gn, scn, shn, scn_s, shn_s = nxt
        in_specs += [pl.BlockSpec((1, 1, d), lambda i, k: (l + 1, 0, 0)), sub.mod_spec(), mod_s, sub.mod_spec(), mod_s]
        args += [gn, scn, scn_s, shn, shn_s]
        out_specs += [row, row_s]
        out_shape += [jax.ShapeDtypeStruct((m, d), BF16), jax.ShapeDtypeStruct((ms, d), BF16)]
    return pl.pallas_call(
        functools.partial(_down_kernel, nk=nk, with_next=nxt is not None, sub=min(tm, 256)),
        grid=(m // tm, nk),
        in_specs=in_specs,
        out_specs=out_specs,
        out_shape=out_shape,
        scratch_shapes=[pltpu.VMEM((tm, d), F32), pltpu.VMEM((ms, d), F32)],
        compiler_params=_params("arbitrary", "arbitrary", vmem=VMEM_LIMIT_MAX),
        name="mlp_down",
    )(*args)


def kernel(x_prompt, x_sample, c_prompt, c_sample, state_conv, state_retention, cache_win128, cache_win512, cache_win2048, norm_mix_g, norm_mix_post_g, norm_ffn_g, norm_ffn_post_g, w_mod, b_mod, w_in, conv_w, conv_b, conv_ln_g, conv_ln_b, w_branch_a, w_branch_b, w_branch_c, w_out, w_up, w_down):
    depth = w_in.shape[0]
    bp, tp, d = x_prompt.shape
    bs, ts, _ = x_sample.shape
    mp, ms = bp * tp, bs * ts
    ng = len(DIL_GROUPS)

    w_a_b, w_b_b, w_c_b, w_out_b, w_down_b = [
        w.astype(BF16) for w in (w_branch_a, w_branch_b, w_branch_c, w_out, w_down)]
    conv_w8 = jnp.broadcast_to(conv_w[:, :, None, :], (depth, CONV_WIDTH, SUBLANES, CONV_DIM))
    vec3 = lambda a: a.reshape(depth, 1, a.shape[-1])
    g_mix, g_mix_post, g_ffn, g_ffn_post = map(vec3, (norm_mix_g, norm_mix_post_g, norm_ffn_g, norm_ffn_post_g))
    conv_b3, ln_g3, ln_b3 = map(vec3, (conv_b, conv_ln_g, conv_ln_b))
    caches = [c.reshape(c.shape[0], c.shape[1], c.shape[2] * KV_ROWS, HEAD_DIM)
              for c in (cache_win128, cache_win512, cache_win2048)]

    lp_s = 16
    qkv_rope = _qkv_rope_tables(np.arange(tp)) + _qkv_rope_tables(np.tile(PAST_LEN + np.arange(ts), bs))
    retn_p = _rope_tables(np.arange(tp), _retnet_inv_freq(), tp)
    retn_s = _rope_tables(PAST_LEN + np.arange(ts), _retnet_inv_freq(), lp_s)

    n_c = bp + bs
    c_rows = -(-n_c // 8) * 8
    c_all = jnp.concatenate([c_prompt, c_sample, jnp.zeros((c_rows - n_c, d), F32)], axis=0)
    mod = _modulation(c_all, w_mod, b_mod)

    rows_p = _Rows(mp, 1024, tp)
    rows_s = _Rows(ms, ms, None)

    def mods(l):
        parts_p = [mod[l, :bp, i * d:(i + 1) * d].reshape(bp, 1, d) for i in range(6)]
        parts_s = [jnp.repeat(mod[l, bp:n_c, i * d:(i + 1) * d], ts, axis=0).reshape(1, ms, d) for i in range(6)]
        return parts_p, parts_s

    xp = x_prompt.reshape(mp, d)
    xs = x_sample.reshape(ms, d)
    mod_p, mod_s = mods(0)
    hp = _prenorm(xp, g_mix, 0, mod_p[1], mod_p[0], rows_p)
    hs = _prenorm(xs, g_mix, 0, mod_s[1], mod_s[0], rows_s)

    conv_p, conv_s, ret_p, ret_s = [], [], [], []
    win_p = [[] for _ in range(ng)]
    new_kv_s = [[] for _ in range(ng)]

    for l in range(depth):
        sh1p, sc1p, g1p, sh2p, sc2p, g2p = mod_p
        sh1s, sc1s, g1s, sh2s, sc2s, g2s = mod_s

        proj_ab, proj_ab_s = _matmul(hp, hs, w_in, l, F32, 2048, 512, ncols=OFF_DIL, name="matmul_in_ab")
        qkv_h, qkv_s = _matmul(hp, hs, w_in, l, F32, 2048, HW, col0=OFF_DIL, ncols=OFF_GATE - OFF_DIL,
                               heads=True, rope=qkv_rope, name="matmul_in_qkv")

        proj3 = proj_ab.reshape(bp, tp, OFF_DIL)
        ya, st = _conv_prompt(proj3, l, conv_w8, conv_b3, ln_g3, ln_b3)
        conv_p.append(st[:, CONV_HALO - (CONV_WIDTH - 1):])
        yb, s_new = _retention(proj3, retn_p[0], retn_p[1], None, l)
        ret_p.append(s_new)
        yc = _attn_prompt(qkv_h, bp, tp)
        for g, (window, dil) in enumerate(DIL_GROUPS):
            keep = min(window, tp)
            win_p[g].append(_win_pack(qkv_h, g, bp, tp, keep))
        proj_s3 = proj_ab_s.reshape(bs, ts, OFF_DIL)
        ya_s, st_s = _conv_sample(proj_s3, l, state_conv, conv_w, conv_b3, ln_g3, ln_b3)
        conv_s.append(st_s)
        yb_s, s_new_s = _retention(proj_s3, retn_s[0], retn_s[1], state_retention, l)
        ret_s.append(s_new_s)
        yc_s, kv_new = _attn_sample(qkv_s.reshape(bs, ts, OFF_GATE - OFF_DIL), caches, l, lp_s)
        for g in range(ng):
            new_kv_s[g].append(kv_new[g])
        merged, merged_s = _merge(hp, ya.reshape(mp, CONV_DIM), yb.reshape(mp, RV_W), yc,
                                  hs, ya_s.reshape(ms, CONV_DIM), yb_s.reshape(ms, RV_W), yc_s.reshape(ms, HW),
                                  w_in, w_a_b, w_b_b, w_c_b, l, 1024, 256)
        xp, h2, xs, h2_s = _outproj(merged, merged_s, w_out_b, l, xp, xs, g_mix_post, g_ffn,
                                    (g1p, sc2p, sh2p), (g1s, sc2s, sh2s), rows_p)
        a, a_s = _matmul(h2, h2_s, w_up, l, BF16, 2048, 1024, act="relu2", name="mlp_up")
        if l + 1 < depth:
            mod_p_n, mod_s_n = mods(l + 1)
            nxt = (g_mix, mod_p_n[1], mod_p_n[0], mod_s_n[1], mod_s_n[0])
            xp, xs, hp, hs = _down(a, a_s, w_down_b, l, xp, xs, g_ffn_post, g2p, g2s, nxt, rows_p)
            mod_p, mod_s = mod_p_n, mod_s_n
        else:
            xp, xs = _down(a, a_s, w_down_b, l, xp, xs, g_ffn_post, g2p, g2s, None, rows_p)

    win_s = [_cache_shift(caches[g], jnp.stack(new_kv_s[g])) for g in range(ng)]
    kv5 = lambda a: a.reshape(a.shape[0], a.shape[1], a.shape[2] // KV_ROWS, 2, N_HEADS, HEAD_DIM)
    win_p = [kv5(jnp.stack(w)) for w in win_p]
    win_s = [kv5(w) for w in win_s]

    return (xp.reshape(bp, tp, d), xs.reshape(bs, ts, d),
            jnp.stack(conv_p), jnp.stack(conv_s), jnp.stack(ret_p), jnp.stack(ret_s),
            win_p[0], win_s[0], win_p[1], win_s[1], win_p[2], win_s[2])
```
